```python
import math
import jax, jax.numpy as jnp
from jax import lax
import numpy as np

D_MODEL = 2048
BATCH = 2
SEQ = 16384
DEPTH = 2
DEC_BATCH = 32
DEC_SEQ = 64
PAST_LEN = 2048

CHUNK = 64
D_MIX = D_MODEL
CONV_K = 4
GDN_DK = 128
GDN_DV = 128
GDN_HEADS = (D_MIX // 2) // GDN_DV
GDN_WIDTH = GDN_HEADS * GDN_DV
GDN_CONV_DIM = GDN_HEADS * (2 * GDN_DK + GDN_DV)
S5_WIDTH = D_MIX // 4
S5_GROUP = 16
S5_GROUPS = S5_WIDTH // S5_GROUP
S5_STATE = 64
SSD_WIDTH = D_MIX - GDN_WIDTH - S5_WIDTH
SSD_HEADDIM = 64
SSD_HEADS = SSD_WIDTH // SSD_HEADDIM
SSD_NGROUPS = 2
SSD_STATE = 128
SSD_CONV_DIM = SSD_WIDTH + 2 * SSD_NGROUPS * SSD_STATE
OFF_Z_A = GDN_CONV_DIM
OFF_B_A = OFF_Z_A + GDN_WIDTH
OFF_A_A = OFF_B_A + GDN_HEADS
OFF_S5 = OFF_A_A + GDN_HEADS
OFF_Z_C = OFF_S5 + S5_WIDTH
OFF_XBC = OFF_Z_C + SSD_WIDTH
OFF_DT_C = OFF_XBC + SSD_CONV_DIM
D_IN = OFF_DT_C + SSD_HEADS
N_EXPERT_GROUPS = 4
EXPERTS_PER_GROUP = 8
N_EXPERTS = N_EXPERT_GROUPS * EXPERTS_PER_GROUP
TOP_K = 2
D_EXPERT = D_MODEL // 4
MOE_BLOCK = 128
EPS = 1e-6

kernel_name = 'hybrid_gdn_s5_ssd_hmoe_stream_step'

F32 = jnp.float32


def rmsnorm(x, w):
    xf = x.astype(F32)
    y = xf * lax.rsqrt(jnp.mean(xf * xf, axis=-1, keepdims=True) + EPS)
    return (y * w.astype(F32)).astype(x.dtype)


def norm_then_gate(y, z, w):
    y = y * lax.rsqrt(jnp.mean(y * y, axis=-1, keepdims=True) + EPS)
    return y * w.astype(F32) * jax.nn.silu(z)


def gate_then_norm(y, z, w):
    y = y * jax.nn.silu(z)
    return y * lax.rsqrt(jnp.mean(y * y, axis=-1, keepdims=True) + EPS) * w.astype(F32)


def l2norm(x):
    return x * lax.rsqrt(jnp.sum(x * x, axis=-1, keepdims=True) + 1e-6)


def causal_dwconv(xp, w):
    seq = xp.shape[1] - (CONV_K - 1)
    out = xp[:, 0:seq] * w[0]
    for j in range(1, CONV_K):
        out = out + xp[:, j:j + seq] * w[j]
    return out


def to_chunks(t, n_chunks):
    pad = n_chunks * CHUNK - t.shape[1]
    t = jnp.pad(t, [(0, 0), (0, pad)] + [(0, 0)] * (t.ndim - 2))
    return t.reshape((t.shape[0], n_chunks, CHUNK) + t.shape[2:])


def gated_delta_chunked(q, k, v, g, beta, s0):
    bsz, seq = q.shape[0], q.shape[1]
    nc = -(-seq // CHUNK)
    q, k, v = [jnp.swapaxes(to_chunks(t, nc), 2, 3) for t in (q, k, v)]
    g, beta = [jnp.swapaxes(to_chunks(t, nc), 2, 3) for t in (g, beta)]
    gc = jnp.cumsum(g, axis=-1)
    pos = jnp.arange(CHUNK)
    causal = pos[:, None] >= pos[None, :]
    strict = pos[:, None] > pos[None, :]
    decay = jnp.exp(jnp.where(causal, gc[..., :, None] - gc[..., None, :], -jnp.inf))
    kk = jnp.einsum('bchid,bchjd->bchij', k, k)
    m = jnp.where(strict, beta[..., :, None] * kk * decay, 0.0) + jnp.eye(CHUNK, dtype=F32)
    rhs = jnp.concatenate([v * beta[..., None], k * (beta * jnp.exp(gc))[..., None]], axis=-1)
    sol = lax.linalg.triangular_solve(m, rhs, left_side=True, lower=True, unit_diagonal=True)
    u, w = sol[..., :GDN_DV], sol[..., GDN_DV:]
    qk = jnp.einsum('bchid,bchjd->bchij', q, k) * decay
    q_dec = q * jnp.exp(gc)[..., None]
    g_last = gc[..., -1]
    k_dec = k * jnp.exp(g_last[..., None] - gc)[..., None]

    def step(s, xs):
        u_c, w_c, qk_c, qd_c, kd_c, gl_c = xs
        v_new = u_c - jnp.einsum('bhik,bhkv->bhiv', w_c, s)
        o_c = jnp.einsum('bhik,bhkv->bhiv', qd_c, s) + jnp.einsum('bhij,bhjv->bhiv', qk_c, v_new)
        s = s * jnp.exp(gl_c)[..., None, None] + jnp.einsum('bhik,bhiv->bhkv', kd_c, v_new)
        return s, o_c

    xs = tuple(jnp.moveaxis(t, 1, 0) for t in (u, w, qk, q_dec, k_dec, g_last))
    s_fin, o = lax.scan(step, s0, xs)
    o = jnp.transpose(o, (1, 0, 3, 2, 4)).reshape(bsz, nc * CHUNK, GDN_HEADS, GDN_DV)[:, :seq]
    return o, s_fin


def gdn_mixer(qkv_raw, z, b_raw, a_raw, conv_prev, s0, conv_w, a_log, dt_bias, norm_w):
    bsz, seq, _ = qkv_raw.shape
    xp = jnp.concatenate([conv_prev.astype(qkv_raw.dtype), qkv_raw], axis=1)
    qkv = jax.nn.silu(causal_dwconv(xp, conv_w).astype(F32))
    kd = GDN_HEADS * GDN_DK
    q = l2norm(qkv[..., :kd].reshape(bsz, seq, GDN_HEADS, GDN_DK)) * (GDN_DK ** -0.5)
    k = l2norm(qkv[..., kd:2 * kd].reshape(bsz, seq, GDN_HEADS, GDN_DK))
    v = qkv[..., 2 * kd:].reshape(bsz, seq, GDN_HEADS, GDN_DV)
    beta = jax.nn.sigmoid(b_raw.astype(F32))
    g = -jnp.exp(a_log.astype(F32)) * jax.nn.softplus(a_raw.astype(F32) + dt_bias.astype(F32))
    o, s_new = gated_delta_chunked(q, k, v, g, beta, s0.astype(F32))
    o = norm_then_gate(o, z.astype(F32).reshape(bsz, seq, GDN_HEADS, GDN_DV), norm_w)
    return (o.reshape(bsz, seq, GDN_WIDTH).astype(qkv_raw.dtype), xp[:, -(CONV_K - 1):],
            s_new.astype(qkv_raw.dtype))


def complex_affine_combine(e1, e2):
    a1r, a1i, b1r, b1i = e1
    a2r, a2i, b2r, b2i = e2
    return (a2r * a1r - a2i * a1i, a2r * a1i + a2i * a1r,
            a2r * b1r - a2i * b1i + b2r, a2r * b1i + a2i * b1r + b2i)


def s5_mixer(u, h0, a_re, a_im, b_re, b_im, c_re, c_im, log_dt, d_skip, w_glu, b_glu):
    bsz, seq, _ = u.shape
    uf = u.astype(F32).reshape(bsz, seq, S5_GROUPS, S5_GROUP)
    a_re, a_im = a_re.astype(F32), a_im.astype(F32)
    dt = jnp.exp(log_dt.astype(F32))[:, None]
    mag = jnp.exp(dt * a_re)
    abar_re, abar_im = mag * jnp.cos(dt * a_im), mag * jnp.sin(dt * a_im)
    den = a_re * a_re + a_im * a_im
    num_re, num_im = abar_re - 1.0, abar_im
    zoh_re = (num_re * a_re + num_im * a_im) / den
    zoh_im = (num_im * a_re - num_re * a_im) / den
    b_re, b_im = b_re.astype(F32), b_im.astype(F32)
    bbar_re = zoh_re[..., None] * b_re - zoh_im[..., None] * b_im
    bbar_im = zoh_re[..., None] * b_im + zoh_im[..., None] * b_re
    bu_re = jnp.einsum('blgc,gpc->blgp', uf, bbar_re)
    bu_im = jnp.einsum('blgc,gpc->blgp', uf, bbar_im)
    h0 = h0.astype(F32)
    h0_re, h0_im = h0[..., 0], h0[..., 1]
    bu_re = bu_re.at[:, 0].add(abar_re * h0_re - abar_im * h0_im)
    bu_im = bu_im.at[:, 0].add(abar_re * h0_im + abar_im * h0_re)
    a_seq_re = jnp.broadcast_to(abar_re, bu_re.shape)
    a_seq_im = jnp.broadcast_to(abar_im, bu_im.shape)
    _, _, h_re, h_im = lax.associative_scan(complex_affine_combine,
                                            (a_seq_re, a_seq_im, bu_re, bu_im), axis=1)
    y = (jnp.einsum('blgp,gcp->blgc', h_re, c_re.astype(F32))
         - jnp.einsum('blgp,gcp->blgc', h_im, c_im.astype(F32))
         + uf * d_skip.astype(F32).reshape(S5_GROUPS, S5_GROUP))
    y = jax.nn.gelu(y.reshape(bsz, seq, S5_WIDTH))
    out = y * jax.nn.sigmoid(y @ w_glu.astype(F32) + b_glu.astype(F32))
    h_new = jnp.stack([h_re[:, -1], h_im[:, -1]], axis=-1)
    return out.astype(u.dtype), h_new.astype(u.dtype)


def ssd_chunked(x, dt, a, bm, cm, s0):
    bsz, seq = x.shape[0], x.shape[1]
    nc = -(-seq // CHUNK)
    x, dt, bm, cm = [to_chunks(t, nc) for t in (x, dt, bm, cm)]
    cs = jnp.cumsum(dt * a, axis=2)
    xdt = x * dt[..., None]
    pos = jnp.arange(CHUNK)
    causal = (pos[:, None] >= pos[None, :])[:, :, None]
    seg = jnp.exp(jnp.where(causal, cs[:, :, :, None, :] - cs[:, :, None, :, :], -jnp.inf))
    scores = jnp.einsum('bclhn,bcshn->bclsh', cm, bm) * seg
    y_diag = jnp.einsum('bclsh,bcshp->bclhp', scores, xdt)
    chunk_states = jnp.einsum('bclhn,bclhp->bchpn', bm * jnp.exp(cs[:, :, -1:] - cs)[..., None], xdt)
    chunk_decay = jnp.exp(cs[:, :, -1])

    def step(s, xs):
        st, dec = xs
        return s * dec[..., None, None] + st, s

    s_fin, s_in = lax.scan(step, s0, (jnp.moveaxis(chunk_states, 1, 0), jnp.moveaxis(chunk_decay, 1, 0)))
    s_in = jnp.moveaxis(s_in, 0, 1)
    y_off = jnp.einsum('bclhn,bchpn->bclhp', cm * jnp.exp(cs)[..., None], s_in)
    y = (y_diag + y_off).reshape(bsz, nc * CHUNK, SSD_HEADS, SSD_HEADDIM)[:, :seq]
    return y, s_fin


def ssd_mixer(z, xbc_raw, dt_raw, conv_prev, s0, conv_w, conv_b, a_log, dt_bias, d_skip, norm_w):
    bsz, seq, _ = xbc_raw.shape
    xp = jnp.concatenate([conv_prev.astype(xbc_raw.dtype), xbc_raw], axis=1)
    xbc = jax.nn.silu((causal_dwconv(xp, conv_w) + conv_b).astype(F32))
    nb = SSD_NGROUPS * SSD_STATE
    rep = SSD_HEADS // SSD_NGROUPS
    x = xbc[..., :SSD_WIDTH].reshape(bsz, seq, SSD_HEADS, SSD_HEADDIM)
    bm = jnp.repeat(xbc[..., SSD_WIDTH:SSD_WIDTH + nb].reshape(bsz, seq, SSD_NGROUPS, SSD_STATE), rep, axis=2)
    cm = jnp.repeat(xbc[..., SSD_WIDTH + nb:].reshape(bsz, seq, SSD_NGROUPS, SSD_STATE), rep, axis=2)
    dt = jax.nn.softplus(dt_raw.astype(F32) + dt_bias.astype(F32))
    a = -jnp.exp(a_log.astype(F32))
    y, s_new = ssd_chunked(x, dt, a, bm, cm, s0.astype(F32))
    y = y + x * d_skip.astype(F32)[:, None]
    y = gate_then_norm(y.reshape(bsz, seq, SSD_WIDTH), z.astype(F32), norm_w)
    return y.astype(xbc_raw.dtype), xp[:, -(CONV_K - 1):], s_new.astype(xbc_raw.dtype)


def mixer_layer(h, conv_gdn, s_gdn, s_s5, conv_ssd, s_ssd, p, l):
    proj = h @ p['w_in'][l]
    qkv_raw = proj[..., :OFF_Z_A]
    z_a = proj[..., OFF_Z_A:OFF_B_A]
    b_a = proj[..., OFF_B_A:OFF_A_A]
    a_a = proj[..., OFF_A_A:OFF_S5]
    u_b = proj[..., OFF_S5:OFF_Z_C]
    z_c = proj[..., OFF_Z_C:OFF_XBC]
    xbc_raw = proj[..., OFF_XBC:OFF_DT_C]
    dt_c = proj[..., OFF_DT_C:]
    o_a, conv_gdn_new, s_gdn_new = gdn_mixer(
        qkv_raw, z_a, b_a, a_a, conv_gdn, s_gdn, p['gdn_conv_w'][l], p['gdn_a_log'][l],
        p['gdn_dt_bias'][l], p['gdn_norm'][l])
    o_b, s_s5_new = s5_mixer(
        u_b, s_s5, p['s5_a_re'][l], p['s5_a_im'][l], p['s5_b_re'][l], p['s5_b_im'][l],
        p['s5_c_re'][l], p['s5_c_im'][l], p['s5_log_dt'][l], p['s5_d'][l], p['s5_w_glu'][l],
        p['s5_b_glu'][l])
    o_c, conv_ssd_new, s_ssd_new = ssd_mixer(
        z_c, xbc_raw, dt_c, conv_ssd, s_ssd, p['ssd_conv_w'][l], p['ssd_conv_b'][l],
        p['ssd_a_log'][l], p['ssd_dt_bias'][l], p['ssd_d'][l], p['ssd_norm'][l])
    mix = jnp.concatenate([o_a, o_b, o_c], axis=-1).astype(h.dtype) @ p['w_out'][l]
    return mix, (conv_gdn_new, s_gdn_new, s_s5_new, conv_ssd_new, s_ssd_new)


def hier_moe(h, rg_w, rg_b, re_w, re_b, w_gate, w_up, w_down):
    bsz, seq, _ = h.shape
    t = h.reshape(-1, D_MODEL)
    n_tok = t.shape[0]
    group_p = jax.nn.softmax((t @ rg_w).astype(F32) + rg_b.astype(F32), axis=-1)
    gp_top, g_top = lax.top_k(group_p, 1)
    exp_logits = ((t @ re_w).astype(F32) + re_b.astype(F32)).reshape(n_tok, N_EXPERT_GROUPS, EXPERTS_PER_GROUP)
    in_group = exp_logits[jnp.arange(n_tok), g_top[:, 0]]
    e_top_p, e_top = lax.top_k(jax.nn.softmax(in_group, axis=-1), TOP_K)
    gate = gp_top * e_top_p / jnp.sum(e_top_p, axis=-1, keepdims=True)
    expert_idx = (g_top * EXPERTS_PER_GROUP + e_top).astype(jnp.int32)

    n_assign = n_tok * TOP_K
    flat_e = expert_idx.reshape(-1)
    order = jnp.argsort(flat_e)
    sorted_e = flat_e[order]
    counts = jnp.bincount(flat_e, length=N_EXPERTS)
    padded = (counts + MOE_BLOCK - 1) // MOE_BLOCK * MOE_BLOCK
    pad_end = jnp.cumsum(padded)
    pad_start = pad_end - padded
    start = jnp.cumsum(counts) - counts
    slot_sorted = pad_start[sorted_e] + jnp.arange(n_assign, dtype=jnp.int32) - start[sorted_e]
    slot = jnp.zeros((n_assign,), jnp.int32).at[order].set(slot_sorted.astype(jnp.int32))
    n_blocks = -(-n_assign // MOE_BLOCK) + N_EXPERTS
    slot_token = jnp.full((n_blocks * MOE_BLOCK,), n_tok, jnp.int32).at[slot].set(
        jnp.arange(n_assign, dtype=jnp.int32) // TOP_K)
    block_expert = jnp.minimum(
        jnp.searchsorted(pad_end, jnp.arange(n_blocks, dtype=jnp.int32) * MOE_BLOCK, side='right'),
        N_EXPERTS - 1)
    t_pad = jnp.concatenate([t, jnp.zeros((1, D_MODEL), t.dtype)], axis=0)
    xs = t_pad[slot_token].reshape(n_blocks, MOE_BLOCK, D_MODEL)

    def expert_block(args):
        xb, e = args
        return (jax.nn.silu(xb @ w_gate[e]) * (xb @ w_up[e])) @ w_down[e]

    ys = lax.map(expert_block, (xs, block_expert)).reshape(n_blocks * MOE_BLOCK, D_MODEL)
    y = jnp.sum(ys[slot].astype(F32).reshape(n_tok, TOP_K, D_MODEL) * gate[..., None], axis=1)
    return y.astype(h.dtype).reshape(bsz, seq, D_MODEL)


def trunk(x, states, p):
    conv_gdn, s_gdn, s_s5, conv_ssd, s_ssd = states
    new = []
    for l in range(DEPTH):
        mix, st = mixer_layer(rmsnorm(x, p['norm_mix'][l]), conv_gdn[l], s_gdn[l], s_s5[l],
                              conv_ssd[l], s_ssd[l], p, l)
        x = x + mix
        x = x + hier_moe(rmsnorm(x, p['norm_ffn'][l]), p['router_group_w'][l], p['router_group_b'][l],
                         p['router_expert_w'][l], p['router_expert_b'][l], p['expert_w_gate'][l],
                         p['expert_w_up'][l], p['expert_w_down'][l])
        new.append(st)
    stacked = tuple(jnp.stack([st[i] for st in new]) for i in range(5))
    return rmsnorm(x, p['norm_final']), stacked


def setup_inputs(seed: int = 0) -> dict:
    key = jax.random.key(seed)
    ks = iter(jax.random.split(key, 48))

    def nrm(shape, scale=1.0):
        return jax.random.normal(next(ks), shape, F32) * scale

    def unif(shape, lo, hi):
        return jax.random.uniform(next(ks), shape, F32, lo, hi)

    def inv_softplus(y):
        return y + jnp.log(-jnp.expm1(-y))

    lo_dt, hi_dt = math.log(1e-3), math.log(1e-1)
    x_prompt = nrm((BATCH, SEQ, D_MODEL))
    x_sample = nrm((DEC_BATCH, DEC_SEQ, D_MODEL))
    cache_conv_gdn = nrm((DEPTH, DEC_BATCH, CONV_K - 1, GDN_CONV_DIM))
    state_gdn = nrm((DEPTH, DEC_BATCH, GDN_HEADS, GDN_DK, GDN_DV), 0.1)
    state_s5 = nrm((DEPTH, DEC_BATCH, S5_GROUPS, S5_STATE, 2), 0.1)
    cache_conv_ssd = nrm((DEPTH, DEC_BATCH, CONV_K - 1, SSD_CONV_DIM))
    state_ssd = nrm((DEPTH, DEC_BATCH, SSD_HEADS, SSD_HEADDIM, SSD_STATE), 0.1)
    norm_mix = 1.0 + nrm((DEPTH, D_MODEL), 0.01)
    w_in = nrm((DEPTH, D_MODEL, D_IN), D_MODEL ** -0.5)
    gdn_conv_w = nrm((DEPTH, CONV_K, GDN_CONV_DIM), CONV_K ** -0.5)
    gdn_a_log = jnp.log(unif((DEPTH, GDN_HEADS), 1.0, 16.0))
    gdn_dt_bias = inv_softplus(jnp.exp(unif((DEPTH, GDN_HEADS), lo_dt, hi_dt)))
    gdn_norm = 1.0 + nrm((DEPTH, GDN_DV), 0.01)
    s5_a_re = -0.5 + nrm((DEPTH, S5_GROUPS, S5_STATE), 0.01)
    s5_a_im = math.pi * jnp.arange(S5_STATE, dtype=F32) + nrm((DEPTH, S5_GROUPS, S5_STATE), 0.01)
    s5_b_re = nrm((DEPTH, S5_GROUPS, S5_STATE, S5_GROUP), (2 * S5_GROUP) ** -0.5)
    s5_b_im = nrm((DEPTH, S5_GROUPS, S5_STATE, S5_GROUP), (2 * S5_GROUP) ** -0.5)
    s5_c_re = nrm((DEPTH, S5_GROUPS, S5_GROUP, S5_STATE), (2 * S5_STATE) ** -0.5)
    s5_c_im = nrm((DEPTH, S5_GROUPS, S5_GROUP, S5_STATE), (2 * S5_STATE) ** -0.5)
    s5_log_dt = unif((DEPTH, S5_GROUPS), lo_dt, hi_dt)
    s5_d = nrm((DEPTH, S5_WIDTH))
    s5_w_glu = nrm((DEPTH, S5_WIDTH, S5_WIDTH), S5_WIDTH ** -0.5)
    s5_b_glu = nrm((DEPTH, S5_WIDTH), 0.01)
    ssd_conv_w = nrm((DEPTH, CONV_K, SSD_CONV_DIM), CONV_K ** -0.5)
    ssd_conv_b = nrm((DEPTH, SSD_CONV_DIM), 0.01)
    ssd_a_log = jnp.log(unif((DEPTH, SSD_HEADS), 1.0, 16.0))
    ssd_dt_bias = inv_softplus(jnp.exp(unif((DEPTH, SSD_HEADS), lo_dt, hi_dt)))
    ssd_d = 1.0 + nrm((DEPTH, SSD_HEADS), 0.01)
    ssd_norm = 1.0 + nrm((DEPTH, SSD_WIDTH), 0.01)
    w_out = nrm((DEPTH, D_MIX, D_MODEL), D_MIX ** -0.5)
    norm_ffn = 1.0 + nrm((DEPTH, D_MODEL), 0.01)
    router_group_w = nrm((DEPTH, D_MODEL, N_EXPERT_GROUPS), D_MODEL ** -0.5)
    router_group_b = nrm((DEPTH, N_EXPERT_GROUPS), 0.01)
    router_expert_w = nrm((DEPTH, D_MODEL, N_EXPERTS), D_MODEL ** -0.5)
    router_expert_b = nrm((DEPTH, N_EXPERTS), 0.01)
    expert_w_gate = nrm((DEPTH, N_EXPERTS, D_MODEL, D_EXPERT), D_MODEL ** -0.5)
    expert_w_up = nrm((DEPTH, N_EXPERTS, D_MODEL, D_EXPERT), D_MODEL ** -0.5)
    expert_w_down = nrm((DEPTH, N_EXPERTS, D_EXPERT, D_MODEL), D_EXPERT ** -0.5)
    norm_final = 1.0 + nrm((D_MODEL,), 0.01)
    return {
        'x_prompt': x_prompt, 'x_sample': x_sample,
        'cache_conv_gdn': cache_conv_gdn, 'state_gdn': state_gdn, 'state_s5': state_s5,
        'cache_conv_ssd': cache_conv_ssd, 'state_ssd': state_ssd,
        'norm_mix': norm_mix, 'w_in': w_in,
        'gdn_conv_w': gdn_conv_w, 'gdn_a_log': gdn_a_log, 'gdn_dt_bias': gdn_dt_bias, 'gdn_norm': gdn_norm,
        's5_a_re': s5_a_re, 's5_a_im': s5_a_im, 's5_b_re': s5_b_re, 's5_b_im': s5_b_im,
        's5_c_re': s5_c_re, 's5_c_im': s5_c_im, 's5_log_dt': s5_log_dt, 's5_d': s5_d,
        's5_w_glu': s5_w_glu, 's5_b_glu': s5_b_glu,
        'ssd_conv_w': ssd_conv_w, 'ssd_conv_b': ssd_conv_b, 'ssd_a_log': ssd_a_log,
        'ssd_dt_bias': ssd_dt_bias, 'ssd_d': ssd_d, 'ssd_norm': ssd_norm,
        'w_out': w_out, 'norm_ffn': norm_ffn,
        'router_group_w': router_group_w, 'router_group_b': router_group_b,
        'router_expert_w': router_expert_w, 'router_expert_b': router_expert_b,
        'expert_w_gate': expert_w_gate, 'expert_w_up': expert_w_up, 'expert_w_down': expert_w_down,
        'norm_final': norm_final,
    }


def reference(x_prompt, x_sample, cache_conv_gdn, state_gdn, state_s5, cache_conv_ssd, state_ssd,
              norm_mix, w_in, gdn_conv_w, gdn_a_log, gdn_dt_bias, gdn_norm,
              s5_a_re, s5_a_im, s5_b_re, s5_b_im, s5_c_re, s5_c_im, s5_log_dt, s5_d, s5_w_glu, s5_b_glu,
              ssd_conv_w, ssd_conv_b, ssd_a_log, ssd_dt_bias, ssd_d, ssd_norm,
              w_out, norm_ffn, router_group_w, router_group_b, router_expert_w, router_expert_b,
              expert_w_gate, expert_w_up, expert_w_down, norm_final):
    p = {
        'norm_mix': norm_mix, 'w_in': w_in,
        'gdn_conv_w': gdn_conv_w, 'gdn_a_log': gdn_a_log, 'gdn_dt_bias': gdn_dt_bias, 'gdn_norm': gdn_norm,
        's5_a_re': s5_a_re, 's5_a_im': s5_a_im, 's5_b_re': s5_b_re, 's5_b_im': s5_b_im,
        's5_c_re': s5_c_re, 's5_c_im': s5_c_im, 's5_log_dt': s5_log_dt, 's5_d': s5_d,
        's5_w_glu': s5_w_glu, 's5_b_glu': s5_b_glu,
        'ssd_conv_w': ssd_conv_w, 'ssd_conv_b': ssd_conv_b, 'ssd_a_log': ssd_a_log,
        'ssd_dt_bias': ssd_dt_bias, 'ssd_d': ssd_d, 'ssd_norm': ssd_norm,
        'w_out': w_out, 'norm_ffn': norm_ffn,
        'router_group_w': router_group_w, 'router_group_b': router_group_b,
        'router_expert_w': router_expert_w, 'router_expert_b': router_expert_b,
        'expert_w_gate': expert_w_gate, 'expert_w_up': expert_w_up, 'expert_w_down': expert_w_down,
        'norm_final': norm_final,
    }
    nb = x_prompt.shape[0]
    dtp = x_prompt.dtype
    empty = (jnp.zeros((DEPTH, nb, CONV_K - 1, GDN_CONV_DIM), dtp),
             jnp.zeros((DEPTH, nb, GDN_HEADS, GDN_DK, GDN_DV), dtp),
             jnp.zeros((DEPTH, nb, S5_GROUPS, S5_STATE, 2), dtp),
             jnp.zeros((DEPTH, nb, CONV_K - 1, SSD_CONV_DIM), dtp),
             jnp.zeros((DEPTH, nb, SSD_HEADS, SSD_HEADDIM, SSD_STATE), dtp))
    y_prompt, st_p = trunk(x_prompt, empty, p)
    conv_gdn_p, state_gdn_p, state_s5_p, conv_ssd_p, state_ssd_p = st_p
    y_sample, st_s = trunk(x_sample, (cache_conv_gdn, state_gdn, state_s5, cache_conv_ssd, state_ssd), p)
    conv_gdn_s, state_gdn_s, state_s5_s, conv_ssd_s, state_ssd_s = st_s
    return (y_prompt, y_sample, conv_gdn_p, state_gdn_p, state_s5_p, conv_ssd_p, state_ssd_p,
            conv_gdn_s, state_gdn_s, state_s5_s, conv_ssd_s, state_ssd_s)
```

```python
import functools
import math

import numpy as np
import jax
import jax.numpy as jnp
from jax import lax
from jax.experimental import pallas as pl
from jax.experimental.pallas import tpu as pltpu

F32 = jnp.float32
BF16 = jnp.bfloat16
I32 = jnp.int32

EPS = 1e-6
CHUNK = 64
CONV_K = 4
LANES = 128
SUBLANES = 8
VMEM_LIMIT = 56 * 1024 * 1024

GDN_HEADS = 8
GDN_DK = 128
GDN_DV = 128
GDN_KD = GDN_HEADS * GDN_DK
GDN_CONV = 3 * GDN_KD
S5_WIDTH = 512
S5_GROUPS = 32
S5_GROUP = 16
S5_STATE = 64
S5_N = S5_GROUPS * S5_STATE
SSD_WIDTH = 512
SSD_HEADS = 8
SSD_HEADDIM = 64
SSD_NGROUPS = 2
SSD_STATE = 128
SSD_CONV = SSD_WIDTH + 2 * SSD_NGROUPS * SSD_STATE
N_GROUPS = 4
EPG = 8
N_EXPERTS = 32

MIX_TILE = 512
TILE_CHUNKS = MIX_TILE // CHUNK


def _cparams(sem, vmem=VMEM_LIMIT):
    return pltpu.CompilerParams(dimension_semantics=sem, vmem_limit_bytes=vmem)


def _silu(x):
    return x * jax.nn.sigmoid(x)


def _softplus(x):
    return jnp.maximum(x, 0.0) + jnp.log1p(jnp.exp(-jnp.abs(x)))


def _dot(a, b):
    return jnp.dot(a, b, preferred_element_type=F32)


def _dot3(a, b):
    ah = a.astype(BF16)
    bh = b.astype(BF16)
    al = (a - ah.astype(F32)).astype(BF16)
    bl = (b - bh.astype(F32)).astype(BF16)
    return _dot(ah, bh) + (_dot(ah, bl) + _dot(al, bh))


def _dot_nt(a, b):
    return lax.dot_general(a, b, (((1,), (1,)), ((), ())), preferred_element_type=F32)


def _dot_tn(a, b):
    return lax.dot_general(a, b, (((0,), (0,)), ((), ())), preferred_element_type=F32)


def _cumsum_rows(x):
    row = lax.broadcasted_iota(I32, x.shape, 0) & (CHUNK - 1)
    k = 1
    while k < CHUNK:
        x = x + jnp.where(row >= k, pltpu.roll(x, k, 0), 0.0)
        k *= 2
    return x


def _cumsum_lanes(x):
    lane = lax.broadcasted_iota(I32, x.shape, 1) & (CHUNK - 1)
    k = 1
    while k < CHUNK:
        x = x + jnp.where(lane >= k, pltpu.roll(x, k, 1), 0.0)
        k *= 2
    return x


PROJ_TM = 1024
PROJ_TN = 1280
P_QKV, P_ZA, P_UB, P_ZC, P_XBC = 0, 3072, 4096, 4608, 5120
P_MAIN = 6144
P_TOTAL = 6400
P_SMALL = P_TOTAL - LANES


def _proj_kernel(x_ref, nw_ref, w_ref, o_ref, h_scr):
    @pl.when(pl.program_id(1) == 0)
    def _():
        x = x_ref[...]
        ms = jnp.mean(x * x, axis=-1, keepdims=True)
        h_scr[...] = (x * lax.rsqrt(ms + EPS) * nw_ref[...]).astype(BF16)

    o_ref[...] = _dot(h_scr[...], w_ref[...])


def _proj(x, nw, w):
    t, d = x.shape
    n = w.shape[1]
    tm = min(PROJ_TM, t)
    assert t % tm == 0 and n % PROJ_TN == 0
    return pl.pallas_call(
        _proj_kernel,
        grid=(t // tm, n // PROJ_TN),
        in_specs=[pl.BlockSpec((tm, d), lambda i, j: (i, 0)),
                  pl.BlockSpec((1, d), lambda i, j: (0, 0)),
                  pl.BlockSpec((d, PROJ_TN), lambda i, j: (0, j))],
        out_specs=pl.BlockSpec((tm, PROJ_TN), lambda i, j: (i, j)),
        out_shape=jax.ShapeDtypeStruct((t, n), F32),
        scratch_shapes=[pltpu.VMEM((tm, d), BF16)],
        compiler_params=_cparams(("parallel", "arbitrary")),
        name="proj_in",
    )(x, nw, w)


def _rearrange_w_in(w_in):
    d = w_in.shape[0]
    off_za = GDN_CONV
    off_ba = off_za + GDN_KD
    off_s5 = off_ba + 2 * GDN_HEADS
    off_zc = off_s5 + S5_WIDTH
    off_xbc = off_zc + SSD_WIDTH
    off_dt = off_xbc + SSD_CONV
    small = jnp.concatenate([w_in[:, off_ba:off_s5], w_in[:, off_dt:],
                             jnp.zeros((d, LANES - 3 * GDN_HEADS), w_in.dtype)], axis=1)
    w = jnp.concatenate([w_in[:, :off_ba], w_in[:, off_s5:off_dt],
                         jnp.zeros((d, P_TOTAL - P_MAIN - LANES), w_in.dtype), small], axis=1)
    return w.astype(BF16)


def _chunk_tables(nbp, seq, nbs, dseq):
    assert seq % MIX_TILE == 0 and dseq == CHUNK and (nbs * dseq) % MIX_TILE == 0
    cps = seq // CHUNK
    ncp = nbp * cps
    nc = ncp + nbs
    kind = np.zeros((nc,), np.int32)
    emit = np.full((nc,), -1, np.int32)
    for c in range(nc):
        if c < ncp:
            kind[c] = 1 if c % cps == 0 else 0
            if c % TILE_CHUNKS == TILE_CHUNKS - 1:
                emit[c] = 0
        else:
            kind[c] = 1
            emit[c] = (c - ncp) % TILE_CHUNKS
    ntp = ncp // TILE_CHUNKS
    nts = nbs // TILE_CHUNKS
    in_blk = np.concatenate([np.zeros((ntp,), np.int32), 1 + np.arange(nts, dtype=np.int32)])
    out_blk = np.concatenate([np.arange(ntp, dtype=np.int32) // (cps // TILE_CHUNKS),
                              nbp + np.arange(nts, dtype=np.int32)])
    return kind, emit, in_blk, out_blk


def _init_slots(x):
    return jnp.concatenate([jnp.zeros((TILE_CHUNKS,) + x.shape[1:], x.dtype), x], axis=0)


def _conv_cache_slots(cache):
    nbs, k1, c = cache.shape
    padded = jnp.concatenate([jnp.zeros((nbs, SUBLANES - k1, c), cache.dtype), cache], axis=1)
    return _init_slots(padded)


def _conv_silu_tile(i, kind_ref, in_ref, cache_ref, cw_ref, tail, act, tmp, bias_row):
    l, c = in_ref.shape
    cb = 512
    k1 = CONV_K - 1
    nch = l // CHUNK

    @pl.when(i == 0)
    def _():
        tail[...] = jnp.zeros(tail.shape, F32)

    def taps(src, lo, hi, c0):
        acc = src[lo - k1:hi - k1, c0:c0 + cb] * cw_ref[0:1, c0:c0 + cb]
        for j in range(1, CONV_K):
            acc = acc + src[lo - k1 + j:hi - k1 + j, c0:c0 + cb] * cw_ref[j:j + 1, c0:c0 + cb]
        if bias_row is not None:
            acc = acc + cw_ref[bias_row:bias_row + 1, c0:c0 + cb]
        return _silu(acc)

    def head_rows(prev, r0):
        tmp[0:SUBLANES, :] = prev
        tmp[SUBLANES:2 * SUBLANES, :] = in_ref[r0:r0 + SUBLANES, :]
        for c0 in range(0, c, cb):
            act[r0:r0 + SUBLANES, c0:c0 + cb] = taps(tmp, SUBLANES, 2 * SUBLANES, c0)

    for rb in range(nch):
        lo = SUBLANES if rb == 0 else rb * CHUNK
        for c0 in range(0, c, cb):
            act[lo:(rb + 1) * CHUNK, c0:c0 + cb] = taps(in_ref, lo, (rb + 1) * CHUNK, c0)
    head_rows(tail[...], 0)

    for cl in range(nch):
        @pl.when(kind_ref[i * nch + cl] == 1)
        def _():
            head_rows(cache_ref[cl], cl * CHUNK)

    tail[...] = in_ref[l - SUBLANES:l, :]


def _gdn_kernel(kind_ref, emit_ref, inb_ref, outb_ref,
                qkv_ref, z_ref, sm_ref, cache_ref, s0_ref, cw_ref, parr_ref, parc_ref, nw_ref,
                o_ref, sout_ref,
                tail, act, tmp, state, gcol, beta_s, grow):
    del inb_ref, outb_ref
    i = pl.program_id(0)
    l = qkv_ref.shape[0]
    nch = l // CHUNK
    h_n, dk = GDN_HEADS, GDN_DK

    _conv_silu_tile(i, kind_ref, qkv_ref, cache_ref, cw_ref, tail, act, tmp, None)

    sm = sm_ref[...]
    beta_s[...] = jax.nn.sigmoid(sm)
    g = parr_ref[0:1, :] * _softplus(sm + parr_ref[1:2, :])
    gcol[...] = _cumsum_rows(g)
    a_t = sm.T[h_n:2 * h_n, :]
    g_t = parc_ref[0:h_n, 0:1] * _softplus(a_t + parc_ref[h_n:2 * h_n, 0:1])
    g_t = _cumsum_lanes(g_t)
    for cl in range(nch):
        grow[cl] = g_t[:, cl * CHUNK:(cl + 1) * CHUNK]

    sout_ref[1:, :, :, :] = jnp.zeros((nch - 1,) + tuple(sout_ref.shape[1:]), F32)

    ri = lax.broadcasted_iota(I32, (CHUNK, CHUNK), 0)
    ci = lax.broadcasted_iota(I32, (CHUNK, CHUNK), 1)
    causal = ri >= ci
    strict = ri > ci

    def chunk(c, carry):
        base = pl.multiple_of(c * CHUNK, CHUNK)
        gi = i * nch + c

        @pl.when(kind_ref[gi] == 1)
        def _():
            state[...] = s0_ref[c]

        gc_blk = gcol[pl.ds(base, CHUNK), :]
        bt_blk = beta_s[pl.ds(base, CHUNK), :]
        gr_blk = grow[c]
        for h in range(h_n):
            q = act[pl.ds(base, CHUNK), h * dk:(h + 1) * dk]
            k = act[pl.ds(base, CHUNK), GDN_KD + h * dk:GDN_KD + (h + 1) * dk]
            v = act[pl.ds(base, CHUNK), 2 * GDN_KD + h * dk:2 * GDN_KD + (h + 1) * dk]
            q = q * lax.rsqrt(jnp.sum(q * q, axis=-1, keepdims=True) + 1e-6) * (dk ** -0.5)
            k = k * lax.rsqrt(jnp.sum(k * k, axis=-1, keepdims=True) + 1e-6)
            gcl = gc_blk[:, h_n + h:h_n + h + 1]
            btl = bt_blk[:, h:h + 1]
            grw = gr_blk[h:h + 1, :]
            glast = gc_blk[CHUNK - 1:CHUNK, h_n + h:h_n + h + 1]
            eg = jnp.exp(gcl)
            decay = jnp.where(causal, jnp.exp(jnp.minimum(gcl - grw, 0.0)), 0.0)
            kb = k.astype(BF16)
            kk = _dot_nt(kb, kb)
            xm = jnp.where(strict, -(btl * kk * decay), 0.0)
            y = jnp.concatenate([v * btl, k * (btl * eg)], axis=1)
            p = xm
            for step in range(6):
                if step < 2:
                    y = y + _dot3(p, y)
                else:
                    y = y + _dot(p.astype(BF16), y.astype(BF16))
                if step == 0:
                    p = _dot3(p, p)
                elif step < 5:
                    pb = p.astype(BF16)
                    p = _dot(pb, pb)
            u = y[:, :GDN_DV]
            w = y[:, GDN_DV:]
            qk = _dot_nt(q.astype(BF16), kb) * decay
            qd = q * eg
            kd = k * jnp.exp(glast - gcl)
            s = state[h]
            sb = s.astype(BF16)
            v_new = u - _dot(w.astype(BF16), sb)
            vb = v_new.astype(BF16)
            o = _dot(qd.astype(BF16), sb) + _dot(qk.astype(BF16), vb)
            state[h] = s * jnp.exp(glast) + _dot_tn(kd.astype(BF16), vb)
            zz = z_ref[pl.ds(base, CHUNK), h * GDN_DV:(h + 1) * GDN_DV].astype(F32)
            on = o * lax.rsqrt(jnp.mean(o * o, axis=-1, keepdims=True) + EPS) * nw_ref[...] * _silu(zz)
            o_ref[pl.ds(base, CHUNK), h * GDN_DV:(h + 1) * GDN_DV] = on.astype(BF16)

        @pl.when(emit_ref[gi] >= 0)
        def _():
            sout_ref[emit_ref[gi]] = state[...]

        return carry

    lax.fori_loop(0, nch, chunk, 0)


def _gdn(proj, cache_slots, s0_slots, conv_w, a_log, dt_bias, norm_w, tables, n_out_blk):
    t = proj.shape[0]
    l = MIX_TILE
    nch = TILE_CHUNKS
    kind, emit, in_blk, out_blk = tables
    cw = jnp.concatenate([conv_w, jnp.zeros((SUBLANES - CONV_K, GDN_CONV), F32)], axis=0)
    a_neg = -jnp.exp(a_log.astype(F32))
    lane_pad = jnp.zeros((LANES - 2 * GDN_HEADS,), F32)
    parr = jnp.zeros((SUBLANES, LANES), F32)
    parr = parr.at[0].set(jnp.concatenate([jnp.zeros((GDN_HEADS,), F32), a_neg, lane_pad]))
    parr = parr.at[1].set(jnp.concatenate([jnp.zeros((GDN_HEADS,), F32), dt_bias.astype(F32), lane_pad]))
    parc = jnp.broadcast_to(jnp.concatenate([a_neg, dt_bias.astype(F32)])[:, None], (2 * GDN_HEADS, LANES))
    grid_spec = pltpu.PrefetchScalarGridSpec(
        num_scalar_prefetch=4,
        grid=(t // l,),
        in_specs=[
            pl.BlockSpec((l, GDN_CONV), lambda i, *_: (i, 0)),
            pl.BlockSpec((l, GDN_KD), lambda i, *_: (i, P_ZA // GDN_KD)),
            pl.BlockSpec((l, LANES), lambda i, *_: (i, P_SMALL // LANES)),
            pl.BlockSpec((nch, SUBLANES, GDN_CONV), lambda i, k, e, ib, ob: (ib[i], 0, 0)),
            pl.BlockSpec((nch, GDN_HEADS, GDN_DK, GDN_DV), lambda i, k, e, ib, ob: (ib[i], 0, 0, 0)),
            pl.BlockSpec((SUBLANES, GDN_CONV), lambda i, *_: (0, 0)),
            pl.BlockSpec((SUBLANES, LANES), lambda i, *_: (0, 0)),
            pl.BlockSpec((2 * GDN_HEADS, LANES), lambda i, *_: (0, 0)),
            pl.BlockSpec((1, GDN_DV), lambda i, *_: (0, 0)),
        ],
        out_specs=[
            pl.BlockSpec((l, GDN_KD), lambda i, *_: (i, 0)),
            pl.BlockSpec((nch, GDN_HEADS, GDN_DK, GDN_DV), lambda i, k, e, ib, ob: (ob[i], 0, 0, 0)),
        ],
        scratch_shapes=[
            pltpu.VMEM((SUBLANES, GDN_CONV), F32),
            pltpu.VMEM((l, GDN_CONV), F32),
            pltpu.VMEM((2 * SUBLANES, GDN_CONV), F32),
            pltpu.VMEM((GDN_HEADS, GDN_DK, GDN_DV), F32),
            pltpu.VMEM((l, LANES), F32),
            pltpu.VMEM((l, LANES), F32),
            pltpu.VMEM((nch, GDN_HEADS, CHUNK), F32),
        ],
    )
    return pl.pallas_call(
        _gdn_kernel,
        grid_spec=grid_spec,
        out_shape=[jax.ShapeDtypeStruct((t, GDN_KD), BF16),
                   jax.ShapeDtypeStruct((n_out_blk * nch, GDN_HEADS, GDN_DK, GDN_DV), F32)],
        compiler_params=_cparams(("arbitrary",)),
        name="gdn_mixer",
    )(jnp.asarray(kind), jnp.asarray(emit), jnp.asarray(in_blk), jnp.asarray(out_blk),
      proj, proj, proj, cache_slots, s0_slots, cw, parr, parc, norm_w.reshape(1, GDN_DV).astype(F32))


SSD_PAIRS = SSD_HEADS // 2
SM_DT = 2 * GDN_HEADS


def _ssd_kernel(kind_ref, emit_ref, inb_ref, outb_ref,
                xbc_ref, z_ref, sm_ref, cache_ref, s0_ref, cw_ref, parr_ref, parc_ref, nw_ref, dsk_ref,
                o_ref, sout_ref,
                tail, act, tmp, state, cscol, dtcol, csrow):
    del inb_ref, outb_ref
    i = pl.program_id(0)
    l = xbc_ref.shape[0]
    nch = l // CHUNK
    hp = 2 * SSD_HEADDIM

    _conv_silu_tile(i, kind_ref, xbc_ref, cache_ref, cw_ref, tail, act, tmp, CONV_K)

    sm = sm_ref[...]
    dtc = _softplus(sm + parr_ref[1:2, :])
    dtcol[...] = dtc
    cscol[...] = _cumsum_rows(parr_ref[0:1, :] * dtc)
    dt_t = _softplus(sm.T[SM_DT:SM_DT + SSD_HEADS, :] + parc_ref[SSD_HEADS:2 * SSD_HEADS, 0:1])
    cs_t = _cumsum_lanes(parc_ref[0:SSD_HEADS, 0:1] * dt_t)
    left_row = (lax.broadcasted_iota(I32, (1, l), 1) & (hp - 1)) < SSD_HEADDIM
    for p in range(SSD_PAIRS):
        ra = cs_t[2 * p:2 * p + 1, :]
        rb = cs_t[2 * p + 1:2 * p + 2, :]
        even = jnp.where(left_row, ra, pltpu.roll(rb, SSD_HEADDIM, 1))
        odd = jnp.where(left_row, pltpu.roll(ra, l - SSD_HEADDIM, 1), rb)
        for cl in range(nch):
            src = even if cl % 2 == 0 else odd
            v0 = (cl // 2) * hp
            csrow[cl, p:p + 1, :] = src[:, v0:v0 + hp]

    sout_ref[1:, :, :, :] = jnp.zeros((nch - 1,) + tuple(sout_ref.shape[1:]), F32)

    ri = lax.broadcasted_iota(I32, (CHUNK, hp), 0)
    li = lax.broadcasted_iota(I32, (CHUNK, hp), 1)
    left = li < SSD_HEADDIM
    causal = ri >= (li & (SSD_HEADDIM - 1))
    top = lax.broadcasted_iota(I32, (hp, 1), 0) < SSD_HEADDIM
    zpad_b = jnp.zeros((CHUNK, SSD_STATE), BF16)
    zpad_f = jnp.zeros((CHUNK, hp), F32)

    def chunk(c, carry):
        base = pl.multiple_of(c * CHUNK, CHUNK)
        gi = i * nch + c

        @pl.when(kind_ref[gi] == 1)
        def _():
            state[...] = s0_ref[c]

        dt_blk = dtcol[pl.ds(base, CHUNK), :]
        cs_blk = cscol[pl.ds(base, CHUNK), :]
        csr = csrow[c]
        ys = []
        for g in range(SSD_NGROUPS):
            b0 = SSD_WIDTH + g * SSD_STATE
            c0 = SSD_WIDTH + SSD_NGROUPS * SSD_STATE + g * SSD_STATE
            bf = act[pl.ds(base, CHUNK), b0:b0 + SSD_STATE]
            cf = act[pl.ds(base, CHUNK), c0:c0 + SSD_STATE]
            bg = bf.astype(BF16)
            cg = cf.astype(BF16)
            cbw = _dot_nt(cg, jnp.concatenate([bg, bg], axis=0))
            for q in range(SSD_PAIRS // SSD_NGROUPS):
                p = g * (SSD_PAIRS // SSD_NGROUPS) + q
                h0 = SM_DT + 2 * p
                xp = act[pl.ds(base, CHUNK), p * hp:(p + 1) * hp]
                dtp = jnp.where(left, dt_blk[:, h0:h0 + 1], dt_blk[:, h0 + 1:h0 + 2])
                csp = jnp.where(left, cs_blk[:, h0:h0 + 1], cs_blk[:, h0 + 1:h0 + 2])
                cl0 = cs_blk[CHUNK - 1:CHUNK, h0:h0 + 1]
                cl1 = cs_blk[CHUNK - 1:CHUNK, h0 + 1:h0 + 2]
                xdt = xp * dtp
                seg = jnp.where(causal, jnp.exp(jnp.minimum(csp - csr[p:p + 1, :], 0.0)), 0.0)
                scores = (cbw * seg).astype(BF16)
                bd = jnp.concatenate([jnp.where(left, xdt, 0.0), jnp.where(left, 0.0, xdt)], axis=0)
                y = _dot(scores, bd.astype(BF16))
                sp = state[p]
                sb = sp.astype(BF16)
                xdt_t = jnp.concatenate([xdt, zpad_f], axis=0).T.astype(BF16)
                yo, st = [], []
                for hh in range(2):
                    ecs = jnp.exp(cs_blk[:, h0 + hh:h0 + hh + 1])
                    cl = cl0 if hh == 0 else cl1
                    dec = jnp.exp(cl - cs_blk[:, h0 + hh:h0 + hh + 1])
                    yo.append(_dot_nt((cf * ecs).astype(BF16), sb))
                    st.append(_dot(xdt_t, jnp.concatenate([(bf * dec).astype(BF16), zpad_b], axis=0)))
                y = y + jnp.where(left, yo[0], yo[1])
                y = y + xp * dsk_ref[:, p * hp:(p + 1) * hp]
                state[p] = sp * jnp.where(top, jnp.exp(cl0), jnp.exp(cl1)) + jnp.where(top, st[0], st[1])
                ys.append(y)
        yf = jnp.concatenate(ys, axis=1)
        yg = yf * _silu(z_ref[pl.ds(base, CHUNK), :].astype(F32))
        out = yg * lax.rsqrt(jnp.mean(yg * yg, axis=-1, keepdims=True) + EPS) * nw_ref[...]
        o_ref[pl.ds(base, CHUNK), :] = out.astype(BF16)

        @pl.when(emit_ref[gi] >= 0)
        def _():
            sout_ref[emit_ref[gi]] = state[...]

        return carry

    lax.fori_loop(0, nch, chunk, 0)


def _ssd(proj, cache_slots, s0_slots, conv_w, conv_b, a_log, dt_bias, d_skip, norm_w, tables, n_out_blk):
    t = proj.shape[0]
    l = MIX_TILE
    nch = TILE_CHUNKS
    hp = 2 * SSD_HEADDIM
    kind, emit, in_blk, out_blk = tables
    cw = jnp.concatenate([conv_w, conv_b[None, :], jnp.zeros((SUBLANES - CONV_K - 1, SSD_CONV), F32)], axis=0)
    a_neg = -jnp.exp(a_log.astype(F32))
    pre = jnp.zeros((SM_DT,), F32)
    post = jnp.zeros((LANES - SM_DT - SSD_HEADS,), F32)
    parr = jnp.zeros((SUBLANES, LANES), F32)
    parr = parr.at[0].set(jnp.concatenate([pre, a_neg, post]))
    parr = parr.at[1].set(jnp.concatenate([pre, dt_bias.astype(F32), post]))
    parc = jnp.broadcast_to(jnp.concatenate([a_neg, dt_bias.astype(F32)])[:, None], (2 * SSD_HEADS, LANES))
    dsk = jnp.repeat(d_skip.astype(F32), SSD_HEADDIM).reshape(1, SSD_WIDTH)
    grid_spec = pltpu.PrefetchScalarGridSpec(
        num_scalar_prefetch=4,
        grid=(t // l,),
        in_specs=[
            pl.BlockSpec((l, SSD_CONV), lambda i, *_: (i, P_XBC // SSD_CONV)),
            pl.BlockSpec((l, SSD_WIDTH), lambda i, *_: (i, P_ZC // SSD_WIDTH)),
            pl.BlockSpec((l, LANES), lambda i, *_: (i, P_SMALL // LANES)),
            pl.BlockSpec((nch, SUBLANES, SSD_CONV), lambda i, k, e, ib, ob: (ib[i], 0, 0)),
            pl.BlockSpec((nch, SSD_PAIRS, hp, SSD_STATE), lambda i, k, e, ib, ob: (ib[i], 0, 0, 0)),
            pl.BlockSpec((SUBLANES, SSD_CONV), lambda i, *_: (0, 0)),
            pl.BlockSpec((SUBLANES, LANES), lambda i, *_: (0, 0)),
            pl.BlockSpec((2 * SSD_HEADS, LANES), lambda i, *_: (0, 0)),
            pl.BlockSpec((1, SSD_WIDTH), lambda i, *_: (0, 0)),
            pl.BlockSpec((1, SSD_WIDTH), lambda i, *_: (0, 0)),
        ],
        out_specs=[
            pl.BlockSpec((l, SSD_WIDTH), lambda i, *_: (i, 0)),
            pl.BlockSpec((nch, SSD_PAIRS, hp, SSD_STATE), lambda i, k, e, ib, ob: (ob[i], 0, 0, 0)),
        ],
        scratch_shapes=[
            pltpu.VMEM((SUBLANES, SSD_CONV), F32),
            pltpu.VMEM((l, SSD_CONV), F32),
            pltpu.VMEM((2 * SUBLANES, SSD_CONV), F32),
            pltpu.VMEM((SSD_PAIRS, hp, SSD_STATE), F32),
            pltpu.VMEM((l, LANES), F32),
            pltpu.VMEM((l, LANES), F32),
            pltpu.VMEM((nch, SSD_PAIRS, hp), F32),
        ],
    )
    return pl.pallas_call(
        _ssd_kernel,
        grid_spec=grid_spec,
        out_shape=[jax.ShapeDtypeStruct((t, SSD_WIDTH), BF16),
                   jax.ShapeDtypeStruct((n_out_blk * nch, SSD_PAIRS, hp, SSD_STATE), F32)],
        compiler_params=_cparams(("arbitrary",)),
        name="ssd_mixer",
    )(jnp.asarray(kind), jnp.asarray(emit), jnp.asarray(in_blk), jnp.asarray(out_blk),
      proj, proj, proj, cache_slots, s0_slots, cw, parr, parc,
      norm_w.reshape(1, SSD_WIDTH).astype(F32), dsk)


S5_SB = 2
S5_LB = 512


def _s5_kernel(kind_ref, u_ref, h0r_ref, h0i_ref, perm_ref, permt_ref, bre_ref, bim_ref, cre_ref, cim_ref,
               ar_ref, ai_ref, dsk_ref, wglu_ref, bglu_ref,
               o_ref, hfr_ref, hfi_ref,
               bur, bui, pre, pim, cr, ci, inr, ini):
    i = pl.program_id(0)
    l = u_ref.shape[0]
    nch = l // CHUNK
    n = S5_N
    usb = S5_WIDTH // S5_SB
    nsb = n // S5_SB

    @pl.when(i == 0)
    def _():
        pre[0:1, :] = ar_ref[...]
        pim[0:1, :] = ai_ref[...]

        def pw(t, carry):
            pr = pre[pl.ds(t - 1, 1), :]
            pi = pim[pl.ds(t - 1, 1), :]
            pre[pl.ds(t, 1), :] = pr * ar_ref[...] - pi * ai_ref[...]
            pim[pl.ds(t, 1), :] = pr * ai_ref[...] + pi * ar_ref[...]
            return carry

        lax.fori_loop(1, CHUNK, pw, 0)
        cr[...] = jnp.zeros((1, n), F32)
        ci[...] = jnp.zeros((1, n), F32)

    up = _dot(perm_ref[...], u_ref[...].astype(BF16)).astype(BF16)
    for sb in range(S5_SB):
        us = up[:, sb * usb:(sb + 1) * usb]
        bur[:, sb * nsb:(sb + 1) * nsb] = _dot(us, bre_ref[sb])
        bui[:, sb * nsb:(sb + 1) * nsb] = _dot(us, bim_ref[sb])

    for c0 in range(0, n, S5_LB):
        a_r = jnp.broadcast_to(ar_ref[:, c0:c0 + S5_LB], (nch, S5_LB))
        a_i = jnp.broadcast_to(ai_ref[:, c0:c0 + S5_LB], (nch, S5_LB))

        def step(t, carry):
            hr, hi = carry
            r0 = pl.multiple_of(t * nch, nch)
            nr = a_r * hr - a_i * hi + bur[pl.ds(r0, nch), c0:c0 + S5_LB]
            ni = a_r * hi + a_i * hr + bui[pl.ds(r0, nch), c0:c0 + S5_LB]
            bur[pl.ds(r0, nch), c0:c0 + S5_LB] = nr
            bui[pl.ds(r0, nch), c0:c0 + S5_LB] = ni
            return nr, ni

        z = jnp.zeros((nch, S5_LB), F32)
        lax.fori_loop(0, CHUNK, step, (z, z))

    a64r = pre[CHUNK - 1:CHUNK, :]
    a64i = pim[CHUNK - 1:CHUNK, :]
    c_r = cr[...]
    c_i = ci[...]
    for s in range(nch):
        start = kind_ref[i * nch + s] == 1
        i_r = jnp.where(start, h0r_ref[s:s + 1, :], c_r)
        i_i = jnp.where(start, h0i_ref[s:s + 1, :], c_i)
        inr[s:s + 1, :] = i_r
        ini[s:s + 1, :] = i_i
        e_r = bur[l - nch + s:l - nch + s + 1, :]
        e_i = bui[l - nch + s:l - nch + s + 1, :]
        c_r = a64r * i_r - a64i * i_i + e_r
        c_i = a64r * i_i + a64i * i_r + e_i
        hfr_ref[s:s + 1, :] = c_r
        hfi_ref[s:s + 1, :] = c_i
    cr[...] = c_r
    ci[...] = c_i

    for c0 in range(0, n, S5_LB):
        n_r = inr[:, c0:c0 + S5_LB]
        n_i = ini[:, c0:c0 + S5_LB]

        def fix(t, carry):
            r0 = pl.multiple_of(t * nch, nch)
            p_r = pre[pl.ds(t, 1), c0:c0 + S5_LB]
            p_i = pim[pl.ds(t, 1), c0:c0 + S5_LB]
            bur[pl.ds(r0, nch), c0:c0 + S5_LB] += p_r * n_r - p_i * n_i
            bui[pl.ds(r0, nch), c0:c0 + S5_LB] += p_r * n_i + p_i * n_r
            return carry

        lax.fori_loop(0, CHUNK, fix, 0)

    ys = []
    for sb in range(S5_SB):
        hr = bur[:, sb * nsb:(sb + 1) * nsb].astype(BF16)
        hi = bui[:, sb * nsb:(sb + 1) * nsb].astype(BF16)
        ys.append(_dot(hr, cre_ref[sb]) - _dot(hi, cim_ref[sb]))
    yp = jnp.concatenate(ys, axis=1)
    y_hi = yp.astype(BF16)
    r1 = yp - y_hi.astype(F32)
    y_mid = r1.astype(BF16)
    y_lo = (r1 - y_mid.astype(F32)).astype(BF16)
    pt = permt_ref[...]
    y = (_dot(pt, y_hi) + _dot(pt, y_mid)) + _dot(pt, y_lo)
    y = y + u_ref[...] * dsk_ref[...]
    y = y * (0.5 * (1.0 + jnp.tanh(math.sqrt(2.0 / math.pi) * (y + 0.044715 * (y * y * y)))))
    out = y * jax.nn.sigmoid(_dot(y.astype(BF16), wglu_ref[...]) + bglu_ref[...])
    o_ref[...] = out.astype(BF16)


def _s5_tables(a_re, a_im, b_re, b_im, c_re, c_im, log_dt):
    a_re, a_im = a_re.astype(F32), a_im.astype(F32)
    dt = jnp.exp(log_dt.astype(F32))[:, None]
    mag = jnp.exp(dt * a_re)
    abar_re, abar_im = mag * jnp.cos(dt * a_im), mag * jnp.sin(dt * a_im)
    den = a_re * a_re + a_im * a_im
    num_re, num_im = abar_re - 1.0, abar_im
    zoh_re = (num_re * a_re + num_im * a_im) / den
    zoh_im = (num_im * a_re - num_re * a_im) / den
    b_re, b_im = b_re.astype(F32), b_im.astype(F32)
    bbar_re = zoh_re[..., None] * b_re - zoh_im[..., None] * b_im
    bbar_im = zoh_re[..., None] * b_im + zoh_im[..., None] * b_re
    gsb = S5_GROUPS // S5_SB
    eye = jnp.eye(gsb, dtype=F32)

    def bblk(b):
        b = b.reshape(S5_SB, gsb, S5_STATE, S5_GROUP)
        return jnp.einsum('sgpc,gh->sgchp', b, eye).reshape(S5_SB, gsb * S5_GROUP, gsb * S5_STATE).astype(BF16)

    def cblk(c):
        c = c.astype(F32).reshape(S5_SB, gsb, S5_GROUP, S5_STATE)
        return jnp.einsum('sgcp,gh->sgphc', c, eye).reshape(S5_SB, gsb * S5_STATE, gsb * S5_GROUP).astype(BF16)

    return (abar_re.reshape(1, S5_N), abar_im.reshape(1, S5_N),
            bblk(bbar_re), bblk(bbar_im), cblk(c_re), cblk(c_im))


def _s5_perm():
    r_new = np.arange(MIX_TILE)
    r_old = (r_new % TILE_CHUNKS) * CHUNK + r_new // TILE_CHUNKS
    p = np.zeros((MIX_TILE, MIX_TILE), np.float32)
    p[r_new, r_old] = 1.0
    return jnp.asarray(p, BF16), jnp.asarray(p.T, BF16)


def _s5(proj, h0r, h0i, tabs, d_skip, w_glu, b_glu, kind):
    t = proj.shape[0]
    l = MIX_TILE
    nch = TILE_CHUNKS
    abr, abi, bre, bim, cre, cim = tabs
    perm, permt = _s5_perm()
    full2 = lambda a: pl.BlockSpec(a.shape, lambda i, *_: (0, 0))
    full3 = lambda a: pl.BlockSpec(a.shape, lambda i, *_: (0, 0, 0))
    dsk = d_skip.astype(F32).reshape(1, S5_WIDTH)
    wg = w_glu.astype(BF16)
    bg = b_glu.astype(F32).reshape(1, S5_WIDTH)
    grid_spec = pltpu.PrefetchScalarGridSpec(
        num_scalar_prefetch=1,
        grid=(t // l,),
        in_specs=[
            pl.BlockSpec((l, S5_WIDTH), lambda i, *_: (i, P_UB // S5_WIDTH)),
            pl.BlockSpec((nch, S5_N), lambda i, *_: (i, 0)),
            pl.BlockSpec((nch, S5_N), lambda i, *_: (i, 0)),
            full2(perm), full2(permt), full3(bre), full3(bim), full3(cre), full3(cim),
            full2(abr), full2(abi), full2(dsk), full2(wg), full2(bg),
        ],
        out_specs=[
            pl.BlockSpec((l, S5_WIDTH), lambda i, *_: (i, 0)),
            pl.BlockSpec((nch, S5_N), lambda i, *_: (i, 0)),
            pl.BlockSpec((nch, S5_N), lambda i, *_: (i, 0)),
        ],
        scratch_shapes=[
            pltpu.VMEM((l, S5_N), F32), pltpu.VMEM((l, S5_N), F32),
            pltpu.VMEM((CHUNK, S5_N), F32), pltpu.VMEM((CHUNK, S5_N), F32),
            pltpu.VMEM((1, S5_N), F32), pltpu.VMEM((1, S5_N), F32),
            pltpu.VMEM((nch, S5_N), F32), pltpu.VMEM((nch, S5_N), F32),
        ],
    )
    nseg = t // CHUNK
    return pl.pallas_call(
        _s5_kernel,
        grid_spec=grid_spec,
        out_shape=[jax.ShapeDtypeStruct((t, S5_WIDTH), BF16),
                   jax.ShapeDtypeStruct((nseg, S5_N), F32),
                   jax.ShapeDtypeStruct((nseg, S5_N), F32)],
        compiler_params=_cparams(("arbitrary",)),
        name="s5_mixer",
    )(jnp.asarray(kind), proj, h0r, h0i, perm, permt, bre, bim, cre, cim, abr, abi, dsk, wg, bg)


MIXOUT_TM = 512
R_GRP = 0
R_EXP = SUBLANES
NEG_BIG = -1e30


def _mixout_kernel(oa_ref, ob_ref, oc_ref, w_ref, x_ref, nw_ref, wr_ref, rb_ref, x1_ref, h2_ref, lg_ref):
    acc = _dot(oa_ref[...], w_ref[0:GDN_KD, :])
    acc = acc + _dot(ob_ref[...], w_ref[GDN_KD:GDN_KD + S5_WIDTH, :])
    acc = acc + _dot(oc_ref[...], w_ref[GDN_KD + S5_WIDTH:, :])
    x1 = x_ref[...] + acc
    x1_ref[...] = x1
    h = x1 * lax.rsqrt(jnp.mean(x1 * x1, axis=-1, keepdims=True) + EPS) * nw_ref[...]
    h2_ref[...] = h
    lg_ref[...] = _dot(h.astype(BF16), wr_ref[...]) + rb_ref[...]


def _mixout(oa, ob, oc, w_out, x, nw, wr, rb):
    t, d = x.shape
    tm = MIXOUT_TM
    row = lambda w: pl.BlockSpec((tm, w), lambda i: (i, 0))
    full = lambda a: pl.BlockSpec(a.shape, lambda i: (0, 0))
    return pl.pallas_call(
        _mixout_kernel,
        grid=(t // tm,),
        in_specs=[row(GDN_KD), row(S5_WIDTH), row(SSD_WIDTH), full(w_out), row(d), full(nw), full(wr), full(rb)],
        out_specs=[row(d), row(d), row(LANES)],
        out_shape=[jax.ShapeDtypeStruct((t, d), F32), jax.ShapeDtypeStruct((t, d), F32),
                   jax.ShapeDtypeStruct((t, LANES), F32)],
        compiler_params=_cparams(("parallel",)),
        name="mix_out",
    )(oa, ob, oc, w_out, x, nw, wr, rb)


def _router_weights(rg_w, rg_b, re_w, re_b):
    d = rg_w.shape[0]
    wr = jnp.concatenate([rg_w, jnp.zeros((d, R_EXP - N_GROUPS), F32), re_w,
                          jnp.zeros((d, LANES - R_EXP - N_EXPERTS), F32)], axis=1).astype(BF16)
    rb = jnp.concatenate([rg_b.astype(F32), jnp.full((R_EXP - N_GROUPS,), NEG_BIG, F32), re_b.astype(F32),
                          jnp.zeros((LANES - R_EXP - N_EXPERTS,), F32)]).reshape(1, LANES)
    return wr, rb


ROUTE_TM = 512


def _router_kernel(lg_ref, tri_ref, idx_ref, gate_ref, cnt_ref, run):
    i = pl.program_id(0)
    tm = lg_ref.shape[0]

    @pl.when(i == 0)
    def _():
        run[...] = jnp.zeros(run.shape, F32)

    lt = lg_ref[...].T
    row8 = lax.broadcasted_iota(I32, (SUBLANES, tm), 0)
    grp = lt[R_GRP:R_GRP + SUBLANES, :]
    gm = jnp.max(grp, axis=0, keepdims=True)
    gp_top = 1.0 / jnp.sum(jnp.exp(grp - gm), axis=0, keepdims=True)
    g_top = jnp.min(jnp.where(grp == gm, row8, SUBLANES), axis=0, keepdims=True)
    ing = jnp.zeros((EPG, tm), F32)
    for g in range(N_GROUPS):
        ing = jnp.where(g_top == g, lt[R_EXP + g * EPG:R_EXP + (g + 1) * EPG, :], ing)
    em = jnp.max(ing, axis=0, keepdims=True)
    ee = jnp.exp(ing - em)
    p = ee / jnp.sum(ee, axis=0, keepdims=True)
    v1 = jnp.max(p, axis=0, keepdims=True)
    i1 = jnp.min(jnp.where(p == v1, row8, EPG), axis=0, keepdims=True)
    p2 = jnp.where(row8 == i1, -1.0, p)
    v2 = jnp.max(p2, axis=0, keepdims=True)
    i2 = jnp.min(jnp.where(p2 == v2, row8, EPG), axis=0, keepdims=True)
    den = v1 + v2
    gate1 = gp_top * v1 / den
    gate2 = gp_top * v2 / den
    e1 = g_top * EPG + i1
    e2 = g_top * EPG + i2

    erow = lax.broadcasted_iota(I32, (N_EXPERTS, tm), 0)
    hit1 = erow == e1
    hit2 = erow == e2
    oh = jnp.where(hit1 | hit2, 1.0, 0.0)
    before = _dot(oh.astype(BF16), tri_ref[...]) + run[:, 0:1]
    rank1 = jnp.sum(jnp.where(hit1, before, 0.0), axis=0, keepdims=True).astype(I32)
    rank2 = jnp.sum(jnp.where(hit2, before, 0.0), axis=0, keepdims=True).astype(I32)
    run[...] = run[...] + jnp.sum(oh, axis=1, keepdims=True)
    cnt_ref[...] = run[...].astype(I32)

    zi = jnp.zeros((SUBLANES - 4, tm), I32)
    idx_ref[...] = jnp.concatenate([e1, e2, rank1, rank2, zi], axis=0)
    r128 = lax.broadcasted_iota(I32, (LANES, tm), 0)
    gt = jnp.where(r128 == 0, gate1, jnp.where(r128 == 1, gate2, 0.0))
    gate_ref[...] = gt.T


def _router(logits):
    t = logits.shape[0]
    tm = ROUTE_TM
    tri = jnp.asarray(np.triu(np.ones((tm, tm), np.float32), 1), BF16)
    return pl.pallas_call(
        _router_kernel,
        grid=(t // tm,),
        in_specs=[pl.BlockSpec((tm, LANES), lambda i: (i, 0)),
                  pl.BlockSpec((tm, tm), lambda i: (0, 0))],
        out_specs=[pl.BlockSpec((SUBLANES, tm), lambda i: (0, i)),
                   pl.BlockSpec((tm, LANES), lambda i: (i, 0)),
                   pl.BlockSpec((N_EXPERTS, LANES), lambda i: (0, 0))],
        out_shape=[jax.ShapeDtypeStruct((SUBLANES, t), I32),
                   jax.ShapeDtypeStruct((t, LANES), F32),
                   jax.ShapeDtypeStruct((N_EXPERTS, LANES), I32)],
        scratch_shapes=[pltpu.VMEM((N_EXPERTS, LANES), F32)],
        compiler_params=_cparams(("arbitrary",)),
        name="router",
    )(logits, tri)


MOE_BLOCK = 256
DISPATCH_TM = 512
COMBINE_TM = 256


def _dispatch_kernel(pstart_ref, idx_ref, h2_ref, xs_in_ref, xs_ref, sem):
    del xs_in_ref
    i = pl.program_id(0)
    tm = idx_ref.shape[1]

    def copy(t, k):
        slot = pstart_ref[idx_ref[k, t]] + idx_ref[2 + k, t]
        return pltpu.make_async_copy(h2_ref.at[pl.ds(i * tm + t, 1)], xs_ref.at[pl.ds(slot, 1)], sem)

    def issue(t, carry):
        copy(t, 0).start()
        copy(t, 1).start()
        return carry

    def drain(t, carry):
        copy(t, 0).wait()
        copy(t, 1).wait()
        return carry

    lax.fori_loop(0, tm, issue, 0)
    lax.fori_loop(0, tm, drain, 0)


def _dispatch(pad_start, idx, h2, n_slots):
    t, d = h2.shape
    tm = DISPATCH_TM
    xs0 = jnp.zeros((n_slots, d), F32)
    grid_spec = pltpu.PrefetchScalarGridSpec(
        num_scalar_prefetch=1,
        grid=(t // tm,),
        in_specs=[pl.BlockSpec((SUBLANES, tm), lambda i, *_: (0, i), memory_space=pltpu.SMEM),
                  pl.BlockSpec(memory_space=pl.ANY),
                  pl.BlockSpec(memory_space=pl.ANY)],
        out_specs=pl.BlockSpec(memory_space=pl.ANY),
        scratch_shapes=[pltpu.SemaphoreType.DMA(())],
    )
    return pl.pallas_call(
        _dispatch_kernel,
        grid_spec=grid_spec,
        out_shape=jax.ShapeDtypeStruct((n_slots, d), F32),
        input_output_aliases={3: 0},
        compiler_params=_cparams(("arbitrary",)),
        name="moe_dispatch",
    )(pad_start, idx, h2, xs0)


def _expert_kernel(be_ref, nv_ref, xs_ref, wg_ref, wu_ref, wd_ref, ys_ref):
    b = pl.program_id(0)

    @pl.when(b < nv_ref[0])
    def _():
        x = xs_ref[...].astype(BF16)
        g = _dot(x, wg_ref[0])
        u = _dot(x, wu_ref[0])
        h = (_silu(g) * u).astype(BF16)
        ys_ref[...] = _dot(h, wd_ref[0])

    @pl.when(b >= nv_ref[0])
    def _():
        ys_ref[...] = jnp.zeros(ys_ref.shape, F32)


def _experts(block_expert, n_valid, xs, wg, wu, wd):
    n_slots, d = xs.shape
    de = wg.shape[2]
    nb = n_slots // MOE_BLOCK
    blk = lambda b, be, nv: (jnp.minimum(b, nv[0] - 1), 0)
    wsel = lambda b, be, nv: (be[jnp.minimum(b, nv[0] - 1)], 0, 0)
    grid_spec = pltpu.PrefetchScalarGridSpec(
        num_scalar_prefetch=2,
        grid=(nb,),
        in_specs=[pl.BlockSpec((MOE_BLOCK, d), blk),
                  pl.BlockSpec((1, d, de), wsel),
                  pl.BlockSpec((1, d, de), wsel),
                  pl.BlockSpec((1, de, d), wsel)],
        out_specs=pl.BlockSpec((MOE_BLOCK, d), lambda b, be, nv: (b, 0)),
    )
    return pl.pallas_call(
        _expert_kernel,
        grid_spec=grid_spec,
        out_shape=jax.ShapeDtypeStruct((n_slots, d), F32),
        compiler_params=_cparams(("arbitrary",)),
        name="moe_experts",
    )(block_expert, n_valid, xs, wg, wu, wd)


def _combine_kernel(pstart_ref, idx_ref, x1_ref, gate_ref, ys_ref, nw_ref, out_ref, ybuf, sem, *, final):
    tm = x1_ref.shape[0]

    def copy(t, k):
        slot = pstart_ref[idx_ref[k, t]] + idx_ref[2 + k, t]
        return pltpu.make_async_copy(ys_ref.at[pl.ds(slot, 1)], ybuf.at[k, pl.ds(t, 1)], sem)

    def issue(t, carry):
        copy(t, 0).start()
        copy(t, 1).start()
        return carry

    def drain(t, carry):
        copy(t, 0).wait()
        copy(t, 1).wait()
        return carry

    lax.fori_loop(0, tm, issue, 0)
    lax.fori_loop(0, tm, drain, 0)
    g = gate_ref[...]
    y = ybuf[0] * g[:, 0:1] + ybuf[1] * g[:, 1:2]
    x2 = x1_ref[...] + y
    if final:
        x2 = x2 * lax.rsqrt(jnp.mean(x2 * x2, axis=-1, keepdims=True) + EPS) * nw_ref[...]
    out_ref[...] = x2


def _combine(pad_start, idx, x1, gates, ys, nw, final):
    t, d = x1.shape
    tm = COMBINE_TM
    grid_spec = pltpu.PrefetchScalarGridSpec(
        num_scalar_prefetch=1,
        grid=(t // tm,),
        in_specs=[pl.BlockSpec((SUBLANES, tm), lambda i, *_: (0, i), memory_space=pltpu.SMEM),
                  pl.BlockSpec((tm, d), lambda i, *_: (i, 0)),
                  pl.BlockSpec((tm, LANES), lambda i, *_: (i, 0)),
                  pl.BlockSpec(memory_space=pl.ANY),
                  pl.BlockSpec((1, d), lambda i, *_: (0, 0))],
        out_specs=pl.BlockSpec((tm, d), lambda i, *_: (i, 0)),
        scratch_shapes=[pltpu.VMEM((2, tm, d), F32), pltpu.SemaphoreType.DMA(())],
    )
    return pl.pallas_call(
        functools.partial(_combine_kernel, final=final),
        grid_spec=grid_spec,
        out_shape=jax.ShapeDtypeStruct((t, d), F32),
        compiler_params=_cparams(("arbitrary",)),
        name="moe_combine",
    )(pad_start, idx, x1, gates, ys, nw)


def _moe(x1, h2, logits, wg, wu, wd, norm_final, final):
    t, d = x1.shape
    idx, gates, cnt = _router(logits)
    counts = cnt[:, 0]
    padded = (counts + MOE_BLOCK - 1) // MOE_BLOCK * MOE_BLOCK
    pad_end = jnp.cumsum(padded)
    pad_start = (pad_end - padded).astype(I32)
    nb = (2 * t + N_EXPERTS * (MOE_BLOCK - 1) + MOE_BLOCK - 1) // MOE_BLOCK
    n_valid = (pad_end[-1] // MOE_BLOCK).astype(I32).reshape(1)
    block_expert = jnp.minimum(
        jnp.searchsorted(pad_end, jnp.arange(nb, dtype=I32) * MOE_BLOCK, side='right'),
        N_EXPERTS - 1).astype(I32)
    xs = _dispatch(pad_start, idx, h2, nb * MOE_BLOCK)
    ys = _experts(block_expert, n_valid, xs, wg, wu, wd)
    return _combine(pad_start, idx, x1, gates, ys, norm_final, final)


def _stream_ends(nbp, seq, nbs, dseq):
    ends = [(b + 1) * seq for b in range(nbp)] + [nbp * seq + (s + 1) * dseq for s in range(nbs)]
    return np.asarray(ends)


def kernel(x_prompt, x_sample, cache_conv_gdn, state_gdn, state_s5, cache_conv_ssd, state_ssd, norm_mix, w_in, gdn_conv_w, gdn_a_log, gdn_dt_bias, gdn_norm, s5_a_re, s5_a_im, s5_b_re, s5_b_im, s5_c_re, s5_c_im, s5_log_dt, s5_d, s5_w_glu, s5_b_glu, ssd_conv_w, ssd_conv_b, ssd_a_log, ssd_dt_bias, ssd_d, ssd_norm, w_out, norm_ffn, router_group_w, router_group_b, router_expert_w, router_expert_b, expert_w_gate, expert_w_up, expert_w_down, norm_final):
    nbp, seq, d = x_prompt.shape
    nbs, dseq, _ = x_sample.shape
    depth = w_in.shape[0]
    tp = nbp * seq
    t = tp + nbs * dseq
    x = jnp.concatenate([x_prompt.reshape(tp, d), x_sample.reshape(nbs * dseq, d)], axis=0)

    tables = _chunk_tables(nbp, seq, nbs, dseq)
    kind = tables[0]
    n_out_blk = nbp + nbs // TILE_CHUNKS
    ends = _stream_ends(nbp, seq, nbs, dseq)
    tail_rows = (ends[:, None] + np.arange(-(CONV_K - 1), 0)[None, :]).reshape(-1)
    end_seg = ends // CHUNK - 1
    ncp = tp // CHUNK
    state_rows = np.concatenate([np.arange(nbp) * TILE_CHUNKS, nbp * TILE_CHUNKS + np.arange(nbs)])

    new_conv_gdn, new_gdn, new_s5, new_conv_ssd, new_ssd = [], [], [], [], []
    for l in range(depth):
        proj = _proj(x, norm_mix[l].reshape(1, d).astype(F32), _rearrange_w_in(w_in[l]))

        oa, sg = _gdn(proj, _conv_cache_slots(cache_conv_gdn[l].astype(F32)),
                      _init_slots(state_gdn[l].astype(F32)), gdn_conv_w[l].astype(F32), gdn_a_log[l],
                      gdn_dt_bias[l], gdn_norm[l], tables, n_out_blk)

        h0 = state_s5[l].astype(F32).reshape(nbs, S5_N, 2)
        zeros_p = jnp.zeros((ncp, S5_N), F32)
        ob, hfr, hfi = _s5(proj, jnp.concatenate([zeros_p, h0[..., 0]], axis=0),
                           jnp.concatenate([zeros_p, h0[..., 1]], axis=0),
                           _s5_tables(s5_a_re[l], s5_a_im[l], s5_b_re[l], s5_b_im[l], s5_c_re[l], s5_c_im[l],
                                      s5_log_dt[l]),
                           s5_d[l], s5_w_glu[l], s5_b_glu[l], kind)

        oc, ss = _ssd(proj, _conv_cache_slots(cache_conv_ssd[l].astype(F32)),
                      _init_slots(state_ssd[l].astype(F32).reshape(nbs, SSD_PAIRS, 2 * SSD_HEADDIM, SSD_STATE)),
                      ssd_conv_w[l].astype(F32), ssd_conv_b[l].astype(F32), ssd_a_log[l], ssd_dt_bias[l],
                      ssd_d[l], ssd_norm[l], tables, n_out_blk)

        wr, rb = _router_weights(router_group_w[l].astype(F32), router_group_b[l],
                                 router_expert_w[l].astype(F32), router_expert_b[l])
        x1, h2, logits = _mixout(oa, ob, oc, w_out[l].astype(BF16), x,
                                 norm_ffn[l].reshape(1, d).astype(F32), wr, rb)
        x = _moe(x1, h2, logits, expert_w_gate[l].astype(BF16), expert_w_up[l].astype(BF16),
                 expert_w_down[l].astype(BF16), norm_final.reshape(1, d).astype(F32), l == depth - 1)

        tails = proj[tail_rows]
        new_conv_gdn.append(tails[:, P_QKV:P_QKV + GDN_CONV].reshape(nbp + nbs, CONV_K - 1, GDN_CONV))
        new_conv_ssd.append(tails[:, P_XBC:P_XBC + SSD_CONV].reshape(nbp + nbs, CONV_K - 1, SSD_CONV))
        new_gdn.append(sg[state_rows])
        new_ssd.append(ss[state_rows].reshape(nbp + nbs, SSD_HEADS, SSD_HEADDIM, SSD_STATE))
        new_s5.append(jnp.stack([hfr[end_seg], hfi[end_seg]], axis=-1)
                      .reshape(nbp + nbs, S5_GROUPS, S5_STATE, 2))

    def split(parts):
        a = jnp.stack(parts)
        return a[:, :nbp], a[:, nbp:]

    cg_p, cg_s = split(new_conv_gdn)
    sg_p, sg_s = split(new_gdn)
    s5_p, s5_s = split(new_s5)
    cs_p, cs_s = split(new_conv_ssd)
    ss_p, ss_s = split(new_ssd)
    y_prompt = x[:tp].reshape(nbp, seq, d)
    y_sample = x[tp:].reshape(nbs, dseq, d)
    return (y_prompt, y_sample, cg_p, sg_p, s5_p, cs_p, ss_p, cg_s, sg_s, s5_s, cs_s, ss_s)
```

```python
import functools
import math

import numpy as np
import jax
import jax.numpy as jnp
from jax import lax
from jax.experimental import pallas as pl
from jax.experimental.pallas import tpu as pltpu

F32 = jnp.float32
BF16 = jnp.bfloat16
I32 = jnp.int32

EPS = 1e-6
CHUNK = 64
CONV_K = 4
LANES = 128
SUBLANES = 8
VMEM_LIMIT = 56 * 1024 * 1024

GDN_HEADS = 8
GDN_DK = 128
GDN_DV = 128
GDN_KD = GDN_HEADS * GDN_DK
GDN_CONV = 3 * GDN_KD
S5_WIDTH = 512
S5_GROUPS = 32
S5_GROUP = 16
S5_STATE = 64
S5_N = S5_GROUPS * S5_STATE
SSD_WIDTH = 512
SSD_HEADS = 8
SSD_HEADDIM = 64
SSD_NGROUPS = 2
SSD_STATE = 128
SSD_CONV = SSD_WIDTH + 2 * SSD_NGROUPS * SSD_STATE
N_GROUPS = 4
EPG = 8
N_EXPERTS = 32

MIX_TILE = 512
TILE_CHUNKS = MIX_TILE // CHUNK


def _cparams(sem, vmem=VMEM_LIMIT):
    return pltpu.CompilerParams(dimension_semantics=sem, vmem_limit_bytes=vmem)


def _silu(x):
    return x * jax.nn.sigmoid(x)


def _softplus(x):
    return jnp.maximum(x, 0.0) + jnp.log1p(jnp.exp(-jnp.abs(x)))


def _dot(a, b):
    return jnp.dot(a, b, preferred_element_type=F32)


def _dot3(a, b):
    ah = a.astype(BF16)
    bh = b.astype(BF16)
    al = (a - ah.astype(F32)).astype(BF16)
    bl = (b - bh.astype(F32)).astype(BF16)
    return _dot(ah, bh) + (_dot(ah, bl) + _dot(al, bh))


def _dot_nt(a, b):
    return lax.dot_general(a, b, (((1,), (1,)), ((), ())), preferred_element_type=F32)


def _dot_tn(a, b):
    return lax.dot_general(a, b, (((0,), (0,)), ((), ())), preferred_element_type=F32)


def _cumsum_rows(x):
    row = lax.broadcasted_iota(I32, x.shape, 0) & (CHUNK - 1)
    k = 1
    while k < CHUNK:
        x = x + jnp.where(row >= k, pltpu.roll(x, k, 0), 0.0)
        k *= 2
    return x


def _cumsum_lanes(x):
    lane = lax.broadcasted_iota(I32, x.shape, 1) & (CHUNK - 1)
    k = 1
    while k < CHUNK:
        x = x + jnp.where(lane >= k, pltpu.roll(x, k, 1), 0.0)
        k *= 2
    return x


PROJ_TM = 1024
PROJ_TN = 1280
P_QKV, P_ZA, P_UB, P_ZC, P_XBC = 0, 3072, 4096, 4608, 5120
P_MAIN = 6144
P_TOTAL = 6400
P_SMALL = P_TOTAL - LANES


def _proj_kernel(x_ref, nw_ref, w_ref, o_ref, h_scr):
    @pl.when(pl.program_id(1) == 0)
    def _():
        x = x_ref[...]
        ms = jnp.mean(x * x, axis=-1, keepdims=True)
        h_scr[...] = (x * lax.rsqrt(ms + EPS) * nw_ref[...]).astype(BF16)

    o_ref[...] = _dot(h_scr[...], w_ref[...])


def _proj(x, nw, w):
    t, d = x.shape
    n = w.shape[1]
    tm = min(PROJ_TM, t)
    assert t % tm == 0 and n % PROJ_TN == 0
    return pl.pallas_call(
        _proj_kernel,
        grid=(t // tm, n // PROJ_TN),
        in_specs=[pl.BlockSpec((tm, d), lambda i, j: (i, 0)),
                  pl.BlockSpec((1, d), lambda i, j: (0, 0)),
                  pl.BlockSpec((d, PROJ_TN), lambda i, j: (0, j))],
        out_specs=pl.BlockSpec((tm, PROJ_TN), lambda i, j: (i, j)),
        out_shape=jax.ShapeDtypeStruct((t, n), F32),
        scratch_shapes=[pltpu.VMEM((tm, d), BF16)],
        compiler_params=_cparams(("parallel", "arbitrary")),
        name="proj_in",
    )(x, nw, w)


def _rearrange_w_in(w_in):
    d = w_in.shape[0]
    off_za = GDN_CONV
    off_ba = off_za + GDN_KD
    off_s5 = off_ba + 2 * GDN_HEADS
    off_zc = off_s5 + S5_WIDTH
    off_xbc = off_zc + SSD_WIDTH
    off_dt = off_xbc + SSD_CONV
    small = jnp.concatenate([w_in[:, off_ba:off_s5], w_in[:, off_dt:],
                             jnp.zeros((d, LANES - 3 * GDN_HEADS), w_in.dtype)], axis=1)
    w = jnp.concatenate([w_in[:, :off_ba], w_in[:, off_s5:off_dt],
                         jnp.zeros((d, P_TOTAL - P_MAIN - LANES), w_in.dtype), small], axis=1)
    return w.astype(BF16)


def _chunk_tables(nbp, seq, nbs, dseq):
    assert seq % MIX_TILE == 0 and dseq == CHUNK and (nbs * dseq) % MIX_TILE == 0
    cps = seq // CHUNK
    ncp = nbp * cps
    nc = ncp + nbs
    kind = np.zeros((nc,), np.int32)
    emit = np.full((nc,), -1, np.int32)
    for c in range(nc):
        if c < ncp:
            kind[c] = 1 if c % cps == 0 else 0
            if c % TILE_CHUNKS == TILE_CHUNKS - 1:
                emit[c] = 0
        else:
            kind[c] = 1
            emit[c] = (c - ncp) % TILE_CHUNKS
    ntp = ncp // TILE_CHUNKS
    nts = nbs // TILE_CHUNKS
    in_blk = np.concatenate([np.zeros((ntp,), np.int32), 1 + np.arange(nts, dtype=np.int32)])
    out_blk = np.concatenate([np.arange(ntp, dtype=np.int32) // (cps // TILE_CHUNKS),
                              nbp + np.arange(nts, dtype=np.int32)])
    return kind, emit, in_blk, out_blk


def _init_slots(x):
    return jnp.concatenate([jnp.zeros((TILE_CHUNKS,) + x.shape[1:], x.dtype), x], axis=0)


def _conv_cache_slots(cache):
    nbs, k1, c = cache.shape
    padded = jnp.concatenate([jnp.zeros((nbs, SUBLANES - k1, c), cache.dtype), cache], axis=1)
    return _init_slots(padded)


def _conv_silu_tile(i, kind_ref, in_ref, cache_ref, cw_ref, tail, act, tmp, bias_row):
    l, c = in_ref.shape
    cb = 512
    k1 = CONV_K - 1
    nch = l // CHUNK

    @pl.when(i == 0)
    def _():
        tail[...] = jnp.zeros(tail.shape, F32)

    def taps(src, lo, hi, c0):
        acc = src[lo - k1:hi - k1, c0:c0 + cb] * cw_ref[0:1, c0:c0 + cb]
        for j in range(1, CONV_K):
            acc = acc + src[lo - k1 + j:hi - k1 + j, c0:c0 + cb] * cw_ref[j:j + 1, c0:c0 + cb]
        if bias_row is not None:
            acc = acc + cw_ref[bias_row:bias_row + 1, c0:c0 + cb]
        return _silu(acc)

    def head_rows(prev, r0):
        tmp[0:SUBLANES, :] = prev
        tmp[SUBLANES:2 * SUBLANES, :] = in_ref[r0:r0 + SUBLANES, :]
        for c0 in range(0, c, cb):
            act[r0:r0 + SUBLANES, c0:c0 + cb] = taps(tmp, SUBLANES, 2 * SUBLANES, c0)

    for rb in range(nch):
        lo = SUBLANES if rb == 0 else rb * CHUNK
        for c0 in range(0, c, cb):
            act[lo:(rb + 1) * CHUNK, c0:c0 + cb] = taps(in_ref, lo, (rb + 1) * CHUNK, c0)
    head_rows(tail[...], 0)

    for cl in range(nch):
        @pl.when(kind_ref[i * nch + cl] == 1)
        def _():
            head_rows(cache_ref[cl], cl * CHUNK)

    tail[...] = in_ref[l - SUBLANES:l, :]


def _gdn_kernel(kind_ref, emit_ref, inb_ref, outb_ref,
                qkv_ref, z_ref, sm_ref, cache_ref, s0_ref, cw_ref, parr_ref, parc_ref, nw_ref,
                o_ref, sout_ref,
                tail, act, tmp, state, gcol, beta_s, grow):
    del inb_ref, outb_ref
    i = pl.program_id(0)
    l = qkv_ref.shape[0]
    nch = l // CHUNK
    h_n, dk = GDN_HEADS, GDN_DK

    _conv_silu_tile(i, kind_ref, qkv_ref, cache_ref, cw_ref, tail, act, tmp, None)

    sm = sm_ref[...]
    beta_s[...] = jax.nn.sigmoid(sm)
    g = parr_ref[0:1, :] * _softplus(sm + parr_ref[1:2, :])
    gcol[...] = _cumsum_rows(g)
    a_t = sm.T[h_n:2 * h_n, :]
    g_t = parc_ref[0:h_n, 0:1] * _softplus(a_t + parc_ref[h_n:2 * h_n, 0:1])
    g_t = _cumsum_lanes(g_t)
    for cl in range(nch):
        grow[cl] = g_t[:, cl * CHUNK:(cl + 1) * CHUNK]

    sout_ref[1:, :, :, :] = jnp.zeros((nch - 1,) + tuple(sout_ref.shape[1:]), F32)

    ri = lax.broadcasted_iota(I32, (CHUNK, CHUNK), 0)
    ci = lax.broadcasted_iota(I32, (CHUNK, CHUNK), 1)
    causal = ri >= ci
    strict = ri > ci

    def chunk(c, carry):
        base = pl.multiple_of(c * CHUNK, CHUNK)
        gi = i * nch + c

        @pl.when(kind_ref[gi] == 1)
        def _():
            state[...] = s0_ref[c]

        gc_blk = gcol[pl.ds(base, CHUNK), :]
        bt_blk = beta_s[pl.ds(base, CHUNK), :]
        gr_blk = grow[c]
        for h in range(h_n):
            q = act[pl.ds(base, CHUNK), h * dk:(h + 1) * dk]
            k = act[pl.ds(base, CHUNK), GDN_KD + h * dk:GDN_KD + (h + 1) * dk]
            v = act[pl.ds(base, CHUNK), 2 * GDN_KD + h * dk:2 * GDN_KD + (h + 1) * dk]
            q = q * lax.rsqrt(jnp.sum(q * q, axis=-1, keepdims=True) + 1e-6) * (dk ** -0.5)
            k = k * lax.rsqrt(jnp.sum(k * k, axis=-1, keepdims=True) + 1e-6)
            gcl = gc_blk[:, h_n + h:h_n + h + 1]
            btl = bt_blk[:, h:h + 1]
            grw = gr_blk[h:h + 1, :]
            glast = gc_blk[CHUNK - 1:CHUNK, h_n + h:h_n + h + 1]
            eg = jnp.exp(gcl)
            decay = jnp.where(causal, jnp.exp(jnp.minimum(gcl - grw, 0.0)), 0.0)
            kb = k.astype(BF16)
            kk = _dot_nt(kb, kb)
            xm = jnp.where(strict, -(btl * kk * decay), 0.0)
            y = jnp.concatenate([v * btl, k * (btl * eg)], axis=1)
            p = xm
            for step in range(6):
                if step < 2:
                    y = y + _dot3(p, y)
                else:
                    y = y + _dot(p.astype(BF16), y.astype(BF16))
                if step == 0:
                    p = _dot3(p, p)
                elif step < 5:
                    pb = p.astype(BF16)
                    p = _dot(pb, pb)
            u = y[:, :GDN_DV]
            w = y[:, GDN_DV:]
            qk = _dot_nt(q.astype(BF16), kb) * decay
            qd = q * eg
            kd = k * jnp.exp(glast - gcl)
            s = state[h]
            sb = s.astype(BF16)
            v_new = u - _dot(w.astype(BF16), sb)
            vb = v_new.astype(BF16)
            o = _dot(qd.astype(BF16), sb) + _dot(qk.astype(BF16), vb)
            state[h] = s * jnp.exp(glast) + _dot_tn(kd.astype(BF16), vb)
            zz = z_ref[pl.ds(base, CHUNK), h * GDN_DV:(h + 1) * GDN_DV].astype(F32)
            on = o * lax.rsqrt(jnp.mean(o * o, axis=-1, keepdims=True) + EPS) * nw_ref[...] * _silu(zz)
            o_ref[pl.ds(base, CHUNK), h * GDN_DV:(h + 1) * GDN_DV] = on.astype(BF16)

        @pl.when(emit_ref[gi] >= 0)
        def _():
            sout_ref[emit_ref[gi]] = state[...]

        return carry

    lax.fori_loop(0, nch, chunk, 0)


def _gdn(proj, cache_slots, s0_slots, conv_w, a_log, dt_bias, norm_w, tables, n_out_blk):
    t = proj.shape[0]
    l = MIX_TILE
    nch = TILE_CHUNKS
    kind, emit, in_blk, out_blk = tables
    cw = jnp.concatenate([conv_w, jnp.zeros((SUBLANES - CONV_K, GDN_CONV), F32)], axis=0)
    a_neg = -jnp.exp(a_log.astype(F32))
    lane_pad = jnp.zeros((LANES - 2 * GDN_HEADS,), F32)
    parr = jnp.zeros((SUBLANES, LANES), F32)
    parr = parr.at[0].set(jnp.concatenate([jnp.zeros((GDN_HEADS,), F32), a_neg, lane_pad]))
    parr = parr.at[1].set(jnp.concatenate([jnp.zeros((GDN_HEADS,), F32), dt_bias.astype(F32), lane_pad]))
    parc = jnp.broadcast_to(jnp.concatenate([a_neg, dt_bias.astype(F32)])[:, None], (2 * GDN_HEADS, LANES))
    grid_spec = pltpu.PrefetchScalarGridSpec(
        num_scalar_prefetch=4,
        grid=(t // l,),
        in_specs=[
            pl.BlockSpec((l, GDN_CONV), lambda i, *_: (i, 0)),
            pl.BlockSpec((l, GDN_KD), lambda i, *_: (i, P_ZA // GDN_KD)),
            pl.BlockSpec((l, LANES), lambda i, *_: (i, P_SMALL // LANES)),
            pl.BlockSpec((nch, SUBLANES, GDN_CONV), lambda i, k, e, ib, ob: (ib[i], 0, 0)),
            pl.BlockSpec((nch, GDN_HEADS, GDN_DK, GDN_DV), lambda i, k, e, ib, ob: (ib[i], 0, 0, 0)),
            pl.BlockSpec((SUBLANES, GDN_CONV), lambda i, *_: (0, 0)),
            pl.BlockSpec((SUBLANES, LANES), lambda i, *_: (0, 0)),
            pl.BlockSpec((2 * GDN_HEADS, LANES), lambda i, *_: (0, 0)),
            pl.BlockSpec((1, GDN_DV), lambda i, *_: (0, 0)),
        ],
        out_specs=[
            pl.BlockSpec((l, GDN_KD), lambda i, *_: (i, 0)),
            pl.BlockSpec((nch, GDN_HEADS, GDN_DK, GDN_DV), lambda i, k, e, ib, ob: (ob[i], 0, 0, 0)),
        ],
        scratch_shapes=[
            pltpu.VMEM((SUBLANES, GDN_CONV), F32),
            pltpu.VMEM((l, GDN_CONV), F32),
            pltpu.VMEM((2 * SUBLANES, GDN_CONV), F32),
            pltpu.VMEM((GDN_HEADS, GDN_DK, GDN_DV), F32),
            pltpu.VMEM((l, LANES), F32),
            pltpu.VMEM((l, LANES), F32),
            pltpu.VMEM((nch, GDN_HEADS, CHUNK), F32),
        ],
    )
    return pl.pallas_call(
        _gdn_kernel,
        grid_spec=grid_spec,
        out_shape=[jax.ShapeDtypeStruct((t, GDN_KD), BF16),
                   jax.ShapeDtypeStruct((n_out_blk * nch, GDN_HEADS, GDN_DK, GDN_DV), F32)],
        compiler_params=_cparams(("arbitrary",)),
        name="gdn_mixer",
    )(jnp.asarray(kind), jnp.asarray(emit), jnp.asarray(in_blk), jnp.asarray(out_blk),
      proj, proj, proj, cache_slots, s0_slots, cw, parr, parc, norm_w.reshape(1, GDN_DV).astype(F32))


SSD_PAIRS = SSD_HEADS // 2
SM_DT = 2 * GDN_HEADS


def _ssd_kernel(kind_ref, emit_ref, inb_ref, outb_ref,
                xbc_ref, z_ref, sm_ref, cache_ref, s0_ref, cw_ref, parr_ref, parc_ref, nw_ref, dsk_ref,
                o_ref, sout_ref,
                tail, act, tmp, state, cscol, dtcol, csrow):
    del inb_ref, outb_ref
    i = pl.program_id(0)
    l = xbc_ref.shape[0]
    nch = l // CHUNK
    hp = 2 * SSD_HEADDIM

    _conv_silu_tile(i, kind_ref, xbc_ref, cache_ref, cw_ref, tail, act, tmp, CONV_K)

    sm = sm_ref[...]
    dtc = _softplus(sm + parr_ref[1:2, :])
    dtcol[...] = dtc
    cscol[...] = _cumsum_rows(parr_ref[0:1, :] * dtc)
    dt_t = _softplus(sm.T[SM_DT:SM_DT + SSD_HEADS, :] + parc_ref[SSD_HEADS:2 * SSD_HEADS, 0:1])
    cs_t = _cumsum_lanes(parc_ref[0:SSD_HEADS, 0:1] * dt_t)
    left_row = (lax.broadcasted_iota(I32, (1, l), 1) & (hp - 1)) < SSD_HEADDIM
    for p in range(SSD_PAIRS):
        ra = cs_t[2 * p:2 * p + 1, :]
        rb = cs_t[2 * p + 1:2 * p + 2, :]
        even = jnp.where(left_row, ra, pltpu.roll(rb, SSD_HEADDIM, 1))
        odd = jnp.where(left_row, pltpu.roll(ra, l - SSD_HEADDIM, 1), rb)
        for cl in range(nch):
            src = even if cl % 2 == 0 else odd
            v0 = (cl // 2) * hp
            csrow[cl, p:p + 1, :] = src[:, v0:v0 + hp]

    sout_ref[1:, :, :, :] = jnp.zeros((nch - 1,) + tuple(sout_ref.shape[1:]), F32)

    ri = lax.broadcasted_iota(I32, (CHUNK, hp), 0)
    li = lax.broadcasted_iota(I32, (CHUNK, hp), 1)
    left = li < SSD_HEADDIM
    causal = ri >= (li & (SSD_HEADDIM - 1))
    top = lax.broadcasted_iota(I32, (hp, 1), 0) < SSD_HEADDIM
    zpad_b = jnp.zeros((CHUNK, SSD_STATE), BF16)
    zpad_f = jnp.zeros((CHUNK, hp), F32)

    def chunk(c, carry):
        base = pl.multiple_of(c * CHUNK, CHUNK)
        gi = i * nch + c

        @pl.when(kind_ref[gi] == 1)
        def _():
            state[...] = s0_ref[c]

        dt_blk = dtcol[pl.ds(base, CHUNK), :]
        cs_blk = cscol[pl.ds(base, CHUNK), :]
        csr = csrow[c]
        ys = []
        for g in range(SSD_NGROUPS):
            b0 = SSD_WIDTH + g * SSD_STATE
            c0 = SSD_WIDTH + SSD_NGROUPS * SSD_STATE + g * SSD_STATE
            bf = act[pl.ds(base, CHUNK), b0:b0 + SSD_STATE]
            cf = act[pl.ds(base, CHUNK), c0:c0 + SSD_STATE]
            bg = bf.astype(BF16)
            cg = cf.astype(BF16)
            cbw = _dot_nt(cg, jnp.concatenate([bg, bg], axis=0))
            for q in range(SSD_PAIRS // SSD_NGROUPS):
                p = g * (SSD_PAIRS // SSD_NGROUPS) + q
                h0 = SM_DT + 2 * p
                xp = act[pl.ds(base, CHUNK), p * hp:(p + 1) * hp]
                dtp = jnp.where(left, dt_blk[:, h0:h0 + 1], dt_blk[:, h0 + 1:h0 + 2])
                csp = jnp.where(left, cs_blk[:, h0:h0 + 1], cs_blk[:, h0 + 1:h0 + 2])
                cl0 = cs_blk[CHUNK - 1:CHUNK, h0:h0 + 1]
                cl1 = cs_blk[CHUNK - 1:CHUNK, h0 + 1:h0 + 2]
                xdt = xp * dtp
                seg = jnp.where(causal, jnp.exp(jnp.minimum(csp - csr[p:p + 1, :], 0.0)), 0.0)
                scores = (cbw * seg).astype(BF16)
                bd = jnp.concatenate([jnp.where(left, xdt, 0.0), jnp.where(left, 0.0, xdt)], axis=0)
                y = _dot(scores, bd.astype(BF16))
                sp = state[p]
                sb = sp.astype(BF16)
                xdt_t = jnp.concatenate([xdt, zpad_f], axis=0).T.astype(BF16)
                yo, st = [], []
                for hh in range(2):
                    ecs = jnp.exp(cs_blk[:, h0 + hh:h0 + hh + 1])
                    cl = cl0 if hh == 0 else cl1
                    dec = jnp.exp(cl - cs_blk[:, h0 + hh:h0 + hh + 1])
                    yo.append(_dot_nt((cf * ecs).astype(BF16), sb))
                    st.append(_dot(xdt_t, jnp.concatenate([(bf * dec).astype(BF16), zpad_b], axis=0)))
                y = y + jnp.where(left, yo[0], yo[1])
                y = y + xp * dsk_ref[:, p * hp:(p + 1) * hp]
                state[p] = sp * jnp.where(top, jnp.exp(cl0), jnp.exp(cl1)) + jnp.where(top, st[0], st[1])
                ys.append(y)
        yf = jnp.concatenate(ys, axis=1)
        yg = yf * _silu(z_ref[pl.ds(base, CHUNK), :].astype(F32))
        out = yg * lax.rsqrt(jnp.mean(yg * yg, axis=-1, keepdims=True) + EPS) * nw_ref[...]
        o_ref[pl.ds(base, CHUNK), :] = out.astype(BF16)

        @pl.when(emit_ref[gi] >= 0)
        def _():
            sout_ref[emit_ref[gi]] = state[...]

        return carry

    lax.fori_loop(0, nch, chunk, 0)


def _ssd(proj, cache_slots, s0_slots, conv_w, conv_b, a_log, dt_bias, d_skip, norm_w, tables, n_out_blk):
    t = proj.shape[0]
    l = MIX_TILE
    nch = TILE_CHUNKS
    hp = 2 * SSD_HEADDIM
    kind, emit, in_blk, out_blk = tables
    cw = jnp.concatenate([conv_w, conv_b[None, :], jnp.zeros((SUBLANES - CONV_K - 1, SSD_CONV), F32)], axis=0)
    a_neg = -jnp.exp(a_log.astype(F32))
    pre = jnp.zeros((SM_DT,), F32)
    post = jnp.zeros((LANES - SM_DT - SSD_HEADS,), F32)
    parr = jnp.zeros((SUBLANES, LANES), F32)
    parr = parr.at[0].set(jnp.concatenate([pre, a_neg, post]))
    parr = parr.at[1].set(jnp.concatenate([pre, dt_bias.astype(F32), post]))
    parc = jnp.broadcast_to(jnp.concatenate([a_neg, dt_bias.astype(F32)])[:, None], (2 * SSD_HEADS, LANES))
    dsk = jnp.repeat(d_skip.astype(F32), SSD_HEADDIM).reshape(1, SSD_WIDTH)
    grid_spec = pltpu.PrefetchScalarGridSpec(
        num_scalar_prefetch=4,
        grid=(t // l,),
        in_specs=[
            pl.BlockSpec((l, SSD_CONV), lambda i, *_: (i, P_XBC // SSD_CONV)),
            pl.BlockSpec((l, SSD_WIDTH), lambda i, *_: (i, P_ZC // SSD_WIDTH)),
            pl.BlockSpec((l, LANES), lambda i, *_: (i, P_SMALL // LANES)),
            pl.BlockSpec((nch, SUBLANES, SSD_CONV), lambda i, k, e, ib, ob: (ib[i], 0, 0)),
            pl.BlockSpec((nch, SSD_PAIRS, hp, SSD_STATE), lambda i, k, e, ib, ob: (ib[i], 0, 0, 0)),
            pl.BlockSpec((SUBLANES, SSD_CONV), lambda i, *_: (0, 0)),
            pl.BlockSpec((SUBLANES, LANES), lambda i, *_: (0, 0)),
            pl.BlockSpec((2 * SSD_HEADS, LANES), lambda i, *_: (0, 0)),
            pl.BlockSpec((1, SSD_WIDTH), lambda i, *_: (0, 0)),
            pl.BlockSpec((1, SSD_WIDTH), lambda i, *_: (0, 0)),
        ],
        out_specs=[
            pl.BlockSpec((l, SSD_WIDTH), lambda i, *_: (i, 0)),
            pl.BlockSpec((nch, SSD_PAIRS, hp, SSD_STATE), lambda i, k, e, ib, ob: (ob[i], 0, 0, 0)),
        ],
        scratch_shapes=[
            pltpu.VMEM((SUBLANES, SSD_CONV), F32),
            pltpu.VMEM((l, SSD_CONV), F32),
            pltpu.VMEM((2 * SUBLANES, SSD_CONV), F32),
            pltpu.VMEM((SSD_PAIRS, hp, SSD_STATE), F32),
            pltpu.VMEM((l, LANES), F32),
            pltpu.VMEM((l, LANES), F32),
            pltpu.VMEM((nch, SSD_PAIRS, hp), F32),
        ],
    )
    return pl.pallas_call(
        _ssd_kernel,
        grid_spec=grid_spec,
        out_shape=[jax.ShapeDtypeStruct((t, SSD_WIDTH), BF16),
                   jax.ShapeDtypeStruct((n_out_blk * nch, SSD_PAIRS, hp, SSD_STATE), F32)],
        compiler_params=_cparams(("arbitrary",)),
        name="ssd_mixer",
    )(jnp.asarray(kind), jnp.asarray(emit), jnp.asarray(in_blk), jnp.asarray(out_blk),
      proj, proj, proj, cache_slots, s0_slots, cw, parr, parc,
      norm_w.reshape(1, SSD_WIDTH).astype(F32), dsk)


S5_SB = 2
S5_LB = 512


def _s5_kernel(kind_ref, u_ref, h0r_ref, h0i_ref, perm_ref, permt_ref, bre_ref, bim_ref, cre_ref, cim_ref,
               ar_ref, ai_ref, dsk_ref, wglu_ref, bglu_ref,
               o_ref, hfr_ref, hfi_ref,
               bur, bui, pre, pim, cr, ci, inr, ini):
    i = pl.program_id(0)
    l = u_ref.shape[0]
    nch = l // CHUNK
    n = S5_N
    usb = S5_WIDTH // S5_SB
    nsb = n // S5_SB

    @pl.when(i == 0)
    def _():
        pre[0:1, :] = ar_ref[...]
        pim[0:1, :] = ai_ref[...]

        def pw(t, carry):
            pr = pre[pl.ds(t - 1, 1), :]
            pi = pim[pl.ds(t - 1, 1), :]
            pre[pl.ds(t, 1), :] = pr * ar_ref[...] - pi * ai_ref[...]
            pim[pl.ds(t, 1), :] = pr * ai_ref[...] + pi * ar_ref[...]
            return carry

        lax.fori_loop(1, CHUNK, pw, 0)
        cr[...] = jnp.zeros((1, n), F32)
        ci[...] = jnp.zeros((1, n), F32)

    up = _dot(perm_ref[...], u_ref[...].astype(BF16)).astype(BF16)
    for sb in range(S5_SB):
        us = up[:, sb * usb:(sb + 1) * usb]
        bur[:, sb * nsb:(sb + 1) * nsb] = _dot(us, bre_ref[sb])
        bui[:, sb * nsb:(sb + 1) * nsb] = _dot(us, bim_ref[sb])

    for c0 in range(0, n, S5_LB):
        a_r = jnp.broadcast_to(ar_ref[:, c0:c0 + S5_LB], (nch, S5_LB))
        a_i = jnp.broadcast_to(ai_ref[:, c0:c0 + S5_LB], (nch, S5_LB))

        def step(t, carry):
            hr, hi = carry
            r0 = pl.multiple_of(t * nch, nch)
            nr = a_r * hr - a_i * hi + bur[pl.ds(r0, nch), c0:c0 + S5_LB]
            ni = a_r * hi + a_i * hr + bui[pl.ds(r0, nch), c0:c0 + S5_LB]
            bur[pl.ds(r0, nch), c0:c0 + S5_LB] = nr
            bui[pl.ds(r0, nch), c0:c0 + S5_LB] = ni
            return nr, ni

        z = jnp.zeros((nch, S5_LB), F32)
        lax.fori_loop(0, CHUNK, step, (z, z))

    a64r = pre[CHUNK - 1:CHUNK, :]
    a64i = pim[CHUNK - 1:CHUNK, :]
    c_r = cr[...]
    c_i = ci[...]
    for s in range(nch):
        start = kind_ref[i * nch + s] == 1
        i_r = jnp.where(start, h0r_ref[s:s + 1, :], c_r)
        i_i = jnp.where(start, h0i_ref[s:s + 1, :], c_i)
        inr[s:s + 1, :] = i_r
        ini[s:s + 1, :] = i_i
        e_r = bur[l - nch + s:l - nch + s + 1, :]
        e_i = bui[l - nch + s:l - nch + s + 1, :]
        c_r = a64r * i_r - a64i * i_i + e_r
        c_i = a64r * i_i + a64i * i_r + e_i
        hfr_ref[s:s + 1, :] = c_r
        hfi_ref[s:s + 1, :] = c_i
    cr[...] = c_r
    ci[...] = c_i

    for c0 in range(0, n, S5_LB):
        n_r = inr[:, c0:c0 + S5_LB]
        n_i = ini[:, c0:c0 + S5_LB]

        def fix(t, carry):
            r0 = pl.multiple_of(t * nch, nch)
            p_r = pre[pl.ds(t, 1), c0:c0 + S5_LB]
            p_i = pim[pl.ds(t, 1), c0:c0 + S5_LB]
            bur[pl.ds(r0, nch), c0:c0 + S5_LB] += p_r * n_r - p_i * n_i
            bui[pl.ds(r0, nch), c0:c0 + S5_LB] += p_r * n_i + p_i * n_r
            return carry

        lax.fori_loop(0, CHUNK, fix, 0)

    ys = []
    for sb in range(S5_SB):
        hr = bur[:, sb * nsb:(sb + 1) * nsb].astype(BF16)
        hi = bui[:, sb * nsb:(sb + 1) * nsb].astype(BF16)
        ys.append(_dot(hr, cre_ref[sb]) - _dot(hi, cim_ref[sb]))
    yp = jnp.concatenate(ys, axis=1)
    y_hi = yp.astype(BF16)
    r1 = yp - y_hi.astype(F32)
    y_mid = r1.astype(BF16)
    y_lo = (r1 - y_mid.astype(F32)).astype(BF16)
    pt = permt_ref[...]
    y = (_dot(pt, y_hi) + _dot(pt, y_mid)) + _dot(pt, y_lo)
    y = y + u_ref[...] * dsk_ref[...]
    y = y * (0.5 * (1.0 + jnp.tanh(math.sqrt(2.0 / math.pi) * (y + 0.044715 * (y * y * y)))))
    out = y * jax.nn.sigmoid(_dot(y.astype(BF16), wglu_ref[...]) + bglu_ref[...])
    o_ref[...] = out.astype(BF16)


def _s5_tables(a_re, a_im, b_re, b_im, c_re, c_im, log_dt):
    a_re, a_im = a_re.astype(F32), a_im.astype(F32)
    dt = jnp.exp(log_dt.astype(F32))[:, None]
    mag = jnp.exp(dt * a_re)
    abar_re, abar_im = mag * jnp.cos(dt * a_im), mag * jnp.sin(dt * a_im)
    den = a_re * a_re + a_im * a_im
    num_re, num_im = abar_re - 1.0, abar_im
    zoh_re = (num_re * a_re + num_im * a_im) / den
    zoh_im = (num_im * a_re - num_re * a_im) / den
    b_re, b_im = b_re.astype(F32), b_im.astype(F32)
    bbar_re = zoh_re[..., None] * b_re - zoh_im[..., None] * b_im
    bbar_im = zoh_re[..., None] * b_im + zoh_im[..., None] * b_re
    gsb = S5_GROUPS // S5_SB
    eye = jnp.eye(gsb, dtype=F32)

    def bblk(b):
        b = b.reshape(S5_SB, gsb, S5_STATE, S5_GROUP)
        return jnp.einsum('sgpc,gh->sgchp', b, eye).reshape(S5_SB, gsb * S5_GROUP, gsb * S5_STATE).astype(BF16)

    def cblk(c):
        c = c.astype(F32).reshape(S5_SB, gsb, S5_GROUP, S5_STATE)
        return jnp.einsum('sgcp,gh->sgphc', c, eye).reshape(S5_SB, gsb * S5_STATE, gsb * S5_GROUP).astype(BF16)

    return (abar_re.reshape(1, S5_N), abar_im.reshape(1, S5_N),
            bblk(bbar_re), bblk(bbar_im), cblk(c_re), cblk(c_im))


def _s5_perm():
    r_new = np.arange(MIX_TILE)
    r_old = (r_new % TILE_CHUNKS) * CHUNK + r_new // TILE_CHUNKS
    p = np.zeros((MIX_TILE, MIX_TILE), np.float32)
    p[r_new, r_old] = 1.0
    return jnp.asarray(p, BF16), jnp.asarray(p.T, BF16)


def _s5(proj, h0r, h0i, tabs, d_skip, w_glu, b_glu, kind):
    t = proj.shape[0]
    l = MIX_TILE
    nch = TILE_CHUNKS
    abr, abi, bre, bim, cre, cim = tabs
    perm, permt = _s5_perm()
    full2 = lambda a: pl.BlockSpec(a.shape, lambda i, *_: (0, 0))
    full3 = lambda a: pl.BlockSpec(a.shape, lambda i, *_: (0, 0, 0))
    dsk = d_skip.astype(F32).reshape(1, S5_WIDTH)
    wg = w_glu.astype(BF16)
    bg = b_glu.astype(F32).reshape(1, S5_WIDTH)
    grid_spec = pltpu.PrefetchScalarGridSpec(
        num_scalar_prefetch=1,
        grid=(t // l,),
        in_specs=[
            pl.BlockSpec((l, S5_WIDTH), lambda i, *_: (i, P_UB // S5_WIDTH)),
            pl.BlockSpec((nch, S5_N), lambda i, *_: (i, 0)),
            pl.BlockSpec((nch, S5_N), lambda i, *_: (i, 0)),
            full2(perm), full2(permt), full3(bre), full3(bim), full3(cre), full3(cim),
            full2(abr), full2(abi), full2(dsk), full2(wg), full2(bg),
        ],
        out_specs=[
            pl.BlockSpec((l, S5_WIDTH), lambda i, *_: (i, 0)),
            pl.BlockSpec((nch, S5_N), lambda i, *_: (i, 0)),
            pl.BlockSpec((nch, S5_N), lambda i, *_: (i, 0)),
        ],
        scratch_shapes=[
            pltpu.VMEM((l, S5_N), F32), pltpu.VMEM((l, S5_N), F32),
            pltpu.VMEM((CHUNK, S5_N), F32), pltpu.VMEM((CHUNK, S5_N), F32),
            pltpu.VMEM((1, S5_N), F32), pltpu.VMEM((1, S5_N), F32),
            pltpu.VMEM((nch, S5_N), F32), pltpu.VMEM((nch, S5_N), F32),
        ],
    )
    nseg = t // CHUNK
    return pl.pallas_call(
        _s5_kernel,
        grid_spec=grid_spec,
        out_shape=[jax.ShapeDtypeStruct((t, S5_WIDTH), BF16),
                   jax.ShapeDtypeStruct((nseg, S5_N), F32),
                   jax.ShapeDtypeStruct((nseg, S5_N), F32)],
        compiler_params=_cparams(("arbitrary",)),
        name="s5_mixer",
    )(jnp.asarray(kind), proj, h0r, h0i, perm, permt, bre, bim, cre, cim, abr, abi, dsk, wg, bg)


MIXOUT_TM = 512
R_GRP = 0
R_EXP = SUBLANES
NEG_BIG = -1e30


def _mixout_kernel(oa_ref, ob_ref, oc_ref, w_ref, x_ref, nw_ref, wr_ref, rb_ref, x1_ref, h2_ref, lg_ref):
    acc = _dot(oa_ref[...], w_ref[0:GDN_KD, :])
    acc = acc + _dot(ob_ref[...], w_ref[GDN_KD:GDN_KD + S5_WIDTH, :])
    acc = acc + _dot(oc_ref[...], w_ref[GDN_KD + S5_WIDTH:, :])
    x1 = x_ref[...] + acc
    x1_ref[...] = x1
    h = x1 * lax.rsqrt(jnp.mean(x1 * x1, axis=-1, keepdims=True) + EPS) * nw_ref[...]
    h2_ref[...] = h
    lg_ref[...] = _dot(h.astype(BF16), wr_ref[...]) + rb_ref[...]


def _mixout(oa, ob, oc, w_out, x, nw, wr, rb):
    t, d = x.shape
    tm = MIXOUT_TM
    row = lambda w: pl.BlockSpec((tm, w), lambda i: (i, 0))
    full = lambda a: pl.BlockSpec(a.shape, lambda i: (0, 0))
    return pl.pallas_call(
        _mixout_kernel,
        grid=(t // tm,),
        in_specs=[row(GDN_KD), row(S5_WIDTH), row(SSD_WIDTH), full(w_out), row(d), full(nw), full(wr), full(rb)],
        out_specs=[row(d), row(d), row(LANES)],
        out_shape=[jax.ShapeDtypeStruct((t, d), F32), jax.ShapeDtypeStruct((t, d), F32),
                   jax.ShapeDtypeStruct((t, LANES), F32)],
        compiler_params=_cparams(("parallel",)),
        name="mix_out",
    )(oa, ob, oc, w_out, x, nw, wr, rb)


def _router_weights(rg_w, rg_b, re_w, re_b):
    d = rg_w.shape[0]
    wr = jnp.concatenate([rg_w, jnp.zeros((d, R_EXP - N_GROUPS), F32), re_w,
                          jnp.zeros((d, LANES - R_EXP - N_EXPERTS), F32)], axis=1).astype(BF16)
    rb = jnp.concatenate([rg_b.astype(F32), jnp.full((R_EXP - N_GROUPS,), NEG_BIG, F32), re_b.astype(F32),
                          jnp.zeros((LANES - R_EXP - N_EXPERTS,), F32)]).reshape(1, LANES)
    return wr, rb


ROUTE_TM = 512


def _router_kernel(lg_ref, tri_ref, idx_ref, gate_ref, cnt_ref, run):
    i = pl.program_id(0)
    tm = lg_ref.shape[0]

    @pl.when(i == 0)
    def _():
        run[...] = jnp.zeros(run.shape, F32)

    lt = lg_ref[...].T
    row8 = lax.broadcasted_iota(I32, (SUBLANES, tm), 0)
    grp = lt[R_GRP:R_GRP + SUBLANES, :]
    gm = jnp.max(grp, axis=0, keepdims=True)
    gp_top = 1.0 / jnp.sum(jnp.exp(grp - gm), axis=0, keepdims=True)
    g_top = jnp.min(jnp.where(grp == gm, row8, SUBLANES), axis=0, keepdims=True)
    ing = jnp.zeros((EPG, tm), F32)
    for g in range(N_GROUPS):
        ing = jnp.where(g_top == g, lt[R_EXP + g * EPG:R_EXP + (g + 1) * EPG, :], ing)
    em = jnp.max(ing, axis=0, keepdims=True)
    ee = jnp.exp(ing - em)
    p = ee / jnp.sum(ee, axis=0, keepdims=True)
    v1 = jnp.max(p, axis=0, keepdims=True)
    i1 = jnp.min(jnp.where(p == v1, row8, EPG), axis=0, keepdims=True)
    p2 = jnp.where(row8 == i1, -1.0, p)
    v2 = jnp.max(p2, axis=0, keepdims=True)
    i2 = jnp.min(jnp.where(p2 == v2, row8, EPG), axis=0, keepdims=True)
    den = v1 + v2
    gate1 = gp_top * v1 / den
    gate2 = gp_top * v2 / den
    e1 = g_top * EPG + i1
    e2 = g_top * EPG + i2

    erow = lax.broadcasted_iota(I32, (N_EXPERTS, tm), 0)
    hit1 = erow == e1
    hit2 = erow == e2
    oh = jnp.where(hit1 | hit2, 1.0, 0.0)
    before = _dot(oh.astype(BF16), tri_ref[...]) + run[:, 0:1]
    rank1 = jnp.sum(jnp.where(hit1, before, 0.0), axis=0, keepdims=True).astype(I32)
    rank2 = jnp.sum(jnp.where(hit2, before, 0.0), axis=0, keepdims=True).astype(I32)
    run[...] = run[...] + jnp.sum(oh, axis=1, keepdims=True)
    cnt_ref[...] = run[...].astype(I32)

    zi = jnp.zeros((SUBLANES - 4, tm), I32)
    idx_ref[...] = jnp.concatenate([e1, e2, rank1, rank2, zi], axis=0)
    r128 = lax.broadcasted_iota(I32, (LANES, tm), 0)
    gt = jnp.where(r128 == 0, gate1, jnp.where(r128 == 1, gate2, 0.0))
    gate_ref[...] = gt.T


def _router(logits):
    t = logits.shape[0]
    tm = ROUTE_TM
    tri = jnp.asarray(np.triu(np.ones((tm, tm), np.float32), 1), BF16)
    return pl.pallas_call(
        _router_kernel,
        grid=(t // tm,),
        in_specs=[pl.BlockSpec((tm, LANES), lambda i: (i, 0)),
                  pl.BlockSpec((tm, tm), lambda i: (0, 0))],
        out_specs=[pl.BlockSpec((SUBLANES, tm), lambda i: (0, i)),
                   pl.BlockSpec((tm, LANES), lambda i: (i, 0)),
                   pl.BlockSpec((N_EXPERTS, LANES), lambda i: (0, 0))],
        out_shape=[jax.ShapeDtypeStruct((SUBLANES, t), I32),
                   jax.ShapeDtypeStruct((t, LANES), F32),
                   jax.ShapeDtypeStruct((N_EXPERTS, LANES), I32)],
        scratch_shapes=[pltpu.VMEM((N_EXPERTS, LANES), F32)],
        compiler_params=_cparams(("arbitrary",)),
        name="router",
    )(logits, tri)


MOE_BLOCK = 256
DISPATCH_TM = 512
COMBINE_TM = 256


def _dispatch_kernel(pstart_ref, idx_ref, h2_ref, xs_in_ref, xs_ref, sem):
    del xs_in_ref
    i = pl.program_id(0)
    tm = idx_ref.shape[1]

    def copy(t, k):
        slot = pstart_ref[idx_ref[k, t]] + idx_ref[2 + k, t]
        return pltpu.make_async_copy(h2_ref.at[pl.ds(t, 1)], xs_ref.at[pl.ds(slot, 1)], sem)

    def issue(t, carry):
        copy(t, 0).start()
        copy(t, 1).start()
        return carry

    def drain(t, carry):
        copy(t, 0).wait()
        copy(t, 1).wait()
        return carry

    lax.fori_loop(0, tm, issue, 0)
    lax.fori_loop(0, tm, drain, 0)


def _dispatch(pad_start, idx, h2, n_slots):
    t, d = h2.shape
    tm = DISPATCH_TM
    xs0 = jnp.zeros((n_slots, d), F32)
    grid_spec = pltpu.PrefetchScalarGridSpec(
        num_scalar_prefetch=1,
        grid=(t // tm,),
        in_specs=[pl.BlockSpec((SUBLANES, tm), lambda i, *_: (0, i), memory_space=pltpu.SMEM),
                  pl.BlockSpec((tm, d), lambda i, *_: (i, 0)),
                  pl.BlockSpec(memory_space=pl.ANY)],
        out_specs=pl.BlockSpec(memory_space=pl.ANY),
        scratch_shapes=[pltpu.SemaphoreType.DMA(())],
    )
    return pl.pallas_call(
        _dispatch_kernel,
        grid_spec=grid_spec,
        out_shape=jax.ShapeDtypeStruct((n_slots, d), F32),
        input_output_aliases={3: 0},
        compiler_params=_cparams(("arbitrary",)),
        name="moe_dispatch",
    )(pad_start, idx, h2, xs0)


def _expert_kernel(be_ref, nv_ref, xs_ref, wg_ref, wu_ref, wd_ref, ys_ref):
    b = pl.program_id(0)

    @pl.when(b < nv_ref[0])
    def _():
        x = xs_ref[...].astype(BF16)
        g = _dot(x, wg_ref[0])
        u = _dot(x, wu_ref[0])
        h = (_silu(g) * u).astype(BF16)
        ys_ref[...] = _dot(h, wd_ref[0])

    @pl.when(b >= nv_ref[0])
    def _():
        ys_ref[...] = jnp.zeros(ys_ref.shape, F32)


def _experts(block_expert, n_valid, xs, wg, wu, wd):
    n_slots, d = xs.shape
    de = wg.shape[2]
    nb = n_slots // MOE_BLOCK
    blk = lambda b, be, nv: (jnp.minimum(b, nv[0] - 1), 0)
    wsel = lambda b, be, nv: (be[jnp.minimum(b, nv[0] - 1)], 0, 0)
    grid_spec = pltpu.PrefetchScalarGridSpec(
        num_scalar_prefetch=2,
        grid=(nb,),
        in_specs=[pl.BlockSpec((MOE_BLOCK, d), blk),
                  pl.BlockSpec((1, d, de), wsel),
                  pl.BlockSpec((1, d, de), wsel),
                  pl.BlockSpec((1, de, d), wsel)],
        out_specs=pl.BlockSpec((MOE_BLOCK, d), lambda b, be, nv: (b, 0)),
    )
    return pl.pallas_call(
        _expert_kernel,
        grid_spec=grid_spec,
        out_shape=jax.ShapeDtypeStruct((n_slots, d), F32),
        compiler_params=_cparams(("arbitrary",)),
        name="moe_experts",
    )(block_expert, n_valid, xs, wg, wu, wd)


def _combine_kernel(pstart_ref, idx_ref, x1_ref, gate_ref, ys_ref, nw_ref, out_ref, ybuf, sem, *, final):
    tm = x1_ref.shape[0]

    def copy(t, k):
        slot = pstart_ref[idx_ref[k, t]] + idx_ref[2 + k, t]
        return pltpu.make_async_copy(ys_ref.at[pl.ds(slot, 1)], ybuf.at[k, pl.ds(t, 1)], sem)

    def issue(t, carry):
        copy(t, 0).start()
        copy(t, 1).start()
        return carry

    def drain(t, carry):
        copy(t, 0).wait()
        copy(t, 1).wait()
        return carry

    lax.fori_loop(0, tm, issue, 0)
    lax.fori_loop(0, tm, drain, 0)
    g = gate_ref[...]
    y = ybuf[0] * g[:, 0:1] + ybuf[1] * g[:, 1:2]
    x2 = x1_ref[...] + y
    if final:
        x2 = x2 * lax.rsqrt(jnp.mean(x2 * x2, axis=-1, keepdims=True) + EPS) * nw_ref[...]
    out_ref[...] = x2


def _combine(pad_start, idx, x1, gates, ys, nw, final):
    t, d = x1.shape
    tm = COMBINE_TM
    grid_spec = pltpu.PrefetchScalarGridSpec(
        num_scalar_prefetch=1,
        grid=(t // tm,),
        in_specs=[pl.BlockSpec((SUBLANES, tm), lambda i, *_: (0, i), memory_space=pltpu.SMEM),
                  pl.BlockSpec((tm, d), lambda i, *_: (i, 0)),
                  pl.BlockSpec((tm, LANES), lambda i, *_: (i, 0)),
                  pl.BlockSpec(memory_space=pl.ANY),
                  pl.BlockSpec((1, d), lambda i, *_: (0, 0))],
        out_specs=pl.BlockSpec((tm, d), lambda i, *_: (i, 0)),
        scratch_shapes=[pltpu.VMEM((2, tm, d), F32), pltpu.SemaphoreType.DMA(())],
    )
    return pl.pallas_call(
        functools.partial(_combine_kernel, final=final),
        grid_spec=grid_spec,
        out_shape=jax.ShapeDtypeStruct((t, d), F32),
        compiler_params=_cparams(("arbitrary",)),
        name="moe_combine",
    )(pad_start, idx, x1, gates, ys, nw)


def _moe(x1, h2, logits, wg, wu, wd, norm_final, final):
    t, d = x1.shape
    idx, gates, cnt = _router(logits)
    counts = cnt[:, 0]
    padded = (counts + MOE_BLOCK - 1) // MOE_BLOCK * MOE_BLOCK
    pad_end = jnp.cumsum(padded)
    pad_start = (pad_end - padded).astype(I32)
    nb = (2 * t + N_EXPERTS * (MOE_BLOCK - 1) + MOE_BLOCK - 1) // MOE_BLOCK
    n_valid = (pad_end[-1] // MOE_BLOCK).astype(I32).reshape(1)
    block_expert = jnp.minimum(
        jnp.searchsorted(pad_end, jnp.arange(nb, dtype=I32) * MOE_BLOCK, side='right'),
        N_EXPERTS - 1).astype(I32)
    xs = _dispatch(pad_start, idx, h2, nb * MOE_BLOCK)
    ys = _experts(block_expert, n_valid, xs, wg, wu, wd)
    return _combine(pad_start, idx, x1, gates, ys, norm_final, final)


def _stream_ends(nbp, seq, nbs, dseq):
    ends = [(b + 1) * seq for b in range(nbp)] + [nbp * seq + (s + 1) * dseq for s in range(nbs)]
    return np.asarray(ends)


def kernel(x_prompt, x_sample, cache_conv_gdn, state_gdn, state_s5, cache_conv_ssd, state_ssd, norm_mix, w_in, gdn_conv_w, gdn_a_log, gdn_dt_bias, gdn_norm, s5_a_re, s5_a_im, s5_b_re, s5_b_im, s5_c_re, s5_c_im, s5_log_dt, s5_d, s5_w_glu, s5_b_glu, ssd_conv_w, ssd_conv_b, ssd_a_log, ssd_dt_bias, ssd_d, ssd_norm, w_out, norm_ffn, router_group_w, router_group_b, router_expert_w, router_expert_b, expert_w_gate, expert_w_up, expert_w_down, norm_final):
    nbp, seq, d = x_prompt.shape
    nbs, dseq, _ = x_sample.shape
    depth = w_in.shape[0]
    tp = nbp * seq
    t = tp + nbs * dseq
    x = jnp.concatenate([x_prompt.reshape(tp, d), x_sample.reshape(nbs * dseq, d)], axis=0)

    tables = _chunk_tables(nbp, seq, nbs, dseq)
    kind = tables[0]
    n_out_blk = nbp + nbs // TILE_CHUNKS
    ends = _stream_ends(nbp, seq, nbs, dseq)
    tail_rows = (ends[:, None] + np.arange(-(CONV_K - 1), 0)[None, :]).reshape(-1)
    end_seg = ends // CHUNK - 1
    ncp = tp // CHUNK
    state_rows = np.concatenate([np.arange(nbp) * TILE_CHUNKS, nbp * TILE_CHUNKS + np.arange(nbs)])

    new_conv_gdn, new_gdn, new_s5, new_conv_ssd, new_ssd = [], [], [], [], []
    for l in range(depth):
        proj = _proj(x, norm_mix[l].reshape(1, d).astype(F32), _rearrange_w_in(w_in[l]))

        oa, sg = _gdn(proj, _conv_cache_slots(cache_conv_gdn[l].astype(F32)),
                      _init_slots(state_gdn[l].astype(F32)), gdn_conv_w[l].astype(F32), gdn_a_log[l],
                      gdn_dt_bias[l], gdn_norm[l], tables, n_out_blk)

        h0 = state_s5[l].astype(F32).reshape(nbs, S5_N, 2)
        zeros_p = jnp.zeros((ncp, S5_N), F32)
        ob, hfr, hfi = _s5(proj, jnp.concatenate([zeros_p, h0[..., 0]], axis=0),
                           jnp.concatenate([zeros_p, h0[..., 1]], axis=0),
                           _s5_tables(s5_a_re[l], s5_a_im[l], s5_b_re[l], s5_b_im[l], s5_c_re[l], s5_c_im[l],
                                      s5_log_dt[l]),
                           s5_d[l], s5_w_glu[l], s5_b_glu[l], kind)

        oc, ss = _ssd(proj, _conv_cache_slots(cache_conv_ssd[l].astype(F32)),
                      _init_slots(state_ssd[l].astype(F32).reshape(nbs, SSD_PAIRS, 2 * SSD_HEADDIM, SSD_STATE)),
                      ssd_conv_w[l].astype(F32), ssd_conv_b[l].astype(F32), ssd_a_log[l], ssd_dt_bias[l],
                      ssd_d[l], ssd_norm[l], tables, n_out_blk)

        wr, rb = _router_weights(router_group_w[l].astype(F32), router_group_b[l],
                                 router_expert_w[l].astype(F32), router_expert_b[l])
        x1, h2, logits = _mixout(oa, ob, oc, w_out[l].astype(BF16), x,
                                 norm_ffn[l].reshape(1, d).astype(F32), wr, rb)
        x = _moe(x1, h2, logits, expert_w_gate[l].astype(BF16), expert_w_up[l].astype(BF16),
                 expert_w_down[l].astype(BF16), norm_final.reshape(1, d).astype(F32), l == depth - 1)

        tails = proj[tail_rows]
        new_conv_gdn.append(tails[:, P_QKV:P_QKV + GDN_CONV].reshape(nbp + nbs, CONV_K - 1, GDN_CONV))
        new_conv_ssd.append(tails[:, P_XBC:P_XBC + SSD_CONV].reshape(nbp + nbs, CONV_K - 1, SSD_CONV))
        new_gdn.append(sg[state_rows])
        new_ssd.append(ss[state_rows].reshape(nbp + nbs, SSD_HEADS, SSD_HEADDIM, SSD_STATE))
        new_s5.append(jnp.stack([hfr[end_seg], hfi[end_seg]], axis=-1)
                      .reshape(nbp + nbs, S5_GROUPS, S5_STATE, 2))

    def split(parts):
        a = jnp.stack(parts)
        return a[:, :nbp], a[:, nbp:]

    cg_p, cg_s = split(new_conv_gdn)
    sg_p, sg_s = split(new_gdn)
    s5_p, s5_s = split(new_s5)
    cs_p, cs_s = split(new_conv_ssd)
    ss_p, ss_s = split(new_ssd)
    y_prompt = x[:tp].reshape(nbp, seq, d)
    y_sample = x[tp:].reshape(nbs, dseq, d)
    return (y_prompt, y_sample, cg_p, sg_p, s5_p, cs_p, ss_p, cg_s, sg_s, s5_s, cs_s, ss_s)
```

```python
import functools
import math

import numpy as np
import jax
import jax.numpy as jnp
from jax import lax
from jax.experimental import pallas as pl
from jax.experimental.pallas import tpu as pltpu

F32 = jnp.float32
BF16 = jnp.bfloat16
I32 = jnp.int32

EPS = 1e-6
CHUNK = 64
CHUNK_SHIFT = 6
CONV_K = 4
LANES = 128
SUBLANES = 8
VMEM_LIMIT = 56 * 1024 * 1024

GDN_HEADS = 8
GDN_DK = 128
GDN_DV = 128
GDN_KD = GDN_HEADS * GDN_DK
GDN_CONV = 3 * GDN_KD
GDN_GROUP = 4
GDN_SOLVE_UNROLL = 2
S5_WIDTH = 512
S5_GROUPS = 32
S5_GROUP = 16
S5_STATE = 64
S5_N = S5_GROUPS * S5_STATE
SSD_WIDTH = 512
SSD_HEADS = 8
SSD_HEADDIM = 64
SSD_NGROUPS = 2
SSD_STATE = 128
SSD_CONV = SSD_WIDTH + 2 * SSD_NGROUPS * SSD_STATE
N_GROUPS = 4
EPG = 8
N_EXPERTS = 32

MIX_TILE = 512
TILE_CHUNKS = MIX_TILE // CHUNK


def _cparams(sem, vmem=VMEM_LIMIT):
    return pltpu.CompilerParams(dimension_semantics=sem, vmem_limit_bytes=vmem)


def _silu(x):
    return x * jax.nn.sigmoid(x)


def _softplus(x):
    return jnp.maximum(x, 0.0) + jnp.log1p(jnp.exp(-jnp.abs(x)))


def _dot(a, b):
    return jnp.dot(a, b, preferred_element_type=F32)


def _dot3(a, b):
    ah = a.astype(BF16)
    bh = b.astype(BF16)
    al = (a - ah.astype(F32)).astype(BF16)
    bl = (b - bh.astype(F32)).astype(BF16)
    return _dot(ah, bh) + (_dot(ah, bl) + _dot(al, bh))


def _dot_nt(a, b):
    return lax.dot_general(a, b, (((1,), (1,)), ((), ())), preferred_element_type=F32)


def _dot_tn(a, b):
    return lax.dot_general(a, b, (((0,), (0,)), ((), ())), preferred_element_type=F32)


def _cumsum_rows(x):
    row = lax.broadcasted_iota(I32, x.shape, 0) & (CHUNK - 1)
    k = 1
    while k < CHUNK:
        x = x + jnp.where(row >= k, pltpu.roll(x, k, 0), 0.0)
        k *= 2
    return x


def _cumsum_lanes(x):
    lane = lax.broadcasted_iota(I32, x.shape, 1) & (CHUNK - 1)
    k = 1
    while k < CHUNK:
        x = x + jnp.where(lane >= k, pltpu.roll(x, k, 1), 0.0)
        k *= 2
    return x


PROJ_TM = 1024
PROJ_TN = 1280
P_QKV, P_ZA, P_UB, P_ZC, P_XBC = 0, 3072, 4096, 4608, 5120
P_MAIN = 6144
P_TOTAL = 6400
P_SMALL = P_TOTAL - LANES


def _proj_kernel(x_ref, nw_ref, w_ref, o_ref, h_scr):
    @pl.when(pl.program_id(1) == 0)
    def _():
        x = x_ref[...]
        ms = jnp.mean(x * x, axis=-1, keepdims=True)
        h_scr[...] = (x * lax.rsqrt(ms + EPS) * nw_ref[...]).astype(BF16)

    o_ref[...] = _dot(h_scr[...], w_ref[...])


def _proj(x, nw, w):
    t, d = x.shape
    n = w.shape[1]
    tm = min(PROJ_TM, t)
    assert t % tm == 0 and n % PROJ_TN == 0
    return pl.pallas_call(
        _proj_kernel,
        grid=(t // tm, n // PROJ_TN),
        in_specs=[pl.BlockSpec((tm, d), lambda i, j: (i, 0)),
                  pl.BlockSpec((1, d), lambda i, j: (0, 0)),
                  pl.BlockSpec((d, PROJ_TN), lambda i, j: (0, j))],
        out_specs=pl.BlockSpec((tm, PROJ_TN), lambda i, j: (i, j)),
        out_shape=jax.ShapeDtypeStruct((t, n), F32),
        scratch_shapes=[pltpu.VMEM((tm, d), BF16)],
        compiler_params=_cparams(("parallel", "arbitrary")),
        name="proj_in",
    )(x, nw, w)


def _rearrange_w_in(w_in):
    d = w_in.shape[0]
    off_za = GDN_CONV
    off_ba = off_za + GDN_KD
    off_s5 = off_ba + 2 * GDN_HEADS
    off_zc = off_s5 + S5_WIDTH
    off_xbc = off_zc + SSD_WIDTH
    off_dt = off_xbc + SSD_CONV
    small = jnp.concatenate([w_in[:, off_ba:off_s5], w_in[:, off_dt:],
                             jnp.zeros((d, LANES - 3 * GDN_HEADS), w_in.dtype)], axis=1)
    w = jnp.concatenate([w_in[:, :off_ba], w_in[:, off_s5:off_dt],
                         jnp.zeros((d, P_TOTAL - P_MAIN - LANES), w_in.dtype), small], axis=1)
    return w.astype(BF16)


def _chunk_tables(nbp, seq, nbs, dseq):
    assert seq % MIX_TILE == 0 and dseq == CHUNK and (nbs * dseq) % MIX_TILE == 0
    cps = seq // CHUNK
    ncp = nbp * cps
    nc = ncp + nbs
    kind = np.zeros((nc,), np.int32)
    emit = np.full((nc,), -1, np.int32)
    for c in range(nc):
        if c < ncp:
            kind[c] = 1 if c % cps == 0 else 0
            if c % TILE_CHUNKS == TILE_CHUNKS - 1:
                emit[c] = 0
        else:
            kind[c] = 1
            emit[c] = (c - ncp) % TILE_CHUNKS
    ntp = ncp // TILE_CHUNKS
    nts = nbs // TILE_CHUNKS
    in_blk = np.concatenate([np.zeros((ntp,), np.int32), 1 + np.arange(nts, dtype=np.int32)])
    out_blk = np.concatenate([np.arange(ntp, dtype=np.int32) // (cps // TILE_CHUNKS),
                              nbp + np.arange(nts, dtype=np.int32)])
    return kind, emit, in_blk, out_blk


def _init_slots(x):
    return jnp.concatenate([jnp.zeros((TILE_CHUNKS,) + x.shape[1:], x.dtype), x], axis=0)


def _conv_cache_slots(cache):
    nbs, k1, c = cache.shape
    padded = jnp.concatenate([jnp.zeros((nbs, SUBLANES - k1, c), cache.dtype), cache], axis=1)
    return _init_slots(padded)


def _conv_silu_tile(i, kind_ref, in_ref, cache_ref, cw_ref, tail, act, tmp, bias_row):
    l, c = in_ref.shape
    cb = 512
    k1 = CONV_K - 1
    nch = l // CHUNK

    @pl.when(i == 0)
    def _():
        tail[...] = jnp.zeros(tail.shape, F32)

    def taps(src, lo, hi, c0):
        acc = src[lo - k1:hi - k1, c0:c0 + cb] * cw_ref[0:1, c0:c0 + cb]
        for j in range(1, CONV_K):
            acc = acc + src[lo - k1 + j:hi - k1 + j, c0:c0 + cb] * cw_ref[j:j + 1, c0:c0 + cb]
        if bias_row is not None:
            acc = acc + cw_ref[bias_row:bias_row + 1, c0:c0 + cb]
        return _silu(acc)

    def head_rows(prev, r0):
        tmp[0:SUBLANES, :] = prev
        tmp[SUBLANES:2 * SUBLANES, :] = in_ref[r0:r0 + SUBLANES, :]
        for c0 in range(0, c, cb):
            act[r0:r0 + SUBLANES, c0:c0 + cb] = taps(tmp, SUBLANES, 2 * SUBLANES, c0)

    for rb in range(nch):
        lo = SUBLANES if rb == 0 else rb * CHUNK
        for c0 in range(0, c, cb):
            act[lo:(rb + 1) * CHUNK, c0:c0 + cb] = taps(in_ref, lo, (rb + 1) * CHUNK, c0)
    head_rows(tail[...], 0)

    for cl in range(nch):
        @pl.when(kind_ref[i * nch + cl] == 1)
        def _():
            head_rows(cache_ref[cl], cl * CHUNK)

    tail[...] = in_ref[l - SUBLANES:l, :]


def _gdn_kernel(kind_ref, emit_ref, inb_ref, outb_ref,
                qkv_ref, z_ref, sm_ref, cache_ref, s0_ref, cw_ref, parr_ref, parc_ref, nw_ref,
                o_ref, sout_ref,
                tail, act, tmp, state, gcol, beta_s, grow, u_s, w_s, qk_s, qd_s, kdt_s):
    del inb_ref, outb_ref
    i = pl.program_id(0)
    l = qkv_ref.shape[0]
    nch = l // CHUNK
    h_n, dk = GDN_HEADS, GDN_DK
    gh = GDN_GROUP
    ng = h_n // gh
    gr = gh * CHUNK

    _conv_silu_tile(i, kind_ref, qkv_ref, cache_ref, cw_ref, tail, act, tmp, None)

    sm = sm_ref[...]
    beta_s[...] = jax.nn.sigmoid(sm)
    g = parr_ref[0:1, :] * _softplus(sm + parr_ref[1:2, :])
    gcol[...] = _cumsum_rows(g)
    a_t = sm.T[h_n:2 * h_n, :]
    g_t = parc_ref[0:h_n, 0:1] * _softplus(a_t + parc_ref[h_n:2 * h_n, 0:1])
    g_t = _cumsum_lanes(g_t)
    for cl in range(nch):
        for h in range(h_n):
            grow[cl, h // gh:h // gh + 1, (h % gh) * CHUNK:(h % gh + 1) * CHUNK] = (
                g_t[h:h + 1, cl * CHUNK:(cl + 1) * CHUNK])

    sout_ref[1:, :, :, :] = jnp.zeros((nch - 1,) + tuple(sout_ref.shape[1:]), F32)

    ri = lax.broadcasted_iota(I32, (gr, gr), 0)
    ci = lax.broadcasted_iota(I32, (gr, gr), 1)
    same = (ri >> CHUNK_SHIFT) == (ci >> CHUNK_SHIFT)
    causal = same & (ri >= ci)
    strict = same & (ri > ci)

    def solve(c2, carry):
        for cc in range(GDN_SOLVE_UNROLL):
            solve_chunk(c2 * GDN_SOLVE_UNROLL + cc)
        return carry

    def solve_chunk(c):
        base = pl.multiple_of(c * CHUNK, CHUNK)
        gc_blk = gcol[pl.ds(base, CHUNK), :]
        bt_blk = beta_s[pl.ds(base, CHUNK), :]
        for g in range(ng):
            qs, ks, vs, gcs, bts, gls = [], [], [], [], [], []
            for j in range(gh):
                h = g * gh + j
                q = act[pl.ds(base, CHUNK), h * dk:(h + 1) * dk]
                k = act[pl.ds(base, CHUNK), GDN_KD + h * dk:GDN_KD + (h + 1) * dk]
                qs.append(q * lax.rsqrt(jnp.sum(q * q, axis=-1, keepdims=True) + 1e-6) * (dk ** -0.5))
                ks.append(k * lax.rsqrt(jnp.sum(k * k, axis=-1, keepdims=True) + 1e-6))
                vs.append(act[pl.ds(base, CHUNK), 2 * GDN_KD + h * dk:2 * GDN_KD + (h + 1) * dk])
                gcs.append(gc_blk[:, h_n + h:h_n + h + 1])
                bts.append(bt_blk[:, h:h + 1])
                gls.append(jnp.broadcast_to(gc_blk[CHUNK - 1:CHUNK, h_n + h:h_n + h + 1], (CHUNK, 1)))
            q4 = jnp.concatenate(qs, axis=0)
            k4 = jnp.concatenate(ks, axis=0)
            v4 = jnp.concatenate(vs, axis=0)
            gc4 = jnp.concatenate(gcs, axis=0)
            bt4 = jnp.concatenate(bts, axis=0)
            gl4 = jnp.concatenate(gls, axis=0)
            eg4 = jnp.exp(gc4)
            decay = jnp.where(causal, jnp.exp(jnp.minimum(gc4 - grow[c, g:g + 1, :], 0.0)), 0.0)
            kb = k4.astype(BF16)
            xm = jnp.where(strict, -(bt4 * _dot_nt(kb, kb) * decay), 0.0)
            y = jnp.concatenate([v4 * bt4, k4 * (bt4 * eg4)], axis=1)
            p = xm
            for step in range(6):
                if step < 2:
                    y = y + _dot3(p, y)
                else:
                    y = y + _dot(p.astype(BF16), y.astype(BF16))
                if step == 0:
                    p = _dot3(p, p)
                elif step < 5:
                    pb = p.astype(BF16)
                    p = _dot(pb, pb)
            idx = c * ng + g
            u_s[idx] = y[:, :GDN_DV]
            w_s[idx] = y[:, GDN_DV:].astype(BF16)
            qk_s[idx] = (_dot_nt(q4.astype(BF16), kb) * decay).astype(BF16)
            qd_s[idx] = (q4 * eg4).astype(BF16)
            kdt_s[idx] = (k4 * jnp.exp(gl4 - gc4)).T.astype(BF16)

    lax.fori_loop(0, nch // GDN_SOLVE_UNROLL, solve, 0)

    rblk = lax.broadcasted_iota(I32, (gr, GDN_DV), 0) >> CHUNK_SHIFT

    def recur(c, carry):
        base = pl.multiple_of(c * CHUNK, CHUNK)
        gi = i * nch + c

        @pl.when(kind_ref[gi] == 1)
        def _():
            for h in range(h_n):
                state[h // gh, :, (h % gh) * GDN_DV:(h % gh + 1) * GDN_DV] = s0_ref[c, h]

        last8 = pl.multiple_of(base + CHUNK - SUBLANES, SUBLANES)
        gc_last = gcol[pl.ds(last8, SUBLANES), :][SUBLANES - 1:SUBLANES]
        for g in range(ng):
            idx = c * ng + g
            s4 = state[g]
            w4 = w_s[idx]
            qd4 = qd_s[idx]
            ws, os_ = [], []
            for j in range(gh):
                sb = s4[:, j * GDN_DV:(j + 1) * GDN_DV].astype(BF16)
                lhs = jnp.concatenate([w4[j * CHUNK:(j + 1) * CHUNK], qd4[j * CHUNK:(j + 1) * CHUNK]], axis=0)
                r = _dot(lhs, sb)
                ws.append(r[:CHUNK])
                os_.append(r[CHUNK:])
            v_new = u_s[idx] - jnp.concatenate(ws, axis=0)
            vb = v_new.astype(BF16)
            o4 = jnp.concatenate(os_, axis=0) + _dot(qk_s[idx], vb)
            vbd = jnp.concatenate([jnp.where(rblk == j, v_new, 0.0) for j in range(gh)], axis=1).astype(BF16)
            egl = jnp.concatenate(
                [jnp.broadcast_to(jnp.exp(gc_last[:, h_n + g * gh + j:h_n + g * gh + j + 1]), (1, GDN_DV))
                 for j in range(gh)], axis=1)
            state[g] = s4 * egl + _dot(kdt_s[idx], vbd)
            for j in range(gh):
                h = g * gh + j
                o = o4[j * CHUNK:(j + 1) * CHUNK]
                zz = z_ref[pl.ds(base, CHUNK), h * GDN_DV:(h + 1) * GDN_DV]
                on = o * lax.rsqrt(jnp.mean(o * o, axis=-1, keepdims=True) + EPS) * nw_ref[...] * _silu(zz)
                o_ref[pl.ds(base, CHUNK), h * GDN_DV:(h + 1) * GDN_DV] = on.astype(BF16)

        @pl.when(emit_ref[gi] >= 0)
        def _():
            for h in range(h_n):
                sout_ref[emit_ref[gi], h] = state[h // gh, :, (h % gh) * GDN_DV:(h % gh + 1) * GDN_DV]

        return carry

    lax.fori_loop(0, nch, recur, 0)


def _gdn(proj, cache_slots, s0_slots, conv_w, a_log, dt_bias, norm_w, tables, n_out_blk):
    t = proj.shape[0]
    l = MIX_TILE
    nch = TILE_CHUNKS
    kind, emit, in_blk, out_blk = tables
    ng = GDN_HEADS // GDN_GROUP
    gr = GDN_GROUP * CHUNK
    cw = jnp.concatenate([conv_w, jnp.zeros((SUBLANES - CONV_K, GDN_CONV), F32)], axis=0)
    a_neg = -jnp.exp(a_log.astype(F32))
    lane_pad = jnp.zeros((LANES - 2 * GDN_HEADS,), F32)
    parr = jnp.zeros((SUBLANES, LANES), F32)
    parr = parr.at[0].set(jnp.concatenate([jnp.zeros((GDN_HEADS,), F32), a_neg, lane_pad]))
    parr = parr.at[1].set(jnp.concatenate([jnp.zeros((GDN_HEADS,), F32), dt_bias.astype(F32), lane_pad]))
    parc = jnp.broadcast_to(jnp.concatenate([a_neg, dt_bias.astype(F32)])[:, None], (2 * GDN_HEADS, LANES))
    grid_spec = pltpu.PrefetchScalarGridSpec(
        num_scalar_prefetch=4,
        grid=(t // l,),
        in_specs=[
            pl.BlockSpec((l, GDN_CONV), lambda i, *_: (i, 0)),
            pl.BlockSpec((l, GDN_KD), lambda i, *_: (i, P_ZA // GDN_KD)),
            pl.BlockSpec((l, LANES), lambda i, *_: (i, P_SMALL // LANES)),
            pl.BlockSpec((nch, SUBLANES, GDN_CONV), lambda i, k, e, ib, ob: (ib[i], 0, 0)),
            pl.BlockSpec((nch, GDN_HEADS, GDN_DK, GDN_DV), lambda i, k, e, ib, ob: (ib[i], 0, 0, 0)),
            pl.BlockSpec((SUBLANES, GDN_CONV), lambda i, *_: (0, 0)),
            pl.BlockSpec((SUBLANES, LANES), lambda i, *_: (0, 0)),
            pl.BlockSpec((2 * GDN_HEADS, LANES), lambda i, *_: (0, 0)),
            pl.BlockSpec((1, GDN_DV), lambda i, *_: (0, 0)),
        ],
        out_specs=[
            pl.BlockSpec((l, GDN_KD), lambda i, *_: (i, 0)),
            pl.BlockSpec((nch, GDN_HEADS, GDN_DK, GDN_DV), lambda i, k, e, ib, ob: (ob[i], 0, 0, 0)),
        ],
        scratch_shapes=[
            pltpu.VMEM((SUBLANES, GDN_CONV), F32),
            pltpu.VMEM((l, GDN_CONV), F32),
            pltpu.VMEM((2 * SUBLANES, GDN_CONV), F32),
            pltpu.VMEM((ng, GDN_DK, GDN_GROUP * GDN_DV), F32),
            pltpu.VMEM((l, LANES), F32),
            pltpu.VMEM((l, LANES), F32),
            pltpu.VMEM((nch, ng, gr), F32),
            pltpu.VMEM((nch * ng, gr, GDN_DV), F32),
            pltpu.VMEM((nch * ng, gr, GDN_DK), BF16),
            pltpu.VMEM((nch * ng, gr, gr), BF16),
            pltpu.VMEM((nch * ng, gr, GDN_DK), BF16),
            pltpu.VMEM((nch * ng, GDN_DK, gr), BF16),
        ],
    )
    return pl.pallas_call(
        _gdn_kernel,
        grid_spec=grid_spec,
        out_shape=[jax.ShapeDtypeStruct((t, GDN_KD), BF16),
                   jax.ShapeDtypeStruct((n_out_blk * nch, GDN_HEADS, GDN_DK, GDN_DV), F32)],
        compiler_params=_cparams(("arbitrary",)),
        name="gdn_mixer",
    )(jnp.asarray(kind), jnp.asarray(emit), jnp.asarray(in_blk), jnp.asarray(out_blk),
      proj, proj, proj, cache_slots, s0_slots, cw, parr, parc, norm_w.reshape(1, GDN_DV).astype(F32))


SSD_PAIRS = SSD_HEADS // 2
SM_DT = 2 * GDN_HEADS


def _ssd_kernel(kind_ref, emit_ref, inb_ref, outb_ref,
                xbc_ref, z_ref, sm_ref, cache_ref, s0_ref, cw_ref, parr_ref, parc_ref, nw_ref, dsk_ref,
                o_ref, sout_ref,
                tail, act, tmp, state, cscol, dtcol, csrow):
    del inb_ref, outb_ref
    i = pl.program_id(0)
    l = xbc_ref.shape[0]
    nch = l // CHUNK
    hp = 2 * SSD_HEADDIM

    _conv_silu_tile(i, kind_ref, xbc_ref, cache_ref, cw_ref, tail, act, tmp, CONV_K)

    sm = sm_ref[...]
    dtc = _softplus(sm + parr_ref[1:2, :])
    dtcol[...] = dtc
    cscol[...] = _cumsum_rows(parr_ref[0:1, :] * dtc)
    dt_t = _softplus(sm.T[SM_DT:SM_DT + SSD_HEADS, :] + parc_ref[SSD_HEADS:2 * SSD_HEADS, 0:1])
    cs_t = _cumsum_lanes(parc_ref[0:SSD_HEADS, 0:1] * dt_t)
    left_row = (lax.broadcasted_iota(I32, (1, l), 1) & (hp - 1)) < SSD_HEADDIM
    for p in range(SSD_PAIRS):
        ra = cs_t[2 * p:2 * p + 1, :]
        rb = cs_t[2 * p + 1:2 * p + 2, :]
        even = jnp.where(left_row, ra, pltpu.roll(rb, SSD_HEADDIM, 1))
        odd = jnp.where(left_row, pltpu.roll(ra, l - SSD_HEADDIM, 1), rb)
        for cl in range(nch):
            src = even if cl % 2 == 0 else odd
            v0 = (cl // 2) * hp
            csrow[cl, p:p + 1, :] = src[:, v0:v0 + hp]

    sout_ref[1:, :, :, :] = jnp.zeros((nch - 1,) + tuple(sout_ref.shape[1:]), F32)

    ri = lax.broadcasted_iota(I32, (CHUNK, hp), 0)
    li = lax.broadcasted_iota(I32, (CHUNK, hp), 1)
    left = li < SSD_HEADDIM
    causal = ri >= (li & (SSD_HEADDIM - 1))
    top = lax.broadcasted_iota(I32, (hp, 1), 0) < SSD_HEADDIM
    zpad_b = jnp.zeros((CHUNK, SSD_STATE), BF16)
    zpad_f = jnp.zeros((CHUNK, hp), F32)

    def chunk(c, carry):
        base = pl.multiple_of(c * CHUNK, CHUNK)
        gi = i * nch + c

        @pl.when(kind_ref[gi] == 1)
        def _():
            state[...] = s0_ref[c]

        dt_blk = dtcol[pl.ds(base, CHUNK), :]
        cs_blk = cscol[pl.ds(base, CHUNK), :]
        csr = csrow[c]
        ys = []
        for g in range(SSD_NGROUPS):
            b0 = SSD_WIDTH + g * SSD_STATE
            c0 = SSD_WIDTH + SSD_NGROUPS * SSD_STATE + g * SSD_STATE
            bf = act[pl.ds(base, CHUNK), b0:b0 + SSD_STATE]
            cf = act[pl.ds(base, CHUNK), c0:c0 + SSD_STATE]
            bg = bf.astype(BF16)
            cg = cf.astype(BF16)
            cbw = _dot_nt(cg, jnp.concatenate([bg, bg], axis=0))
            for q in range(SSD_PAIRS // SSD_NGROUPS):
                p = g * (SSD_PAIRS // SSD_NGROUPS) + q
                h0 = SM_DT + 2 * p
                xp = act[pl.ds(base, CHUNK), p * hp:(p + 1) * hp]
                dtp = jnp.where(left, dt_blk[:, h0:h0 + 1], dt_blk[:, h0 + 1:h0 + 2])
                csp = jnp.where(left, cs_blk[:, h0:h0 + 1], cs_blk[:, h0 + 1:h0 + 2])
                cl0 = cs_blk[CHUNK - 1:CHUNK, h0:h0 + 1]
                cl1 = cs_blk[CHUNK - 1:CHUNK, h0 + 1:h0 + 2]
                xdt = xp * dtp
                seg = jnp.where(causal, jnp.exp(jnp.minimum(csp - csr[p:p + 1, :], 0.0)), 0.0)
                scores = (cbw * seg).astype(BF16)
                bd = jnp.concatenate([jnp.where(left, xdt, 0.0), jnp.where(left, 0.0, xdt)], axis=0)
                y = _dot(scores, bd.astype(BF16))
                sp = state[p]
                sb = sp.astype(BF16)
                xdt_t = jnp.concatenate([xdt, zpad_f], axis=0).T.astype(BF16)
                yo, st = [], []
                for hh in range(2):
                    ecs = jnp.exp(cs_blk[:, h0 + hh:h0 + hh + 1])
                    cl = cl0 if hh == 0 else cl1
                    dec = jnp.exp(cl - cs_blk[:, h0 + hh:h0 + hh + 1])
                    yo.append(_dot_nt((cf * ecs).astype(BF16), sb))
                    st.append(_dot(xdt_t, jnp.concatenate([(bf * dec).astype(BF16), zpad_b], axis=0)))
                y = y + jnp.where(left, yo[0], yo[1])
                y = y + xp * dsk_ref[:, p * hp:(p + 1) * hp]
                state[p] = sp * jnp.where(top, jnp.exp(cl0), jnp.exp(cl1)) + jnp.where(top, st[0], st[1])
                ys.append(y)
        yf = jnp.concatenate(ys, axis=1)
        yg = yf * _silu(z_ref[pl.ds(base, CHUNK), :].astype(F32))
        out = yg * lax.rsqrt(jnp.mean(yg * yg, axis=-1, keepdims=True) + EPS) * nw_ref[...]
        o_ref[pl.ds(base, CHUNK), :] = out.astype(BF16)

        @pl.when(emit_ref[gi] >= 0)
        def _():
            sout_ref[emit_ref[gi]] = state[...]

        return carry

    lax.fori_loop(0, nch, chunk, 0)


def _ssd(proj, cache_slots, s0_slots, conv_w, conv_b, a_log, dt_bias, d_skip, norm_w, tables, n_out_blk):
    t = proj.shape[0]
    l = MIX_TILE
    nch = TILE_CHUNKS
    hp = 2 * SSD_HEADDIM
    kind, emit, in_blk, out_blk = tables
    cw = jnp.concatenate([conv_w, conv_b[None, :], jnp.zeros((SUBLANES - CONV_K - 1, SSD_CONV), F32)], axis=0)
    a_neg = -jnp.exp(a_log.astype(F32))
    pre = jnp.zeros((SM_DT,), F32)
    post = jnp.zeros((LANES - SM_DT - SSD_HEADS,), F32)
    parr = jnp.zeros((SUBLANES, LANES), F32)
    parr = parr.at[0].set(jnp.concatenate([pre, a_neg, post]))
    parr = parr.at[1].set(jnp.concatenate([pre, dt_bias.astype(F32), post]))
    parc = jnp.broadcast_to(jnp.concatenate([a_neg, dt_bias.astype(F32)])[:, None], (2 * SSD_HEADS, LANES))
    dsk = jnp.repeat(d_skip.astype(F32), SSD_HEADDIM).reshape(1, SSD_WIDTH)
    grid_spec = pltpu.PrefetchScalarGridSpec(
        num_scalar_prefetch=4,
        grid=(t // l,),
        in_specs=[
            pl.BlockSpec((l, SSD_CONV), lambda i, *_: (i, P_XBC // SSD_CONV)),
            pl.BlockSpec((l, SSD_WIDTH), lambda i, *_: (i, P_ZC // SSD_WIDTH)),
            pl.BlockSpec((l, LANES), lambda i, *_: (i, P_SMALL // LANES)),
            pl.BlockSpec((nch, SUBLANES, SSD_CONV), lambda i, k, e, ib, ob: (ib[i], 0, 0)),
            pl.BlockSpec((nch, SSD_PAIRS, hp, SSD_STATE), lambda i, k, e, ib, ob: (ib[i], 0, 0, 0)),
            pl.BlockSpec((SUBLANES, SSD_CONV), lambda i, *_: (0, 0)),
            pl.BlockSpec((SUBLANES, LANES), lambda i, *_: (0, 0)),
            pl.BlockSpec((2 * SSD_HEADS, LANES), lambda i, *_: (0, 0)),
            pl.BlockSpec((1, SSD_WIDTH), lambda i, *_: (0, 0)),
            pl.BlockSpec((1, SSD_WIDTH), lambda i, *_: (0, 0)),
        ],
        out_specs=[
            pl.BlockSpec((l, SSD_WIDTH), lambda i, *_: (i, 0)),
            pl.BlockSpec((nch, SSD_PAIRS, hp, SSD_STATE), lambda i, k, e, ib, ob: (ob[i], 0, 0, 0)),
        ],
        scratch_shapes=[
            pltpu.VMEM((SUBLANES, SSD_CONV), F32),
            pltpu.VMEM((l, SSD_CONV), F32),
            pltpu.VMEM((2 * SUBLANES, SSD_CONV), F32),
            pltpu.VMEM((SSD_PAIRS, hp, SSD_STATE), F32),
            pltpu.VMEM((l, LANES), F32),
            pltpu.VMEM((l, LANES), F32),
            pltpu.VMEM((nch, SSD_PAIRS, hp), F32),
        ],
    )
    return pl.pallas_call(
        _ssd_kernel,
        grid_spec=grid_spec,
        out_shape=[jax.ShapeDtypeStruct((t, SSD_WIDTH), BF16),
                   jax.ShapeDtypeStruct((n_out_blk * nch, SSD_PAIRS, hp, SSD_STATE), F32)],
        compiler_params=_cparams(("arbitrary",)),
        name="ssd_mixer",
    )(jnp.asarray(kind), jnp.asarray(emit), jnp.asarray(in_blk), jnp.asarray(out_blk),
      proj, proj, proj, cache_slots, s0_slots, cw, parr, parc,
      norm_w.reshape(1, SSD_WIDTH).astype(F32), dsk)


S5_SB = 2
S5_LB = 512


def _s5_kernel(kind_ref, u_ref, h0r_ref, h0i_ref, perm_ref, permt_ref, bre_ref, bim_ref, cre_ref, cim_ref,
               ar_ref, ai_ref, dsk_ref, wglu_ref, bglu_ref,
               o_ref, hfr_ref, hfi_ref,
               bur, bui, pre, pim, cr, ci, inr, ini):
    i = pl.program_id(0)
    l = u_ref.shape[0]
    nch = l // CHUNK
    n = S5_N
    usb = S5_WIDTH // S5_SB
    nsb = n // S5_SB

    @pl.when(i == 0)
    def _():
        pre[0:1, :] = ar_ref[...]
        pim[0:1, :] = ai_ref[...]

        def pw(t, carry):
            pr = pre[pl.ds(t - 1, 1), :]
            pi = pim[pl.ds(t - 1, 1), :]
            pre[pl.ds(t, 1), :] = pr * ar_ref[...] - pi * ai_ref[...]
            pim[pl.ds(t, 1), :] = pr * ai_ref[...] + pi * ar_ref[...]
            return carry

        lax.fori_loop(1, CHUNK, pw, 0)
        cr[...] = jnp.zeros((1, n), F32)
        ci[...] = jnp.zeros((1, n), F32)

    up = _dot(perm_ref[...], u_ref[...].astype(BF16)).astype(BF16)
    for sb in range(S5_SB):
        us = up[:, sb * usb:(sb + 1) * usb]
        bur[:, sb * nsb:(sb + 1) * nsb] = _dot(us, bre_ref[sb])
        bui[:, sb * nsb:(sb + 1) * nsb] = _dot(us, bim_ref[sb])

    for c0 in range(0, n, S5_LB):
        a_r = jnp.broadcast_to(ar_ref[:, c0:c0 + S5_LB], (nch, S5_LB))
        a_i = jnp.broadcast_to(ai_ref[:, c0:c0 + S5_LB], (nch, S5_LB))

        def step(t, carry):
            hr, hi = carry
            r0 = pl.multiple_of(t * nch, nch)
            nr = a_r * hr - a_i * hi + bur[pl.ds(r0, nch), c0:c0 + S5_LB]
            ni = a_r * hi + a_i * hr + bui[pl.ds(r0, nch), c0:c0 + S5_LB]
            bur[pl.ds(r0, nch), c0:c0 + S5_LB] = nr
            bui[pl.ds(r0, nch), c0:c0 + S5_LB] = ni
            return nr, ni

        z = jnp.zeros((nch, S5_LB), F32)
        lax.fori_loop(0, CHUNK, step, (z, z))

    a64r = pre[CHUNK - 1:CHUNK, :]
    a64i = pim[CHUNK - 1:CHUNK, :]
    c_r = cr[...]
    c_i = ci[...]
    for s in range(nch):
        start = kind_ref[i * nch + s] == 1
        i_r = jnp.where(start, h0r_ref[s:s + 1, :], c_r)
        i_i = jnp.where(start, h0i_ref[s:s + 1, :], c_i)
        inr[s:s + 1, :] = i_r
        ini[s:s + 1, :] = i_i
        e_r = bur[l - nch + s:l - nch + s + 1, :]
        e_i = bui[l - nch + s:l - nch + s + 1, :]
        c_r = a64r * i_r - a64i * i_i + e_r
        c_i = a64r * i_i + a64i * i_r + e_i
        hfr_ref[s:s + 1, :] = c_r
        hfi_ref[s:s + 1, :] = c_i
    cr[...] = c_r
    ci[...] = c_i

    for c0 in range(0, n, S5_LB):
        n_r = inr[:, c0:c0 + S5_LB]
        n_i = ini[:, c0:c0 + S5_LB]

        def fix(t, carry):
            r0 = pl.multiple_of(t * nch, nch)
            p_r = pre[pl.ds(t, 1), c0:c0 + S5_LB]
            p_i = pim[pl.ds(t, 1), c0:c0 + S5_LB]
            bur[pl.ds(r0, nch), c0:c0 + S5_LB] += p_r * n_r - p_i * n_i
            bui[pl.ds(r0, nch), c0:c0 + S5_LB] += p_r * n_i + p_i * n_r
            return carry

        lax.fori_loop(0, CHUNK, fix, 0)

    ys = []
    for sb in range(S5_SB):
        hr = bur[:, sb * nsb:(sb + 1) * nsb].astype(BF16)
        hi = bui[:, sb * nsb:(sb + 1) * nsb].astype(BF16)
        ys.append(_dot(hr, cre_ref[sb]) - _dot(hi, cim_ref[sb]))
    yp = jnp.concatenate(ys, axis=1)
    y_hi = yp.astype(BF16)
    r1 = yp - y_hi.astype(F32)
    y_mid = r1.astype(BF16)
    y_lo = (r1 - y_mid.astype(F32)).astype(BF16)
    pt = permt_ref[...]
    y = (_dot(pt, y_hi) + _dot(pt, y_mid)) + _dot(pt, y_lo)
    y = y + u_ref[...] * dsk_ref[...]
    y = y * (0.5 * (1.0 + jnp.tanh(math.sqrt(2.0 / math.pi) * (y + 0.044715 * (y * y * y)))))
    out = y * jax.nn.sigmoid(_dot(y.astype(BF16), wglu_ref[...]) + bglu_ref[...])
    o_ref[...] = out.astype(BF16)


def _s5_tables(a_re, a_im, b_re, b_im, c_re, c_im, log_dt):
    a_re, a_im = a_re.astype(F32), a_im.astype(F32)
    dt = jnp.exp(log_dt.astype(F32))[:, None]
    mag = jnp.exp(dt * a_re)
    abar_re, abar_im = mag * jnp.cos(dt * a_im), mag * jnp.sin(dt * a_im)
    den = a_re * a_re + a_im * a_im
    num_re, num_im = abar_re - 1.0, abar_im
    zoh_re = (num_re * a_re + num_im * a_im) / den
    zoh_im = (num_im * a_re - num_re * a_im) / den
    b_re, b_im = b_re.astype(F32), b_im.astype(F32)
    bbar_re = zoh_re[..., None] * b_re - zoh_im[..., None] * b_im
    bbar_im = zoh_re[..., None] * b_im + zoh_im[..., None] * b_re
    gsb = S5_GROUPS // S5_SB
    eye = jnp.eye(gsb, dtype=F32)

    def bblk(b):
        b = b.reshape(S5_SB, gsb, S5_STATE, S5_GROUP)
        return jnp.einsum('sgpc,gh->sgchp', b, eye).reshape(S5_SB, gsb * S5_GROUP, gsb * S5_STATE).astype(BF16)

    def cblk(c):
        c = c.astype(F32).reshape(S5_SB, gsb, S5_GROUP, S5_STATE)
        return jnp.einsum('sgcp,gh->sgphc', c, eye).reshape(S5_SB, gsb * S5_STATE, gsb * S5_GROUP).astype(BF16)

    return (abar_re.reshape(1, S5_N), abar_im.reshape(1, S5_N),
            bblk(bbar_re), bblk(bbar_im), cblk(c_re), cblk(c_im))


def _s5_perm():
    r_new = np.arange(MIX_TILE)
    r_old = (r_new % TILE_CHUNKS) * CHUNK + r_new // TILE_CHUNKS
    p = np.zeros((MIX_TILE, MIX_TILE), np.float32)
    p[r_new, r_old] = 1.0
    return jnp.asarray(p, BF16), jnp.asarray(p.T, BF16)


def _s5(proj, h0r, h0i, tabs, d_skip, w_glu, b_glu, kind):
    t = proj.shape[0]
    l = MIX_TILE
    nch = TILE_CHUNKS
    abr, abi, bre, bim, cre, cim = tabs
    perm, permt = _s5_perm()
    full2 = lambda a: pl.BlockSpec(a.shape, lambda i, *_: (0, 0))
    full3 = lambda a: pl.BlockSpec(a.shape, lambda i, *_: (0, 0, 0))
    dsk = d_skip.astype(F32).reshape(1, S5_WIDTH)
    wg = w_glu.astype(BF16)
    bg = b_glu.astype(F32).reshape(1, S5_WIDTH)
    grid_spec = pltpu.PrefetchScalarGridSpec(
        num_scalar_prefetch=1,
        grid=(t // l,),
        in_specs=[
            pl.BlockSpec((l, S5_WIDTH), lambda i, *_: (i, P_UB // S5_WIDTH)),
            pl.BlockSpec((nch, S5_N), lambda i, *_: (i, 0)),
            pl.BlockSpec((nch, S5_N), lambda i, *_: (i, 0)),
            full2(perm), full2(permt), full3(bre), full3(bim), full3(cre), full3(cim),
            full2(abr), full2(abi), full2(dsk), full2(wg), full2(bg),
        ],
        out_specs=[
            pl.BlockSpec((l, S5_WIDTH), lambda i, *_: (i, 0)),
            pl.BlockSpec((nch, S5_N), lambda i, *_: (i, 0)),
            pl.BlockSpec((nch, S5_N), lambda i, *_: (i, 0)),
        ],
        scratch_shapes=[
            pltpu.VMEM((l, S5_N), F32), pltpu.VMEM((l, S5_N), F32),
            pltpu.VMEM((CHUNK, S5_N), F32), pltpu.VMEM((CHUNK, S5_N), F32),
            pltpu.VMEM((1, S5_N), F32), pltpu.VMEM((1, S5_N), F32),
            pltpu.VMEM((nch, S5_N), F32), pltpu.VMEM((nch, S5_N), F32),
        ],
    )
    nseg = t // CHUNK
    return pl.pallas_call(
        _s5_kernel,
        grid_spec=grid_spec,
        out_shape=[jax.ShapeDtypeStruct((t, S5_WIDTH), BF16),
                   jax.ShapeDtypeStruct((nseg, S5_N), F32),
                   jax.ShapeDtypeStruct((nseg, S5_N), F32)],
        compiler_params=_cparams(("arbitrary",)),
        name="s5_mixer",
    )(jnp.asarray(kind), proj, h0r, h0i, perm, permt, bre, bim, cre, cim, abr, abi, dsk, wg, bg)


MIXOUT_TM = 512
R_GRP = 0
R_EXP = SUBLANES
NEG_BIG = -1e30


def _mixout_kernel(oa_ref, ob_ref, oc_ref, w_ref, x_ref, nw_ref, wr_ref, rb_ref, x1_ref, h2_ref, lg_ref):
    acc = _dot(oa_ref[...], w_ref[0:GDN_KD, :])
    acc = acc + _dot(ob_ref[...], w_ref[GDN_KD:GDN_KD + S5_WIDTH, :])
    acc = acc + _dot(oc_ref[...], w_ref[GDN_KD + S5_WIDTH:, :])
    x1 = x_ref[...] + acc
    x1_ref[...] = x1
    h = x1 * lax.rsqrt(jnp.mean(x1 * x1, axis=-1, keepdims=True) + EPS) * nw_ref[...]
    h2_ref[...] = h
    lg_ref[...] = _dot(h.astype(BF16), wr_ref[...]) + rb_ref[...]


def _mixout(oa, ob, oc, w_out, x, nw, wr, rb):
    t, d = x.shape
    tm = MIXOUT_TM
    row = lambda w: pl.BlockSpec((tm, w), lambda i: (i, 0))
    full = lambda a: pl.BlockSpec(a.shape, lambda i: (0, 0))
    return pl.pallas_call(
        _mixout_kernel,
        grid=(t // tm,),
        in_specs=[row(GDN_KD), row(S5_WIDTH), row(SSD_WIDTH), full(w_out), row(d), full(nw), full(wr), full(rb)],
        out_specs=[row(d), row(d), row(LANES)],
        out_shape=[jax.ShapeDtypeStruct((t, d), F32), jax.ShapeDtypeStruct((t, d), F32),
                   jax.ShapeDtypeStruct((t, LANES), F32)],
        compiler_params=_cparams(("parallel",)),
        name="mix_out",
    )(oa, ob, oc, w_out, x, nw, wr, rb)


def _router_weights(rg_w, rg_b, re_w, re_b):
    d = rg_w.shape[0]
    wr = jnp.concatenate([rg_w, jnp.zeros((d, R_EXP - N_GROUPS), F32), re_w,
                          jnp.zeros((d, LANES - R_EXP - N_EXPERTS), F32)], axis=1).astype(BF16)
    rb = jnp.concatenate([rg_b.astype(F32), jnp.full((R_EXP - N_GROUPS,), NEG_BIG, F32), re_b.astype(F32),
                          jnp.zeros((LANES - R_EXP - N_EXPERTS,), F32)]).reshape(1, LANES)
    return wr, rb


ROUTE_TM = 512


def _router_kernel(lg_ref, tri_ref, idx_ref, gate_ref, cnt_ref, run):
    i = pl.program_id(0)
    tm = lg_ref.shape[0]

    @pl.when(i == 0)
    def _():
        run[...] = jnp.zeros(run.shape, F32)

    lt = lg_ref[...].T
    row8 = lax.broadcasted_iota(I32, (SUBLANES, tm), 0)
    grp = lt[R_GRP:R_GRP + SUBLANES, :]
    gm = jnp.max(grp, axis=0, keepdims=True)
    gp_top = 1.0 / jnp.sum(jnp.exp(grp - gm), axis=0, keepdims=True)
    g_top = jnp.min(jnp.where(grp == gm, row8, SUBLANES), axis=0, keepdims=True)
    ing = jnp.zeros((EPG, tm), F32)
    for g in range(N_GROUPS):
        ing = jnp.where(g_top == g, lt[R_EXP + g * EPG:R_EXP + (g + 1) * EPG, :], ing)
    em = jnp.max(ing, axis=0, keepdims=True)
    ee = jnp.exp(ing - em)
    p = ee / jnp.sum(ee, axis=0, keepdims=True)
    v1 = jnp.max(p, axis=0, keepdims=True)
    i1 = jnp.min(jnp.where(p == v1, row8, EPG), axis=0, keepdims=True)
    p2 = jnp.where(row8 == i1, -1.0, p)
    v2 = jnp.max(p2, axis=0, keepdims=True)
    i2 = jnp.min(jnp.where(p2 == v2, row8, EPG), axis=0, keepdims=True)
    den = v1 + v2
    gate1 = gp_top * v1 / den
    gate2 = gp_top * v2 / den
    e1 = g_top * EPG + i1
    e2 = g_top * EPG + i2

    erow = lax.broadcasted_iota(I32, (N_EXPERTS, tm), 0)
    hit1 = erow == e1
    hit2 = erow == e2
    oh = jnp.where(hit1 | hit2, 1.0, 0.0)
    before = _dot(oh.astype(BF16), tri_ref[...]) + run[:, 0:1]
    rank1 = jnp.sum(jnp.where(hit1, before, 0.0), axis=0, keepdims=True).astype(I32)
    rank2 = jnp.sum(jnp.where(hit2, before, 0.0), axis=0, keepdims=True).astype(I32)
    run[...] = run[...] + jnp.sum(oh, axis=1, keepdims=True)
    cnt_ref[...] = run[...].astype(I32)

    zi = jnp.zeros((SUBLANES - 4, tm), I32)
    idx_ref[...] = jnp.concatenate([e1, e2, rank1, rank2, zi], axis=0)
    r128 = lax.broadcasted_iota(I32, (LANES, tm), 0)
    gt = jnp.where(r128 == 0, gate1, jnp.where(r128 == 1, gate2, 0.0))
    gate_ref[...] = gt.T


def _router(logits):
    t = logits.shape[0]
    tm = ROUTE_TM
    tri = jnp.asarray(np.triu(np.ones((tm, tm), np.float32), 1), BF16)
    return pl.pallas_call(
        _router_kernel,
        grid=(t // tm,),
        in_specs=[pl.BlockSpec((tm, LANES), lambda i: (i, 0)),
                  pl.BlockSpec((tm, tm), lambda i: (0, 0))],
        out_specs=[pl.BlockSpec((SUBLANES, tm), lambda i: (0, i)),
                   pl.BlockSpec((tm, LANES), lambda i: (i, 0)),
                   pl.BlockSpec((N_EXPERTS, LANES), lambda i: (0, 0))],
        out_shape=[jax.ShapeDtypeStruct((SUBLANES, t), I32),
                   jax.ShapeDtypeStruct((t, LANES), F32),
                   jax.ShapeDtypeStruct((N_EXPERTS, LANES), I32)],
        scratch_shapes=[pltpu.VMEM((N_EXPERTS, LANES), F32)],
        compiler_params=_cparams(("arbitrary",)),
        name="router",
    )(logits, tri)


MOE_BLOCK = 256
DISPATCH_TM = 512
COMBINE_TM = 256


def _dispatch_kernel(pstart_ref, idx_ref, h2_ref, xs_in_ref, xs_ref, sem):
    del xs_in_ref
    i = pl.program_id(0)
    tm = idx_ref.shape[1]

    def copy(t, k):
        slot = pstart_ref[idx_ref[k, t]] + idx_ref[2 + k, t]
        return pltpu.make_async_copy(h2_ref.at[pl.ds(t, 1)], xs_ref.at[pl.ds(slot, 1)], sem)

    def issue(t, carry):
        copy(t, 0).start()
        copy(t, 1).start()
        return carry

    def drain(t, carry):
        copy(t, 0).wait()
        copy(t, 1).wait()
        return carry

    lax.fori_loop(0, tm, issue, 0)
    lax.fori_loop(0, tm, drain, 0)


def _dispatch(pad_start, idx, h2, n_slots):
    t, d = h2.shape
    tm = DISPATCH_TM
    xs0 = jnp.zeros((n_slots, d), F32)
    grid_spec = pltpu.PrefetchScalarGridSpec(
        num_scalar_prefetch=1,
        grid=(t // tm,),
        in_specs=[pl.BlockSpec((SUBLANES, tm), lambda i, *_: (0, i), memory_space=pltpu.SMEM),
                  pl.BlockSpec((tm, d), lambda i, *_: (i, 0)),
                  pl.BlockSpec(memory_space=pl.ANY)],
        out_specs=pl.BlockSpec(memory_space=pl.ANY),
        scratch_shapes=[pltpu.SemaphoreType.DMA(())],
    )
    return pl.pallas_call(
        _dispatch_kernel,
        grid_spec=grid_spec,
        out_shape=jax.ShapeDtypeStruct((n_slots, d), F32),
        input_output_aliases={3: 0},
        compiler_params=_cparams(("arbitrary",)),
        name="moe_dispatch",
    )(pad_start, idx, h2, xs0)


def _expert_kernel(be_ref, nv_ref, xs_ref, wg_ref, wu_ref, wd_ref, ys_ref):
    b = pl.program_id(0)

    @pl.when(b < nv_ref[0])
    def _():
        x = xs_ref[...].astype(BF16)
        g = _dot(x, wg_ref[0])
        u = _dot(x, wu_ref[0])
        h = (_silu(g) * u).astype(BF16)
        ys_ref[...] = _dot(h, wd_ref[0])

    @pl.when(b >= nv_ref[0])
    def _():
        ys_ref[...] = jnp.zeros(ys_ref.shape, F32)


def _experts(block_expert, n_valid, xs, wg, wu, wd):
    n_slots, d = xs.shape
    de = wg.shape[2]
    nb = n_slots // MOE_BLOCK
    blk = lambda b, be, nv: (jnp.minimum(b, nv[0] - 1), 0)
    wsel = lambda b, be, nv: (be[jnp.minimum(b, nv[0] - 1)], 0, 0)
    grid_spec = pltpu.PrefetchScalarGridSpec(
        num_scalar_prefetch=2,
        grid=(nb,),
        in_specs=[pl.BlockSpec((MOE_BLOCK, d), blk),
                  pl.BlockSpec((1, d, de), wsel),
                  pl.BlockSpec((1, d, de), wsel),
                  pl.BlockSpec((1, de, d), wsel)],
        out_specs=pl.BlockSpec((MOE_BLOCK, d), lambda b, be, nv: (b, 0)),
    )
    return pl.pallas_call(
        _expert_kernel,
        grid_spec=grid_spec,
        out_shape=jax.ShapeDtypeStruct((n_slots, d), F32),
        compiler_params=_cparams(("arbitrary",)),
        name="moe_experts",
    )(block_expert, n_valid, xs, wg, wu, wd)


def _combine_kernel(pstart_ref, idx_ref, x1_ref, gate_ref, ys_ref, nw_ref, out_ref, ybuf, sem, *, final):
    tm = x1_ref.shape[0]

    def copy(t, k):
        slot = pstart_ref[idx_ref[k, t]] + idx_ref[2 + k, t]
        return pltpu.make_async_copy(ys_ref.at[pl.ds(slot, 1)], ybuf.at[k, pl.ds(t, 1)], sem)

    def issue(t, carry):
        copy(t, 0).start()
        copy(t, 1).start()
        return carry

    def drain(t, carry):
        copy(t, 0).wait()
        copy(t, 1).wait()
        return carry

    lax.fori_loop(0, tm, issue, 0)
    lax.fori_loop(0, tm, drain, 0)
    g = gate_ref[...]
    y = ybuf[0] * g[:, 0:1] + ybuf[1] * g[:, 1:2]
    x2 = x1_ref[...] + y
    if final:
        x2 = x2 * lax.rsqrt(jnp.mean(x2 * x2, axis=-1, keepdims=True) + EPS) * nw_ref[...]
    out_ref[...] = x2


def _combine(pad_start, idx, x1, gates, ys, nw, final):
    t, d = x1.shape
    tm = COMBINE_TM
    grid_spec = pltpu.PrefetchScalarGridSpec(
        num_scalar_prefetch=1,
        grid=(t // tm,),
        in_specs=[pl.BlockSpec((SUBLANES, tm), lambda i, *_: (0, i), memory_space=pltpu.SMEM),
                  pl.BlockSpec((tm, d), lambda i, *_: (i, 0)),
                  pl.BlockSpec((tm, LANES), lambda i, *_: (i, 0)),
                  pl.BlockSpec(memory_space=pl.ANY),
                  pl.BlockSpec((1, d), lambda i, *_: (0, 0))],
        out_specs=pl.BlockSpec((tm, d), lambda i, *_: (i, 0)),
        scratch_shapes=[pltpu.VMEM((2, tm, d), F32), pltpu.SemaphoreType.DMA(())],
    )
    return pl.pallas_call(
        functools.partial(_combine_kernel, final=final),
        grid_spec=grid_spec,
        out_shape=jax.ShapeDtypeStruct((t, d), F32),
        compiler_params=_cparams(("arbitrary",)),
        name="moe_combine",
    )(pad_start, idx, x1, gates, ys, nw)


def _moe(x1, h2, logits, wg, wu, wd, norm_final, final):
    t, d = x1.shape
    idx, gates, cnt = _router(logits)
    counts = cnt[:, 0]
    padded = (counts + MOE_BLOCK - 1) // MOE_BLOCK * MOE_BLOCK
    pad_end = jnp.cumsum(padded)
    pad_start = (pad_end - padded).astype(I32)
    nb = (2 * t + N_EXPERTS * (MOE_BLOCK - 1) + MOE_BLOCK - 1) // MOE_BLOCK
    n_valid = (pad_end[-1] // MOE_BLOCK).astype(I32).reshape(1)
    block_expert = jnp.minimum(
        jnp.searchsorted(pad_end, jnp.arange(nb, dtype=I32) * MOE_BLOCK, side='right'),
        N_EXPERTS - 1).astype(I32)
    xs = _dispatch(pad_start, idx, h2, nb * MOE_BLOCK)
    ys = _experts(block_expert, n_valid, xs, wg, wu, wd)
    return _combine(pad_start, idx, x1, gates, ys, norm_final, final)


def _stream_ends(nbp, seq, nbs, dseq):
    ends = [(b + 1) * seq for b in range(nbp)] + [nbp * seq + (s + 1) * dseq for s in range(nbs)]
    return np.asarray(ends)


def kernel(x_prompt, x_sample, cache_conv_gdn, state_gdn, state_s5, cache_conv_ssd, state_ssd, norm_mix, w_in, gdn_conv_w, gdn_a_log, gdn_dt_bias, gdn_norm, s5_a_re, s5_a_im, s5_b_re, s5_b_im, s5_c_re, s5_c_im, s5_log_dt, s5_d, s5_w_glu, s5_b_glu, ssd_conv_w, ssd_conv_b, ssd_a_log, ssd_dt_bias, ssd_d, ssd_norm, w_out, norm_ffn, router_group_w, router_group_b, router_expert_w, router_expert_b, expert_w_gate, expert_w_up, expert_w_down, norm_final):
    nbp, seq, d = x_prompt.shape
    nbs, dseq, _ = x_sample.shape
    depth = w_in.shape[0]
    tp = nbp * seq
    t = tp + nbs * dseq
    x = jnp.concatenate([x_prompt.reshape(tp, d), x_sample.reshape(nbs * dseq, d)], axis=0)

    tables = _chunk_tables(nbp, seq, nbs, dseq)
    kind = tables[0]
    n_out_blk = nbp + nbs // TILE_CHUNKS
    ends = _stream_ends(nbp, seq, nbs, dseq)
    tail_rows = (ends[:, None] + np.arange(-(CONV_K - 1), 0)[None, :]).reshape(-1)
    end_seg = ends // CHUNK - 1
    ncp = tp // CHUNK
    state_rows = np.concatenate([np.arange(nbp) * TILE_CHUNKS, nbp * TILE_CHUNKS + np.arange(nbs)])

    new_conv_gdn, new_gdn, new_s5, new_conv_ssd, new_ssd = [], [], [], [], []
    for l in range(depth):
        proj = _proj(x, norm_mix[l].reshape(1, d).astype(F32), _rearrange_w_in(w_in[l]))

        oa, sg = _gdn(proj, _conv_cache_slots(cache_conv_gdn[l].astype(F32)),
                      _init_slots(state_gdn[l].astype(F32)), gdn_conv_w[l].astype(F32), gdn_a_log[l],
                      gdn_dt_bias[l], gdn_norm[l], tables, n_out_blk)

        h0 = state_s5[l].astype(F32).reshape(nbs, S5_N, 2)
        zeros_p = jnp.zeros((ncp, S5_N), F32)
        ob, hfr, hfi = _s5(proj, jnp.concatenate([zeros_p, h0[..., 0]], axis=0),
                           jnp.concatenate([zeros_p, h0[..., 1]], axis=0),
                           _s5_tables(s5_a_re[l], s5_a_im[l], s5_b_re[l], s5_b_im[l], s5_c_re[l], s5_c_im[l],
                                      s5_log_dt[l]),
                           s5_d[l], s5_w_glu[l], s5_b_glu[l], kind)

        oc, ss = _ssd(proj, _conv_cache_slots(cache_conv_ssd[l].astype(F32)),
                      _init_slots(state_ssd[l].astype(F32).reshape(nbs, SSD_PAIRS, 2 * SSD_HEADDIM, SSD_STATE)),
                      ssd_conv_w[l].astype(F32), ssd_conv_b[l].astype(F32), ssd_a_log[l], ssd_dt_bias[l],
                      ssd_d[l], ssd_norm[l], tables, n_out_blk)

        wr, rb = _router_weights(router_group_w[l].astype(F32), router_group_b[l],
                                 router_expert_w[l].astype(F32), router_expert_b[l])
        x1, h2, logits = _mixout(oa, ob, oc, w_out[l].astype(BF16), x,
                                 norm_ffn[l].reshape(1, d).astype(F32), wr, rb)
        x = _moe(x1, h2, logits, expert_w_gate[l].astype(BF16), expert_w_up[l].astype(BF16),
                 expert_w_down[l].astype(BF16), norm_final.reshape(1, d).astype(F32), l == depth - 1)

        tails = proj[tail_rows]
        new_conv_gdn.append(tails[:, P_QKV:P_QKV + GDN_CONV].reshape(nbp + nbs, CONV_K - 1, GDN_CONV))
        new_conv_ssd.append(tails[:, P_XBC:P_XBC + SSD_CONV].reshape(nbp + nbs, CONV_K - 1, SSD_CONV))
        new_gdn.append(sg[state_rows])
        new_ssd.append(ss[state_rows].reshape(nbp + nbs, SSD_HEADS, SSD_HEADDIM, SSD_STATE))
        new_s5.append(jnp.stack([hfr[end_seg], hfi[end_seg]], axis=-1)
                      .reshape(nbp + nbs, S5_GROUPS, S5_STATE, 2))

    def split(parts):
        a = jnp.stack(parts)
        return a[:, :nbp], a[:, nbp:]

    cg_p, cg_s = split(new_conv_gdn)
    sg_p, sg_s = split(new_gdn)
    s5_p, s5_s = split(new_s5)
    cs_p, cs_s = split(new_conv_ssd)
    ss_p, ss_s = split(new_ssd)
    y_prompt = x[:tp].reshape(nbp, seq, d)
    y_sample = x[tp:].reshape(nbs, dseq, d)
    return (y_prompt, y_sample, cg_p, sg_p, s5_p, cs_p, ss_p, cg_s, sg_s, s5_s, cs_s, ss_s)
```

```python
import functools
import math

import numpy as np
import jax
import jax.numpy as jnp
from jax import lax
from jax.experimental import pallas as pl
from jax.experimental.pallas import tpu as pltpu

F32 = jnp.float32
BF16 = jnp.bfloat16
I32 = jnp.int32

EPS = 1e-6
CHUNK = 64
CHUNK_SHIFT = 6
CONV_K = 4
LANES = 128
SUBLANES = 8
VMEM_LIMIT = 56 * 1024 * 1024

GDN_HEADS = 8
GDN_DK = 128
GDN_DV = 128
GDN_KD = GDN_HEADS * GDN_DK
GDN_CONV = 3 * GDN_KD
GDN_GROUP = 4
S5_WIDTH = 512
S5_GROUPS = 32
S5_GROUP = 16
S5_STATE = 64
S5_N = S5_GROUPS * S5_STATE
SSD_WIDTH = 512
SSD_HEADS = 8
SSD_HEADDIM = 64
SSD_NGROUPS = 2
SSD_STATE = 128
SSD_CONV = SSD_WIDTH + 2 * SSD_NGROUPS * SSD_STATE
N_GROUPS = 4
EPG = 8
N_EXPERTS = 32

MIX_TILE = 512
TILE_CHUNKS = MIX_TILE // CHUNK


def _cparams(sem, vmem=VMEM_LIMIT):
    return pltpu.CompilerParams(dimension_semantics=sem, vmem_limit_bytes=vmem)


def _silu(x):
    return x * jax.nn.sigmoid(x)


def _softplus(x):
    return jnp.maximum(x, 0.0) + jnp.log1p(jnp.exp(-jnp.abs(x)))


def _dot(a, b):
    return jnp.dot(a, b, preferred_element_type=F32)


def _dot3(a, b):
    ah = a.astype(BF16)
    bh = b.astype(BF16)
    al = (a - ah.astype(F32)).astype(BF16)
    bl = (b - bh.astype(F32)).astype(BF16)
    return _dot(ah, bh) + (_dot(ah, bl) + _dot(al, bh))


def _dot_nt(a, b):
    return lax.dot_general(a, b, (((1,), (1,)), ((), ())), preferred_element_type=F32)


def _dot_tn(a, b):
    return lax.dot_general(a, b, (((0,), (0,)), ((), ())), preferred_element_type=F32)


def _cumsum_rows(x):
    row = lax.broadcasted_iota(I32, x.shape, 0) & (CHUNK - 1)
    k = 1
    while k < CHUNK:
        x = x + jnp.where(row >= k, pltpu.roll(x, k, 0), 0.0)
        k *= 2
    return x


def _cumsum_lanes(x):
    lane = lax.broadcasted_iota(I32, x.shape, 1) & (CHUNK - 1)
    k = 1
    while k < CHUNK:
        x = x + jnp.where(lane >= k, pltpu.roll(x, k, 1), 0.0)
        k *= 2
    return x


PROJ_TM = 1024
PROJ_TN = 1280
P_QKV, P_ZA, P_UB, P_ZC, P_XBC = 0, 3072, 4096, 4608, 5120
P_MAIN = 6144
P_TOTAL = 6400
P_SMALL = P_TOTAL - LANES


def _proj_kernel(x_ref, nw_ref, w_ref, o_ref, h_scr):
    @pl.when(pl.program_id(1) == 0)
    def _():
        x = x_ref[...]
        ms = jnp.mean(x * x, axis=-1, keepdims=True)
        h_scr[...] = (x * lax.rsqrt(ms + EPS) * nw_ref[...]).astype(BF16)

    o_ref[...] = _dot(h_scr[...], w_ref[...])


def _proj(x, nw, w):
    t, d = x.shape
    n = w.shape[1]
    tm = min(PROJ_TM, t)
    assert t % tm == 0 and n % PROJ_TN == 0
    return pl.pallas_call(
        _proj_kernel,
        grid=(t // tm, n // PROJ_TN),
        in_specs=[pl.BlockSpec((tm, d), lambda i, j: (i, 0)),
                  pl.BlockSpec((1, d), lambda i, j: (0, 0)),
                  pl.BlockSpec((d, PROJ_TN), lambda i, j: (0, j))],
        out_specs=pl.BlockSpec((tm, PROJ_TN), lambda i, j: (i, j)),
        out_shape=jax.ShapeDtypeStruct((t, n), F32),
        scratch_shapes=[pltpu.VMEM((tm, d), BF16)],
        compiler_params=_cparams(("parallel", "arbitrary")),
        name="proj_in",
    )(x, nw, w)


def _rearrange_w_in(w_in):
    d = w_in.shape[0]
    off_za = GDN_CONV
    off_ba = off_za + GDN_KD
    off_s5 = off_ba + 2 * GDN_HEADS
    off_zc = off_s5 + S5_WIDTH
    off_xbc = off_zc + SSD_WIDTH
    off_dt = off_xbc + SSD_CONV
    small = jnp.concatenate([w_in[:, off_ba:off_s5], w_in[:, off_dt:],
                             jnp.zeros((d, LANES - 3 * GDN_HEADS), w_in.dtype)], axis=1)
    w = jnp.concatenate([w_in[:, :off_ba], w_in[:, off_s5:off_dt],
                         jnp.zeros((d, P_TOTAL - P_MAIN - LANES), w_in.dtype), small], axis=1)
    return w.astype(BF16)


def _chunk_tables(nbp, seq, nbs, dseq):
    assert seq % MIX_TILE == 0 and dseq == CHUNK and (nbs * dseq) % MIX_TILE == 0
    cps = seq // CHUNK
    ncp = nbp * cps
    nc = ncp + nbs
    kind = np.zeros((nc,), np.int32)
    emit = np.full((nc,), -1, np.int32)
    for c in range(nc):
        if c < ncp:
            kind[c] = 1 if c % cps == 0 else 0
            if c % TILE_CHUNKS == TILE_CHUNKS - 1:
                emit[c] = 0
        else:
            kind[c] = 1
            emit[c] = (c - ncp) % TILE_CHUNKS
    ntp = ncp // TILE_CHUNKS
    nts = nbs // TILE_CHUNKS
    in_blk = np.concatenate([np.zeros((ntp,), np.int32), 1 + np.arange(nts, dtype=np.int32)])
    out_blk = np.concatenate([np.arange(ntp, dtype=np.int32) // (cps // TILE_CHUNKS),
                              nbp + np.arange(nts, dtype=np.int32)])
    return kind, emit, in_blk, out_blk


def _init_slots(x):
    return jnp.concatenate([jnp.zeros((TILE_CHUNKS,) + x.shape[1:], x.dtype), x], axis=0)


def _conv_cache_slots(cache):
    nbs, k1, c = cache.shape
    padded = jnp.concatenate([jnp.zeros((nbs, SUBLANES - k1, c), cache.dtype), cache], axis=1)
    return _init_slots(padded)


def _conv_silu_tile(i, kind_ref, in_ref, cache_ref, cw_ref, tail, act, tmp, bias_row):
    l, c = in_ref.shape
    cb = 512
    k1 = CONV_K - 1
    nch = l // CHUNK

    @pl.when(i == 0)
    def _():
        tail[...] = jnp.zeros(tail.shape, F32)

    def taps(src, lo, hi, c0):
        acc = src[lo - k1:hi - k1, c0:c0 + cb] * cw_ref[0:1, c0:c0 + cb]
        for j in range(1, CONV_K):
            acc = acc + src[lo - k1 + j:hi - k1 + j, c0:c0 + cb] * cw_ref[j:j + 1, c0:c0 + cb]
        if bias_row is not None:
            acc = acc + cw_ref[bias_row:bias_row + 1, c0:c0 + cb]
        return _silu(acc)

    def head_rows(prev, r0):
        tmp[0:SUBLANES, :] = prev
        tmp[SUBLANES:2 * SUBLANES, :] = in_ref[r0:r0 + SUBLANES, :]
        for c0 in range(0, c, cb):
            act[r0:r0 + SUBLANES, c0:c0 + cb] = taps(tmp, SUBLANES, 2 * SUBLANES, c0)

    for rb in range(nch):
        lo = SUBLANES if rb == 0 else rb * CHUNK
        for c0 in range(0, c, cb):
            act[lo:(rb + 1) * CHUNK, c0:c0 + cb] = taps(in_ref, lo, (rb + 1) * CHUNK, c0)
    head_rows(tail[...], 0)

    for cl in range(nch):
        @pl.when(kind_ref[i * nch + cl] == 1)
        def _():
            head_rows(cache_ref[cl], cl * CHUNK)

    tail[...] = in_ref[l - SUBLANES:l, :]


def _gdn_kernel(kind_ref, emit_ref, inb_ref, outb_ref,
                qkv_ref, z_ref, sm_ref, cache_ref, s0_ref, cw_ref, parr_ref, parc_ref, nw_ref,
                o_ref, sout_ref,
                tail, act, tmp, state, gcol, beta_s, grow, u_s, w_s, qk_s, qd_s, kdt_s):
    del inb_ref, outb_ref
    i = pl.program_id(0)
    l = qkv_ref.shape[0]
    nch = l // CHUNK
    h_n, dk = GDN_HEADS, GDN_DK
    gh = GDN_GROUP
    ng = h_n // gh
    gr = gh * CHUNK

    _conv_silu_tile(i, kind_ref, qkv_ref, cache_ref, cw_ref, tail, act, tmp, None)

    sm = sm_ref[...]
    beta_s[...] = jax.nn.sigmoid(sm)
    g = parr_ref[0:1, :] * _softplus(sm + parr_ref[1:2, :])
    gcol[...] = _cumsum_rows(g)
    a_t = sm.T[h_n:2 * h_n, :]
    g_t = parc_ref[0:h_n, 0:1] * _softplus(a_t + parc_ref[h_n:2 * h_n, 0:1])
    g_t = _cumsum_lanes(g_t)
    for cl in range(nch):
        for h in range(h_n):
            grow[cl, h // gh:h // gh + 1, (h % gh) * CHUNK:(h % gh + 1) * CHUNK] = (
                g_t[h:h + 1, cl * CHUNK:(cl + 1) * CHUNK])

    sout_ref[1:, :, :, :] = jnp.zeros((nch - 1,) + tuple(sout_ref.shape[1:]), F32)

    ri = lax.broadcasted_iota(I32, (gr, gr), 0)
    ci = lax.broadcasted_iota(I32, (gr, gr), 1)
    same = (ri >> CHUNK_SHIFT) == (ci >> CHUNK_SHIFT)
    causal = same & (ri >= ci)
    strict = same & (ri > ci)

    def row0(c):
        return c * CHUNK if isinstance(c, int) else pl.multiple_of(c * CHUNK, CHUNK)

    def solve_chunk(c):
        base = row0(c)
        gc_blk = gcol[pl.ds(base, CHUNK), :]
        bt_blk = beta_s[pl.ds(base, CHUNK), :]
        for g in range(ng):
            qs, ks, vs, gcs, bts, gls = [], [], [], [], [], []
            for j in range(gh):
                h = g * gh + j
                q = act[pl.ds(base, CHUNK), h * dk:(h + 1) * dk]
                k = act[pl.ds(base, CHUNK), GDN_KD + h * dk:GDN_KD + (h + 1) * dk]
                qs.append(q * lax.rsqrt(jnp.sum(q * q, axis=-1, keepdims=True) + 1e-6) * (dk ** -0.5))
                ks.append(k * lax.rsqrt(jnp.sum(k * k, axis=-1, keepdims=True) + 1e-6))
                vs.append(act[pl.ds(base, CHUNK), 2 * GDN_KD + h * dk:2 * GDN_KD + (h + 1) * dk])
                gcs.append(gc_blk[:, h_n + h:h_n + h + 1])
                bts.append(bt_blk[:, h:h + 1])
                gls.append(jnp.broadcast_to(gc_blk[CHUNK - 1:CHUNK, h_n + h:h_n + h + 1], (CHUNK, 1)))
            q4 = jnp.concatenate(qs, axis=0)
            k4 = jnp.concatenate(ks, axis=0)
            v4 = jnp.concatenate(vs, axis=0)
            gc4 = jnp.concatenate(gcs, axis=0)
            bt4 = jnp.concatenate(bts, axis=0)
            gl4 = jnp.concatenate(gls, axis=0)
            eg4 = jnp.exp(gc4)
            decay = jnp.where(causal, jnp.exp(jnp.minimum(gc4 - grow[c, g:g + 1, :], 0.0)), 0.0)
            kb = k4.astype(BF16)
            xm = jnp.where(strict, -(bt4 * _dot_nt(kb, kb) * decay), 0.0)
            y = jnp.concatenate([v4 * bt4, k4 * (bt4 * eg4)], axis=1)
            p = xm
            for step in range(6):
                if step < 2:
                    y = y + _dot3(p, y)
                else:
                    y = y + _dot(p.astype(BF16), y.astype(BF16))
                if step == 0:
                    p = _dot3(p, p)
                elif step < 5:
                    pb = p.astype(BF16)
                    p = _dot(pb, pb)
            idx = c * ng + g
            u_s[idx] = y[:, :GDN_DV]
            w_s[idx] = y[:, GDN_DV:].astype(BF16)
            qk_s[idx] = (_dot_nt(q4.astype(BF16), kb) * decay).astype(BF16)
            qd_s[idx] = (q4 * eg4).astype(BF16)
            kdt_s[idx] = (k4 * jnp.exp(gl4 - gc4)).T.astype(BF16)

    rblk = lax.broadcasted_iota(I32, (gr, GDN_DV), 0) >> CHUNK_SHIFT

    def recur_chunk(c, solve_next):
        base = row0(c)
        gi = i * nch + c

        @pl.when(kind_ref[gi] == 1)
        def _():
            for h in range(h_n):
                state[h // gh, :, (h % gh) * GDN_DV:(h % gh + 1) * GDN_DV] = s0_ref[c, h]

        last8 = base + CHUNK - SUBLANES
        if not isinstance(c, int):
            last8 = pl.multiple_of(last8, SUBLANES)
        gc_last = gcol[pl.ds(last8, SUBLANES), :][SUBLANES - 1:SUBLANES]
        for g in range(ng):
            idx = c * ng + g
            s4 = state[g]
            w4 = w_s[idx]
            qd4 = qd_s[idx]
            ws, os_ = [], []
            for j in range(gh):
                sb = s4[:, j * GDN_DV:(j + 1) * GDN_DV].astype(BF16)
                lhs = jnp.concatenate([w4[j * CHUNK:(j + 1) * CHUNK], qd4[j * CHUNK:(j + 1) * CHUNK]], axis=0)
                r = _dot(lhs, sb)
                ws.append(r[:CHUNK])
                os_.append(r[CHUNK:])
            v_new = u_s[idx] - jnp.concatenate(ws, axis=0)
            vb = v_new.astype(BF16)
            o4 = jnp.concatenate(os_, axis=0) + _dot(qk_s[idx], vb)
            vbd = jnp.concatenate([jnp.where(rblk == j, v_new, 0.0) for j in range(gh)], axis=1).astype(BF16)
            egl = jnp.concatenate(
                [jnp.broadcast_to(jnp.exp(gc_last[:, h_n + g * gh + j:h_n + g * gh + j + 1]), (1, GDN_DV))
                 for j in range(gh)], axis=1)
            state[g] = s4 * egl + _dot(kdt_s[idx], vbd)
            for j in range(gh):
                h = g * gh + j
                o = o4[j * CHUNK:(j + 1) * CHUNK]
                zz = z_ref[pl.ds(base, CHUNK), h * GDN_DV:(h + 1) * GDN_DV]
                on = o * lax.rsqrt(jnp.mean(o * o, axis=-1, keepdims=True) + EPS) * nw_ref[...] * _silu(zz)
                o_ref[pl.ds(base, CHUNK), h * GDN_DV:(h + 1) * GDN_DV] = on.astype(BF16)

        if solve_next:
            solve_chunk(c + 1)

        @pl.when(emit_ref[gi] >= 0)
        def _():
            for h in range(h_n):
                sout_ref[emit_ref[gi], h] = state[h // gh, :, (h % gh) * GDN_DV:(h % gh + 1) * GDN_DV]

    solve_chunk(0)
    lax.fori_loop(0, nch - 1, lambda c, carry: (recur_chunk(c, True), carry)[1], 0)
    recur_chunk(nch - 1, False)


def _gdn(proj, cache_slots, s0_slots, conv_w, a_log, dt_bias, norm_w, tables, n_out_blk):
    t = proj.shape[0]
    l = MIX_TILE
    nch = TILE_CHUNKS
    kind, emit, in_blk, out_blk = tables
    ng = GDN_HEADS // GDN_GROUP
    gr = GDN_GROUP * CHUNK
    cw = jnp.concatenate([conv_w, jnp.zeros((SUBLANES - CONV_K, GDN_CONV), F32)], axis=0)
    a_neg = -jnp.exp(a_log.astype(F32))
    lane_pad = jnp.zeros((LANES - 2 * GDN_HEADS,), F32)
    parr = jnp.zeros((SUBLANES, LANES), F32)
    parr = parr.at[0].set(jnp.concatenate([jnp.zeros((GDN_HEADS,), F32), a_neg, lane_pad]))
    parr = parr.at[1].set(jnp.concatenate([jnp.zeros((GDN_HEADS,), F32), dt_bias.astype(F32), lane_pad]))
    parc = jnp.broadcast_to(jnp.concatenate([a_neg, dt_bias.astype(F32)])[:, None], (2 * GDN_HEADS, LANES))
    grid_spec = pltpu.PrefetchScalarGridSpec(
        num_scalar_prefetch=4,
        grid=(t // l,),
        in_specs=[
            pl.BlockSpec((l, GDN_CONV), lambda i, *_: (i, 0)),
            pl.BlockSpec((l, GDN_KD), lambda i, *_: (i, P_ZA // GDN_KD)),
            pl.BlockSpec((l, LANES), lambda i, *_: (i, P_SMALL // LANES)),
            pl.BlockSpec((nch, SUBLANES, GDN_CONV), lambda i, k, e, ib, ob: (ib[i], 0, 0)),
            pl.BlockSpec((nch, GDN_HEADS, GDN_DK, GDN_DV), lambda i, k, e, ib, ob: (ib[i], 0, 0, 0)),
            pl.BlockSpec((SUBLANES, GDN_CONV), lambda i, *_: (0, 0)),
            pl.BlockSpec((SUBLANES, LANES), lambda i, *_: (0, 0)),
            pl.BlockSpec((2 * GDN_HEADS, LANES), lambda i, *_: (0, 0)),
            pl.BlockSpec((1, GDN_DV), lambda i, *_: (0, 0)),
        ],
        out_specs=[
            pl.BlockSpec((l, GDN_KD), lambda i, *_: (i, 0)),
            pl.BlockSpec((nch, GDN_HEADS, GDN_DK, GDN_DV), lambda i, k, e, ib, ob: (ob[i], 0, 0, 0)),
        ],
        scratch_shapes=[
            pltpu.VMEM((SUBLANES, GDN_CONV), F32),
            pltpu.VMEM((l, GDN_CONV), F32),
            pltpu.VMEM((2 * SUBLANES, GDN_CONV), F32),
            pltpu.VMEM((ng, GDN_DK, GDN_GROUP * GDN_DV), F32),
            pltpu.VMEM((l, LANES), F32),
            pltpu.VMEM((l, LANES), F32),
            pltpu.VMEM((nch, ng, gr), F32),
            pltpu.VMEM((nch * ng, gr, GDN_DV), F32),
            pltpu.VMEM((nch * ng, gr, GDN_DK), BF16),
            pltpu.VMEM((nch * ng, gr, gr), BF16),
            pltpu.VMEM((nch * ng, gr, GDN_DK), BF16),
            pltpu.VMEM((nch * ng, GDN_DK, gr), BF16),
        ],
    )
    return pl.pallas_call(
        _gdn_kernel,
        grid_spec=grid_spec,
        out_shape=[jax.ShapeDtypeStruct((t, GDN_KD), BF16),
                   jax.ShapeDtypeStruct((n_out_blk * nch, GDN_HEADS, GDN_DK, GDN_DV), F32)],
        compiler_params=_cparams(("arbitrary",)),
        name="gdn_mixer",
    )(jnp.asarray(kind), jnp.asarray(emit), jnp.asarray(in_blk), jnp.asarray(out_blk),
      proj, proj, proj, cache_slots, s0_slots, cw, parr, parc, norm_w.reshape(1, GDN_DV).astype(F32))


SSD_PAIRS = SSD_HEADS // 2
SM_DT = 2 * GDN_HEADS


def _ssd_kernel(kind_ref, emit_ref, inb_ref, outb_ref,
                xbc_ref, z_ref, sm_ref, cache_ref, s0_ref, cw_ref, parr_ref, parc_ref, nw_ref, dsk_ref,
                o_ref, sout_ref,
                tail, act, tmp, state, cscol, dtcol, csrow):
    del inb_ref, outb_ref
    i = pl.program_id(0)
    l = xbc_ref.shape[0]
    nch = l // CHUNK
    hp = 2 * SSD_HEADDIM

    _conv_silu_tile(i, kind_ref, xbc_ref, cache_ref, cw_ref, tail, act, tmp, CONV_K)

    sm = sm_ref[...]
    dtc = _softplus(sm + parr_ref[1:2, :])
    dtcol[...] = dtc
    cscol[...] = _cumsum_rows(parr_ref[0:1, :] * dtc)
    dt_t = _softplus(sm.T[SM_DT:SM_DT + SSD_HEADS, :] + parc_ref[SSD_HEADS:2 * SSD_HEADS, 0:1])
    cs_t = _cumsum_lanes(parc_ref[0:SSD_HEADS, 0:1] * dt_t)
    left_row = (lax.broadcasted_iota(I32, (1, l), 1) & (hp - 1)) < SSD_HEADDIM
    for p in range(SSD_PAIRS):
        ra = cs_t[2 * p:2 * p + 1, :]
        rb = cs_t[2 * p + 1:2 * p + 2, :]
        even = jnp.where(left_row, ra, pltpu.roll(rb, SSD_HEADDIM, 1))
        odd = jnp.where(left_row, pltpu.roll(ra, l - SSD_HEADDIM, 1), rb)
        for cl in range(nch):
            src = even if cl % 2 == 0 else odd
            v0 = (cl // 2) * hp
            csrow[cl, p:p + 1, :] = src[:, v0:v0 + hp]

    sout_ref[1:, :, :, :] = jnp.zeros((nch - 1,) + tuple(sout_ref.shape[1:]), F32)

    ri = lax.broadcasted_iota(I32, (CHUNK, hp), 0)
    li = lax.broadcasted_iota(I32, (CHUNK, hp), 1)
    left = li < SSD_HEADDIM
    causal = ri >= (li & (SSD_HEADDIM - 1))
    top = lax.broadcasted_iota(I32, (hp, 1), 0) < SSD_HEADDIM
    zpad_b = jnp.zeros((CHUNK, SSD_STATE), BF16)
    zpad_f = jnp.zeros((CHUNK, hp), F32)

    def chunk(c, carry):
        base = pl.multiple_of(c * CHUNK, CHUNK)
        gi = i * nch + c

        @pl.when(kind_ref[gi] == 1)
        def _():
            state[...] = s0_ref[c]

        dt_blk = dtcol[pl.ds(base, CHUNK), :]
        cs_blk = cscol[pl.ds(base, CHUNK), :]
        csr = csrow[c]
        ys = []
        for g in range(SSD_NGROUPS):
            b0 = SSD_WIDTH + g * SSD_STATE
            c0 = SSD_WIDTH + SSD_NGROUPS * SSD_STATE + g * SSD_STATE
            bf = act[pl.ds(base, CHUNK), b0:b0 + SSD_STATE]
            cf = act[pl.ds(base, CHUNK), c0:c0 + SSD_STATE]
            bg = bf.astype(BF16)
            cg = cf.astype(BF16)
            cbw = _dot_nt(cg, jnp.concatenate([bg, bg], axis=0))
            for q in range(SSD_PAIRS // SSD_NGROUPS):
                p = g * (SSD_PAIRS // SSD_NGROUPS) + q
                h0 = SM_DT + 2 * p
                xp = act[pl.ds(base, CHUNK), p * hp:(p + 1) * hp]
                dtp = jnp.where(left, dt_blk[:, h0:h0 + 1], dt_blk[:, h0 + 1:h0 + 2])
                csp = jnp.where(left, cs_blk[:, h0:h0 + 1], cs_blk[:, h0 + 1:h0 + 2])
                cl0 = cs_blk[CHUNK - 1:CHUNK, h0:h0 + 1]
                cl1 = cs_blk[CHUNK - 1:CHUNK, h0 + 1:h0 + 2]
                xdt = xp * dtp
                seg = jnp.where(causal, jnp.exp(jnp.minimum(csp - csr[p:p + 1, :], 0.0)), 0.0)
                scores = (cbw * seg).astype(BF16)
                bd = jnp.concatenate([jnp.where(left, xdt, 0.0), jnp.where(left, 0.0, xdt)], axis=0)
                y = _dot(scores, bd.astype(BF16))
                sp = state[p]
                sb = sp.astype(BF16)
                xdt_t = jnp.concatenate([xdt, zpad_f], axis=0).T.astype(BF16)
                yo, st = [], []
                for hh in range(2):
                    ecs = jnp.exp(cs_blk[:, h0 + hh:h0 + hh + 1])
                    cl = cl0 if hh == 0 else cl1
                    dec = jnp.exp(cl - cs_blk[:, h0 + hh:h0 + hh + 1])
                    yo.append(_dot_nt((cf * ecs).astype(BF16), sb))
                    st.append(_dot(xdt_t, jnp.concatenate([(bf * dec).astype(BF16), zpad_b], axis=0)))
                y = y + jnp.where(left, yo[0], yo[1])
                y = y + xp * dsk_ref[:, p * hp:(p + 1) * hp]
                state[p] = sp * jnp.where(top, jnp.exp(cl0), jnp.exp(cl1)) + jnp.where(top, st[0], st[1])
                ys.append(y)
        yf = jnp.concatenate(ys, axis=1)
        yg = yf * _silu(z_ref[pl.ds(base, CHUNK), :].astype(F32))
        out = yg * lax.rsqrt(jnp.mean(yg * yg, axis=-1, keepdims=True) + EPS) * nw_ref[...]
        o_ref[pl.ds(base, CHUNK), :] = out.astype(BF16)

        @pl.when(emit_ref[gi] >= 0)
        def _():
            sout_ref[emit_ref[gi]] = state[...]

        return carry

    lax.fori_loop(0, nch, chunk, 0)


def _ssd(proj, cache_slots, s0_slots, conv_w, conv_b, a_log, dt_bias, d_skip, norm_w, tables, n_out_blk):
    t = proj.shape[0]
    l = MIX_TILE
    nch = TILE_CHUNKS
    hp = 2 * SSD_HEADDIM
    kind, emit, in_blk, out_blk = tables
    cw = jnp.concatenate([conv_w, conv_b[None, :], jnp.zeros((SUBLANES - CONV_K - 1, SSD_CONV), F32)], axis=0)
    a_neg = -jnp.exp(a_log.astype(F32))
    pre = jnp.zeros((SM_DT,), F32)
    post = jnp.zeros((LANES - SM_DT - SSD_HEADS,), F32)
    parr = jnp.zeros((SUBLANES, LANES), F32)
    parr = parr.at[0].set(jnp.concatenate([pre, a_neg, post]))
    parr = parr.at[1].set(jnp.concatenate([pre, dt_bias.astype(F32), post]))
    parc = jnp.broadcast_to(jnp.concatenate([a_neg, dt_bias.astype(F32)])[:, None], (2 * SSD_HEADS, LANES))
    dsk = jnp.repeat(d_skip.astype(F32), SSD_HEADDIM).reshape(1, SSD_WIDTH)
    grid_spec = pltpu.PrefetchScalarGridSpec(
        num_scalar_prefetch=4,
        grid=(t // l,),
        in_specs=[
            pl.BlockSpec((l, SSD_CONV), lambda i, *_: (i, P_XBC // SSD_CONV)),
            pl.BlockSpec((l, SSD_WIDTH), lambda i, *_: (i, P_ZC // SSD_WIDTH)),
            pl.BlockSpec((l, LANES), lambda i, *_: (i, P_SMALL // LANES)),
            pl.BlockSpec((nch, SUBLANES, SSD_CONV), lambda i, k, e, ib, ob: (ib[i], 0, 0)),
            pl.BlockSpec((nch, SSD_PAIRS, hp, SSD_STATE), lambda i, k, e, ib, ob: (ib[i], 0, 0, 0)),
            pl.BlockSpec((SUBLANES, SSD_CONV), lambda i, *_: (0, 0)),
            pl.BlockSpec((SUBLANES, LANES), lambda i, *_: (0, 0)),
            pl.BlockSpec((2 * SSD_HEADS, LANES), lambda i, *_: (0, 0)),
            pl.BlockSpec((1, SSD_WIDTH), lambda i, *_: (0, 0)),
            pl.BlockSpec((1, SSD_WIDTH), lambda i, *_: (0, 0)),
        ],
        out_specs=[
            pl.BlockSpec((l, SSD_WIDTH), lambda i, *_: (i, 0)),
            pl.BlockSpec((nch, SSD_PAIRS, hp, SSD_STATE), lambda i, k, e, ib, ob: (ob[i], 0, 0, 0)),
        ],
        scratch_shapes=[
            pltpu.VMEM((SUBLANES, SSD_CONV), F32),
            pltpu.VMEM((l, SSD_CONV), F32),
            pltpu.VMEM((2 * SUBLANES, SSD_CONV), F32),
            pltpu.VMEM((SSD_PAIRS, hp, SSD_STATE), F32),
            pltpu.VMEM((l, LANES), F32),
            pltpu.VMEM((l, LANES), F32),
            pltpu.VMEM((nch, SSD_PAIRS, hp), F32),
        ],
    )
    return pl.pallas_call(
        _ssd_kernel,
        grid_spec=grid_spec,
        out_shape=[jax.ShapeDtypeStruct((t, SSD_WIDTH), BF16),
                   jax.ShapeDtypeStruct((n_out_blk * nch, SSD_PAIRS, hp, SSD_STATE), F32)],
        compiler_params=_cparams(("arbitrary",)),
        name="ssd_mixer",
    )(jnp.asarray(kind), jnp.asarray(emit), jnp.asarray(in_blk), jnp.asarray(out_blk),
      proj, proj, proj, cache_slots, s0_slots, cw, parr, parc,
      norm_w.reshape(1, SSD_WIDTH).astype(F32), dsk)


S5_SB = 2
S5_LB = 512


def _s5_kernel(kind_ref, u_ref, h0r_ref, h0i_ref, perm_ref, permt_ref, bre_ref, bim_ref, cre_ref, cim_ref,
               ar_ref, ai_ref, dsk_ref, wglu_ref, bglu_ref,
               o_ref, hfr_ref, hfi_ref,
               bur, bui, pre, pim, cr, ci, inr, ini):
    i = pl.program_id(0)
    l = u_ref.shape[0]
    nch = l // CHUNK
    n = S5_N
    usb = S5_WIDTH // S5_SB
    nsb = n // S5_SB

    @pl.when(i == 0)
    def _():
        pre[0:1, :] = ar_ref[...]
        pim[0:1, :] = ai_ref[...]

        def pw(t, carry):
            pr = pre[pl.ds(t - 1, 1), :]
            pi = pim[pl.ds(t - 1, 1), :]
            pre[pl.ds(t, 1), :] = pr * ar_ref[...] - pi * ai_ref[...]
            pim[pl.ds(t, 1), :] = pr * ai_ref[...] + pi * ar_ref[...]
            return carry

        lax.fori_loop(1, CHUNK, pw, 0)
        cr[...] = jnp.zeros((1, n), F32)
        ci[...] = jnp.zeros((1, n), F32)

    up = _dot(perm_ref[...], u_ref[...].astype(BF16)).astype(BF16)
    for sb in range(S5_SB):
        us = up[:, sb * usb:(sb + 1) * usb]
        bur[:, sb * nsb:(sb + 1) * nsb] = _dot(us, bre_ref[sb])
        bui[:, sb * nsb:(sb + 1) * nsb] = _dot(us, bim_ref[sb])

    for c0 in range(0, n, S5_LB):
        a_r = jnp.broadcast_to(ar_ref[:, c0:c0 + S5_LB], (nch, S5_LB))
        a_i = jnp.broadcast_to(ai_ref[:, c0:c0 + S5_LB], (nch, S5_LB))

        def step(t, carry):
            hr, hi = carry
            r0 = pl.multiple_of(t * nch, nch)
            nr = a_r * hr - a_i * hi + bur[pl.ds(r0, nch), c0:c0 + S5_LB]
            ni = a_r * hi + a_i * hr + bui[pl.ds(r0, nch), c0:c0 + S5_LB]
            bur[pl.ds(r0, nch), c0:c0 + S5_LB] = nr
            bui[pl.ds(r0, nch), c0:c0 + S5_LB] = ni
            return nr, ni

        z = jnp.zeros((nch, S5_LB), F32)
        lax.fori_loop(0, CHUNK, step, (z, z))

    a64r = pre[CHUNK - 1:CHUNK, :]
    a64i = pim[CHUNK - 1:CHUNK, :]
    c_r = cr[...]
    c_i = ci[...]
    for s in range(nch):
        start = kind_ref[i * nch + s] == 1
        i_r = jnp.where(start, h0r_ref[s:s + 1, :], c_r)
        i_i = jnp.where(start, h0i_ref[s:s + 1, :], c_i)
        inr[s:s + 1, :] = i_r
        ini[s:s + 1, :] = i_i
        e_r = bur[l - nch + s:l - nch + s + 1, :]
        e_i = bui[l - nch + s:l - nch + s + 1, :]
        c_r = a64r * i_r - a64i * i_i + e_r
        c_i = a64r * i_i + a64i * i_r + e_i
        hfr_ref[s:s + 1, :] = c_r
        hfi_ref[s:s + 1, :] = c_i
    cr[...] = c_r
    ci[...] = c_i

    for c0 in range(0, n, S5_LB):
        n_r = inr[:, c0:c0 + S5_LB]
        n_i = ini[:, c0:c0 + S5_LB]

        def fix(t, carry):
            r0 = pl.multiple_of(t * nch, nch)
            p_r = pre[pl.ds(t, 1), c0:c0 + S5_LB]
            p_i = pim[pl.ds(t, 1), c0:c0 + S5_LB]
            bur[pl.ds(r0, nch), c0:c0 + S5_LB] += p_r * n_r - p_i * n_i
            bui[pl.ds(r0, nch), c0:c0 + S5_LB] += p_r * n_i + p_i * n_r
            return carry

        lax.fori_loop(0, CHUNK, fix, 0)

    ys = []
    for sb in range(S5_SB):
        hr = bur[:, sb * nsb:(sb + 1) * nsb].astype(BF16)
        hi = bui[:, sb * nsb:(sb + 1) * nsb].astype(BF16)
        ys.append(_dot(hr, cre_ref[sb]) - _dot(hi, cim_ref[sb]))
    yp = jnp.concatenate(ys, axis=1)
    y_hi = yp.astype(BF16)
    r1 = yp - y_hi.astype(F32)
    y_mid = r1.astype(BF16)
    y_lo = (r1 - y_mid.astype(F32)).astype(BF16)
    pt = permt_ref[...]
    y = (_dot(pt, y_hi) + _dot(pt, y_mid)) + _dot(pt, y_lo)
    y = y + u_ref[...] * dsk_ref[...]
    y = y * (0.5 * (1.0 + jnp.tanh(math.sqrt(2.0 / math.pi) * (y + 0.044715 * (y * y * y)))))
    out = y * jax.nn.sigmoid(_dot(y.astype(BF16), wglu_ref[...]) + bglu_ref[...])
    o_ref[...] = out.astype(BF16)


def _s5_tables(a_re, a_im, b_re, b_im, c_re, c_im, log_dt):
    a_re, a_im = a_re.astype(F32), a_im.astype(F32)
    dt = jnp.exp(log_dt.astype(F32))[:, None]
    mag = jnp.exp(dt * a_re)
    abar_re, abar_im = mag * jnp.cos(dt * a_im), mag * jnp.sin(dt * a_im)
    den = a_re * a_re + a_im * a_im
    num_re, num_im = abar_re - 1.0, abar_im
    zoh_re = (num_re * a_re + num_im * a_im) / den
    zoh_im = (num_im * a_re - num_re * a_im) / den
    b_re, b_im = b_re.astype(F32), b_im.astype(F32)
    bbar_re = zoh_re[..., None] * b_re - zoh_im[..., None] * b_im
    bbar_im = zoh_re[..., None] * b_im + zoh_im[..., None] * b_re
    gsb = S5_GROUPS // S5_SB
    eye = jnp.eye(gsb, dtype=F32)

    def bblk(b):
        b = b.reshape(S5_SB, gsb, S5_STATE, S5_GROUP)
        return jnp.einsum('sgpc,gh->sgchp', b, eye).reshape(S5_SB, gsb * S5_GROUP, gsb * S5_STATE).astype(BF16)

    def cblk(c):
        c = c.astype(F32).reshape(S5_SB, gsb, S5_GROUP, S5_STATE)
        return jnp.einsum('sgcp,gh->sgphc', c, eye).reshape(S5_SB, gsb * S5_STATE, gsb * S5_GROUP).astype(BF16)

    return (abar_re.reshape(1, S5_N), abar_im.reshape(1, S5_N),
            bblk(bbar_re), bblk(bbar_im), cblk(c_re), cblk(c_im))


def _s5_perm():
    r_new = np.arange(MIX_TILE)
    r_old = (r_new % TILE_CHUNKS) * CHUNK + r_new // TILE_CHUNKS
    p = np.zeros((MIX_TILE, MIX_TILE), np.float32)
    p[r_new, r_old] = 1.0
    return jnp.asarray(p, BF16), jnp.asarray(p.T, BF16)


def _s5(proj, h0r, h0i, tabs, d_skip, w_glu, b_glu, kind):
    t = proj.shape[0]
    l = MIX_TILE
    nch = TILE_CHUNKS
    abr, abi, bre, bim, cre, cim = tabs
    perm, permt = _s5_perm()
    full2 = lambda a: pl.BlockSpec(a.shape, lambda i, *_: (0, 0))
    full3 = lambda a: pl.BlockSpec(a.shape, lambda i, *_: (0, 0, 0))
    dsk = d_skip.astype(F32).reshape(1, S5_WIDTH)
    wg = w_glu.astype(BF16)
    bg = b_glu.astype(F32).reshape(1, S5_WIDTH)
    grid_spec = pltpu.PrefetchScalarGridSpec(
        num_scalar_prefetch=1,
        grid=(t // l,),
        in_specs=[
            pl.BlockSpec((l, S5_WIDTH), lambda i, *_: (i, P_UB // S5_WIDTH)),
            pl.BlockSpec((nch, S5_N), lambda i, *_: (i, 0)),
            pl.BlockSpec((nch, S5_N), lambda i, *_: (i, 0)),
            full2(perm), full2(permt), full3(bre), full3(bim), full3(cre), full3(cim),
            full2(abr), full2(abi), full2(dsk), full2(wg), full2(bg),
        ],
        out_specs=[
            pl.BlockSpec((l, S5_WIDTH), lambda i, *_: (i, 0)),
            pl.BlockSpec((nch, S5_N), lambda i, *_: (i, 0)),
            pl.BlockSpec((nch, S5_N), lambda i, *_: (i, 0)),
        ],
        scratch_shapes=[
            pltpu.VMEM((l, S5_N), F32), pltpu.VMEM((l, S5_N), F32),
            pltpu.VMEM((CHUNK, S5_N), F32), pltpu.VMEM((CHUNK, S5_N), F32),
            pltpu.VMEM((1, S5_N), F32), pltpu.VMEM((1, S5_N), F32),
            pltpu.VMEM((nch, S5_N), F32), pltpu.VMEM((nch, S5_N), F32),
        ],
    )
    nseg = t // CHUNK
    return pl.pallas_call(
        _s5_kernel,
        grid_spec=grid_spec,
        out_shape=[jax.ShapeDtypeStruct((t, S5_WIDTH), BF16),
                   jax.ShapeDtypeStruct((nseg, S5_N), F32),
                   jax.ShapeDtypeStruct((nseg, S5_N), F32)],
        compiler_params=_cparams(("arbitrary",)),
        name="s5_mixer",
    )(jnp.asarray(kind), proj, h0r, h0i, perm, permt, bre, bim, cre, cim, abr, abi, dsk, wg, bg)


MIXOUT_TM = 512
R_GRP = 0
R_EXP = SUBLANES
NEG_BIG = -1e30


def _mixout_kernel(oa_ref, ob_ref, oc_ref, w_ref, x_ref, nw_ref, wr_ref, rb_ref, x1_ref, h2_ref, lg_ref):
    acc = _dot(oa_ref[...], w_ref[0:GDN_KD, :])
    acc = acc + _dot(ob_ref[...], w_ref[GDN_KD:GDN_KD + S5_WIDTH, :])
    acc = acc + _dot(oc_ref[...], w_ref[GDN_KD + S5_WIDTH:, :])
    x1 = x_ref[...] + acc
    x1_ref[...] = x1
    h = x1 * lax.rsqrt(jnp.mean(x1 * x1, axis=-1, keepdims=True) + EPS) * nw_ref[...]
    h2_ref[...] = h
    lg_ref[...] = _dot(h.astype(BF16), wr_ref[...]) + rb_ref[...]


def _mixout(oa, ob, oc, w_out, x, nw, wr, rb):
    t, d = x.shape
    tm = MIXOUT_TM
    row = lambda w: pl.BlockSpec((tm, w), lambda i: (i, 0))
    full = lambda a: pl.BlockSpec(a.shape, lambda i: (0, 0))
    return pl.pallas_call(
        _mixout_kernel,
        grid=(t // tm,),
        in_specs=[row(GDN_KD), row(S5_WIDTH), row(SSD_WIDTH), full(w_out), row(d), full(nw), full(wr), full(rb)],
        out_specs=[row(d), row(d), row(LANES)],
        out_shape=[jax.ShapeDtypeStruct((t, d), F32), jax.ShapeDtypeStruct((t, d), F32),
                   jax.ShapeDtypeStruct((t, LANES), F32)],
        compiler_params=_cparams(("parallel",)),
        name="mix_out",
    )(oa, ob, oc, w_out, x, nw, wr, rb)


def _router_weights(rg_w, rg_b, re_w, re_b):
    d = rg_w.shape[0]
    wr = jnp.concatenate([rg_w, jnp.zeros((d, R_EXP - N_GROUPS), F32), re_w,
                          jnp.zeros((d, LANES - R_EXP - N_EXPERTS), F32)], axis=1).astype(BF16)
    rb = jnp.concatenate([rg_b.astype(F32), jnp.full((R_EXP - N_GROUPS,), NEG_BIG, F32), re_b.astype(F32),
                          jnp.zeros((LANES - R_EXP - N_EXPERTS,), F32)]).reshape(1, LANES)
    return wr, rb


ROUTE_TM = 512


def _router_kernel(lg_ref, tri_ref, idx_ref, gate_ref, cnt_ref, run):
    i = pl.program_id(0)
    tm = lg_ref.shape[0]

    @pl.when(i == 0)
    def _():
        run[...] = jnp.zeros(run.shape, F32)

    lt = lg_ref[...].T
    row8 = lax.broadcasted_iota(I32, (SUBLANES, tm), 0)
    grp = lt[R_GRP:R_GRP + SUBLANES, :]
    gm = jnp.max(grp, axis=0, keepdims=True)
    gp_top = 1.0 / jnp.sum(jnp.exp(grp - gm), axis=0, keepdims=True)
    g_top = jnp.min(jnp.where(grp == gm, row8, SUBLANES), axis=0, keepdims=True)
    ing = jnp.zeros((EPG, tm), F32)
    for g in range(N_GROUPS):
        ing = jnp.where(g_top == g, lt[R_EXP + g * EPG:R_EXP + (g + 1) * EPG, :], ing)
    em = jnp.max(ing, axis=0, keepdims=True)
    ee = jnp.exp(ing - em)
    p = ee / jnp.sum(ee, axis=0, keepdims=True)
    v1 = jnp.max(p, axis=0, keepdims=True)
    i1 = jnp.min(jnp.where(p == v1, row8, EPG), axis=0, keepdims=True)
    p2 = jnp.where(row8 == i1, -1.0, p)
    v2 = jnp.max(p2, axis=0, keepdims=True)
    i2 = jnp.min(jnp.where(p2 == v2, row8, EPG), axis=0, keepdims=True)
    den = v1 + v2
    gate1 = gp_top * v1 / den
    gate2 = gp_top * v2 / den
    e1 = g_top * EPG + i1
    e2 = g_top * EPG + i2

    erow = lax.broadcasted_iota(I32, (N_EXPERTS, tm), 0)
    hit1 = erow == e1
    hit2 = erow == e2
    oh = jnp.where(hit1 | hit2, 1.0, 0.0)
    before = _dot(oh.astype(BF16), tri_ref[...]) + run[:, 0:1]
    rank1 = jnp.sum(jnp.where(hit1, before, 0.0), axis=0, keepdims=True).astype(I32)
    rank2 = jnp.sum(jnp.where(hit2, before, 0.0), axis=0, keepdims=True).astype(I32)
    run[...] = run[...] + jnp.sum(oh, axis=1, keepdims=True)
    cnt_ref[...] = run[...].astype(I32)

    zi = jnp.zeros((SUBLANES - 4, tm), I32)
    idx_ref[...] = jnp.concatenate([e1, e2, rank1, rank2, zi], axis=0)
    r128 = lax.broadcasted_iota(I32, (LANES, tm), 0)
    gt = jnp.where(r128 == 0, gate1, jnp.where(r128 == 1, gate2, 0.0))
    gate_ref[...] = gt.T


def _router(logits):
    t = logits.shape[0]
    tm = ROUTE_TM
    tri = jnp.asarray(np.triu(np.ones((tm, tm), np.float32), 1), BF16)
    return pl.pallas_call(
        _router_kernel,
        grid=(t // tm,),
        in_specs=[pl.BlockSpec((tm, LANES), lambda i: (i, 0)),
                  pl.BlockSpec((tm, tm), lambda i: (0, 0))],
        out_specs=[pl.BlockSpec((SUBLANES, tm), lambda i: (0, i)),
                   pl.BlockSpec((tm, LANES), lambda i: (i, 0)),
                   pl.BlockSpec((N_EXPERTS, LANES), lambda i: (0, 0))],
        out_shape=[jax.ShapeDtypeStruct((SUBLANES, t), I32),
                   jax.ShapeDtypeStruct((t, LANES), F32),
                   jax.ShapeDtypeStruct((N_EXPERTS, LANES), I32)],
        scratch_shapes=[pltpu.VMEM((N_EXPERTS, LANES), F32)],
        compiler_params=_cparams(("arbitrary",)),
        name="router",
    )(logits, tri)


MOE_BLOCK = 256
MOE_BLOCK_SHIFT = 8
DMA_UNROLL = 8
DISPATCH_TM = 512
COMBINE_TM = 256


def _dispatch_kernel(pstart_ref, cnt_ref, nv_ref, idx_ref, h2_ref, xs_ref, zbuf, sem):
    i = pl.program_id(0)
    tm = idx_ref.shape[1]
    nb = xs_ref.shape[0] // MOE_BLOCK

    @pl.when(i == 0)
    def _():
        zbuf[...] = jnp.zeros(zbuf.shape, F32)

        def pad_copy(e, r):
            return pltpu.make_async_copy(zbuf.at[pl.ds(0, 1)], xs_ref.at[pl.ds(pstart_ref[e] + r, 1)], sem)

        def per_expert(e, carry):
            n = cnt_ref[e]
            padded = ((n + MOE_BLOCK - 1) >> MOE_BLOCK_SHIFT) << MOE_BLOCK_SHIFT
            lax.fori_loop(n, padded, lambda r, c: (pad_copy(e, r).start(), c)[1], 0)
            lax.fori_loop(n, padded, lambda r, c: (pad_copy(e, r).wait(), c)[1], 0)
            return carry

        lax.fori_loop(0, N_EXPERTS, per_expert, 0)

        def blk_copy(b):
            return pltpu.make_async_copy(zbuf, xs_ref.at[pl.ds(b * MOE_BLOCK, MOE_BLOCK)], sem)

        lax.fori_loop(nv_ref[0], nb, lambda b, c: (blk_copy(b).start(), c)[1], 0)
        lax.fori_loop(nv_ref[0], nb, lambda b, c: (blk_copy(b).wait(), c)[1], 0)

    def copy(t, k):
        slot = pstart_ref[idx_ref[k, t]] + idx_ref[2 + k, t]
        return pltpu.make_async_copy(h2_ref.at[pl.ds(t, 1)], xs_ref.at[pl.ds(slot, 1)], sem)

    def issue(t, carry):
        copy(t, 0).start()
        copy(t, 1).start()
        return carry

    def drain(t, carry):
        copy(t, 0).wait()
        copy(t, 1).wait()
        return carry

    lax.fori_loop(0, tm, issue, 0, unroll=DMA_UNROLL)
    lax.fori_loop(0, tm, drain, 0, unroll=DMA_UNROLL)


def _dispatch(pad_start, counts, n_valid, idx, h2, n_slots):
    t, d = h2.shape
    tm = DISPATCH_TM
    grid_spec = pltpu.PrefetchScalarGridSpec(
        num_scalar_prefetch=3,
        grid=(t // tm,),
        in_specs=[pl.BlockSpec((SUBLANES, tm), lambda i, *_: (0, i), memory_space=pltpu.SMEM),
                  pl.BlockSpec((tm, d), lambda i, *_: (i, 0))],
        out_specs=pl.BlockSpec(memory_space=pl.ANY),
        scratch_shapes=[pltpu.VMEM((MOE_BLOCK, d), F32), pltpu.SemaphoreType.DMA(())],
    )
    return pl.pallas_call(
        _dispatch_kernel,
        grid_spec=grid_spec,
        out_shape=jax.ShapeDtypeStruct((n_slots, d), F32),
        compiler_params=_cparams(("arbitrary",)),
        name="moe_dispatch",
    )(pad_start, counts, n_valid, idx, h2)


def _expert_kernel(be_ref, nv_ref, xs_ref, wg_ref, wu_ref, wd_ref, ys_ref, wg_s, wu_s, wd_s):
    b = pl.program_id(0)
    valid = b < nv_ref[0]

    @pl.when(valid & ((b == 0) | (be_ref[b] != be_ref[jnp.maximum(b - 1, 0)])))
    def _():
        wg_s[...] = wg_ref[0].astype(BF16)
        wu_s[...] = wu_ref[0].astype(BF16)
        wd_s[...] = wd_ref[0].astype(BF16)

    @pl.when(valid)
    def _():
        x = xs_ref[...].astype(BF16)
        g = _dot(x, wg_s[...])
        u = _dot(x, wu_s[...])
        h = (_silu(g) * u).astype(BF16)
        ys_ref[...] = _dot(h, wd_s[...])

    @pl.when(jnp.logical_not(valid))
    def _():
        ys_ref[...] = jnp.zeros(ys_ref.shape, F32)


def _experts(block_expert, n_valid, xs, wg, wu, wd, layer):
    n_slots, d = xs.shape
    de = wg.shape[3]
    nb = n_slots // MOE_BLOCK
    blk = lambda b, be, nv: (jnp.minimum(b, nv[0] - 1), 0)
    wsel = lambda b, be, nv: (layer, be[jnp.minimum(b, nv[0] - 1)], 0, 0)
    grid_spec = pltpu.PrefetchScalarGridSpec(
        num_scalar_prefetch=2,
        grid=(nb,),
        in_specs=[pl.BlockSpec((MOE_BLOCK, d), blk),
                  pl.BlockSpec((None, 1, d, de), wsel),
                  pl.BlockSpec((None, 1, d, de), wsel),
                  pl.BlockSpec((None, 1, de, d), wsel)],
        out_specs=pl.BlockSpec((MOE_BLOCK, d), lambda b, be, nv: (b, 0)),
        scratch_shapes=[pltpu.VMEM((d, de), BF16), pltpu.VMEM((d, de), BF16), pltpu.VMEM((de, d), BF16)],
    )
    return pl.pallas_call(
        _expert_kernel,
        grid_spec=grid_spec,
        out_shape=jax.ShapeDtypeStruct((n_slots, d), F32),
        compiler_params=_cparams(("arbitrary",)),
        name="moe_experts",
    )(block_expert, n_valid, xs, wg, wu, wd)


def _combine_kernel(pstart_ref, idx_ref, x1_ref, gate_ref, ys_ref, nw_ref, *rest, n_first):
    if n_first is None:
        out_ref, ybuf, sem = rest
    else:
        out_a_ref, out_b_ref, ybuf, sem = rest
    tm = x1_ref.shape[0]

    def copy(t, k):
        slot = pstart_ref[idx_ref[k, t]] + idx_ref[2 + k, t]
        return pltpu.make_async_copy(ys_ref.at[pl.ds(slot, 1)], ybuf.at[k, pl.ds(t, 1)], sem)

    def issue(t, carry):
        copy(t, 0).start()
        copy(t, 1).start()
        return carry

    def drain(t, carry):
        copy(t, 0).wait()
        copy(t, 1).wait()
        return carry

    lax.fori_loop(0, tm, issue, 0, unroll=DMA_UNROLL)
    lax.fori_loop(0, tm, drain, 0, unroll=DMA_UNROLL)
    g = gate_ref[...]
    y = ybuf[0] * g[:, 0:1] + ybuf[1] * g[:, 1:2]
    x2 = x1_ref[...] + y
    if n_first is None:
        out_ref[...] = x2
    else:
        x2 = x2 * lax.rsqrt(jnp.mean(x2 * x2, axis=-1, keepdims=True) + EPS) * nw_ref[...]
        i = pl.program_id(0)

        @pl.when(i < n_first)
        def _():
            out_a_ref[...] = x2

        @pl.when(i >= n_first)
        def _():
            out_b_ref[...] = x2


def _combine(pad_start, idx, x1, gates, ys, nw, t_first):
    t, d = x1.shape
    tm = COMBINE_TM
    row = pl.BlockSpec((tm, d), lambda i, *_: (i, 0))
    if t_first is None:
        n_first = None
        out_specs = row
        out_shape = jax.ShapeDtypeStruct((t, d), F32)
    else:
        assert t_first % tm == 0
        n_first = t_first // tm
        out_specs = [pl.BlockSpec((tm, d), lambda i, *_: (jnp.minimum(i, n_first - 1), 0)),
                     pl.BlockSpec((tm, d), lambda i, *_: (jnp.maximum(i - n_first, 0), 0))]
        out_shape = [jax.ShapeDtypeStruct((t_first, d), F32), jax.ShapeDtypeStruct((t - t_first, d), F32)]
    grid_spec = pltpu.PrefetchScalarGridSpec(
        num_scalar_prefetch=1,
        grid=(t // tm,),
        in_specs=[pl.BlockSpec((SUBLANES, tm), lambda i, *_: (0, i), memory_space=pltpu.SMEM),
                  row,
                  pl.BlockSpec((tm, LANES), lambda i, *_: (i, 0)),
                  pl.BlockSpec(memory_space=pl.ANY),
                  pl.BlockSpec((1, d), lambda i, *_: (0, 0))],
        out_specs=out_specs,
        scratch_shapes=[pltpu.VMEM((2, tm, d), F32), pltpu.SemaphoreType.DMA(())],
    )
    return pl.pallas_call(
        functools.partial(_combine_kernel, n_first=n_first),
        grid_spec=grid_spec,
        out_shape=out_shape,
        compiler_params=_cparams(("arbitrary",)),
        name="moe_combine",
    )(pad_start, idx, x1, gates, ys, nw)


def _moe(x1, h2, logits, wg, wu, wd, layer, norm_final, t_first):
    t, d = x1.shape
    idx, gates, cnt = _router(logits)
    counts = cnt[:, 0]
    padded = ((counts + MOE_BLOCK - 1) >> MOE_BLOCK_SHIFT) << MOE_BLOCK_SHIFT
    pad_end = jnp.cumsum(padded)
    pad_start = (pad_end - padded).astype(I32)
    nb = (2 * t + N_EXPERTS * (MOE_BLOCK - 1) + MOE_BLOCK - 1) // MOE_BLOCK
    n_valid = (pad_end[-1] >> MOE_BLOCK_SHIFT).astype(I32).reshape(1)
    starts = jnp.arange(nb, dtype=I32) * MOE_BLOCK
    block_expert = jnp.minimum(jnp.sum((pad_end[None, :] <= starts[:, None]).astype(I32), axis=1),
                               N_EXPERTS - 1).astype(I32)
    xs = _dispatch(pad_start, counts, n_valid, idx, h2, nb * MOE_BLOCK)
    ys = _experts(block_expert, n_valid, xs, wg, wu, wd, layer)
    return _combine(pad_start, idx, x1, gates, ys, norm_final, t_first)


def _stream_ends(nbp, seq, nbs, dseq):
    ends = [(b + 1) * seq for b in range(nbp)] + [nbp * seq + (s + 1) * dseq for s in range(nbs)]
    return np.asarray(ends)


def kernel(x_prompt, x_sample, cache_conv_gdn, state_gdn, state_s5, cache_conv_ssd, state_ssd, norm_mix, w_in, gdn_conv_w, gdn_a_log, gdn_dt_bias, gdn_norm, s5_a_re, s5_a_im, s5_b_re, s5_b_im, s5_c_re, s5_c_im, s5_log_dt, s5_d, s5_w_glu, s5_b_glu, ssd_conv_w, ssd_conv_b, ssd_a_log, ssd_dt_bias, ssd_d, ssd_norm, w_out, norm_ffn, router_group_w, router_group_b, router_expert_w, router_expert_b, expert_w_gate, expert_w_up, expert_w_down, norm_final):
    nbp, seq, d = x_prompt.shape
    nbs, dseq, _ = x_sample.shape
    depth = w_in.shape[0]
    tp = nbp * seq
    t = tp + nbs * dseq
    x = jnp.concatenate([x_prompt.reshape(tp, d), x_sample.reshape(nbs * dseq, d)], axis=0)

    tables = _chunk_tables(nbp, seq, nbs, dseq)
    kind = tables[0]
    n_out_blk = nbp + nbs // TILE_CHUNKS
    ends = _stream_ends(nbp, seq, nbs, dseq)
    tail_rows = (ends[:, None] + np.arange(-(CONV_K - 1), 0)[None, :]).reshape(-1)
    end_seg = ends // CHUNK - 1
    ncp = tp // CHUNK
    state_rows = np.concatenate([np.arange(nbp) * TILE_CHUNKS, nbp * TILE_CHUNKS + np.arange(nbs)])

    new_conv_gdn, new_gdn, new_s5, new_conv_ssd, new_ssd = [], [], [], [], []
    for l in range(depth):
        proj = _proj(x, norm_mix[l].reshape(1, d).astype(F32), _rearrange_w_in(w_in[l]))

        oa, sg = _gdn(proj, _conv_cache_slots(cache_conv_gdn[l].astype(F32)),
                      _init_slots(state_gdn[l].astype(F32)), gdn_conv_w[l].astype(F32), gdn_a_log[l],
                      gdn_dt_bias[l], gdn_norm[l], tables, n_out_blk)

        h0 = state_s5[l].astype(F32).reshape(nbs, S5_N, 2)
        zeros_p = jnp.zeros((ncp, S5_N), F32)
        ob, hfr, hfi = _s5(proj, jnp.concatenate([zeros_p, h0[..., 0]], axis=0),
                           jnp.concatenate([zeros_p, h0[..., 1]], axis=0),
                           _s5_tables(s5_a_re[l], s5_a_im[l], s5_b_re[l], s5_b_im[l], s5_c_re[l], s5_c_im[l],
                                      s5_log_dt[l]),
                           s5_d[l], s5_w_glu[l], s5_b_glu[l], kind)

        oc, ss = _ssd(proj, _conv_cache_slots(cache_conv_ssd[l].astype(F32)),
                      _init_slots(state_ssd[l].astype(F32).reshape(nbs, SSD_PAIRS, 2 * SSD_HEADDIM, SSD_STATE)),
                      ssd_conv_w[l].astype(F32), ssd_conv_b[l].astype(F32), ssd_a_log[l], ssd_dt_bias[l],
                      ssd_d[l], ssd_norm[l], tables, n_out_blk)

        wr, rb = _router_weights(router_group_w[l].astype(F32), router_group_b[l],
                                 router_expert_w[l].astype(F32), router_expert_b[l])
        x1, h2, logits = _mixout(oa, ob, oc, w_out[l].astype(BF16), x,
                                 norm_ffn[l].reshape(1, d).astype(F32), wr, rb)
        x = _moe(x1, h2, logits, expert_w_gate, expert_w_up, expert_w_down, l,
                 norm_final.reshape(1, d).astype(F32), tp if l == depth - 1 else None)

        tails = proj[tail_rows]
        new_conv_gdn.append(tails[:, P_QKV:P_QKV + GDN_CONV].reshape(nbp + nbs, CONV_K - 1, GDN_CONV))
        new_conv_ssd.append(tails[:, P_XBC:P_XBC + SSD_CONV].reshape(nbp + nbs, CONV_K - 1, SSD_CONV))
        new_gdn.append(sg[state_rows])
        new_ssd.append(ss[state_rows].reshape(nbp + nbs, SSD_HEADS, SSD_HEADDIM, SSD_STATE))
        new_s5.append(jnp.stack([hfr[end_seg], hfi[end_seg]], axis=-1)
                      .reshape(nbp + nbs, S5_GROUPS, S5_STATE, 2))

    def split(parts):
        a = jnp.stack(parts)
        return a[:, :nbp], a[:, nbp:]

    cg_p, cg_s = split(new_conv_gdn)
    sg_p, sg_s = split(new_gdn)
    s5_p, s5_s = split(new_s5)
    cs_p, cs_s = split(new_conv_ssd)
    ss_p, ss_s = split(new_ssd)
    y_prompt = x[0].reshape(nbp, seq, d)
    y_sample = x[1].reshape(nbs, dseq, d)
    return (y_prompt, y_sample, cg_p, sg_p, s5_p, cs_p, ss_p, cg_s, sg_s, s5_s, cs_s, ss_s)
```

```python
import functools
import math

import numpy as np
import jax
import jax.numpy as jnp
from jax import lax
from jax.experimental import pallas as pl
from jax.experimental.pallas import tpu as pltpu

F32 = jnp.float32
BF16 = jnp.bfloat16
I32 = jnp.int32
U32 = jnp.uint32

EPS = 1e-6
CHUNK = 64
CHUNK_SHIFT = 6
CONV_K = 4
LANES = 128
SUBLANES = 8
VMEM_LIMIT = 56 * 1024 * 1024

GDN_HEADS = 8
GDN_DK = 128
GDN_DV = 128
GDN_KD = GDN_HEADS * GDN_DK
GDN_CONV = 3 * GDN_KD
GDN_GROUP = 4
GDN_SPLIT_STEPS = 2
S5_WIDTH = 512
S5_GROUPS = 32
S5_GROUP = 16
S5_STATE = 64
S5_N = S5_GROUPS * S5_STATE
SSD_WIDTH = 512
SSD_HEADS = 8
SSD_HEADDIM = 64
SSD_NGROUPS = 2
SSD_STATE = 128
SSD_CONV = SSD_WIDTH + 2 * SSD_NGROUPS * SSD_STATE
N_GROUPS = 4
EPG = 8
N_EXPERTS = 32

MIX_TILE = 512
TILE_CHUNKS = MIX_TILE // CHUNK


def _cparams(sem, vmem=VMEM_LIMIT):
    return pltpu.CompilerParams(dimension_semantics=sem, vmem_limit_bytes=vmem)


def _silu(x):
    return x * jax.nn.sigmoid(x)


def _softplus(x):
    return jnp.maximum(x, 0.0) + jnp.log1p(jnp.exp(-jnp.abs(x)))


def _dot(a, b):
    return jnp.dot(a, b, preferred_element_type=F32)


def _dot3(a, b):
    ah = a.astype(BF16)
    bh = b.astype(BF16)
    al = (a - ah.astype(F32)).astype(BF16)
    bl = (b - bh.astype(F32)).astype(BF16)
    return _dot(ah, bh) + (_dot(ah, bl) + _dot(al, bh))


def _pack_bf16_pairs(xb):
    n = xb.shape[1] // 2
    lo = lax.bitcast_convert_type(xb[:, :n].astype(F32), U32)
    hi = lax.bitcast_convert_type(xb[:, n:].astype(F32), U32)
    return (hi & jnp.uint32(0xFFFF0000)) | (lo >> 16)


def _unpack_bf16_pairs(w):
    lo = lax.bitcast_convert_type(w << 16, F32).astype(BF16)
    hi = lax.bitcast_convert_type(w & jnp.uint32(0xFFFF0000), F32).astype(BF16)
    return lo, hi


def _dot_nt(a, b):
    return lax.dot_general(a, b, (((1,), (1,)), ((), ())), preferred_element_type=F32)


def _dot_tn(a, b):
    return lax.dot_general(a, b, (((0,), (0,)), ((), ())), preferred_element_type=F32)


def _cumsum_rows(x):
    row = lax.broadcasted_iota(I32, x.shape, 0) & (CHUNK - 1)
    k = 1
    while k < CHUNK:
        x = x + jnp.where(row >= k, pltpu.roll(x, k, 0), 0.0)
        k *= 2
    return x


def _cumsum_lanes(x):
    lane = lax.broadcasted_iota(I32, x.shape, 1) & (CHUNK - 1)
    k = 1
    while k < CHUNK:
        x = x + jnp.where(lane >= k, pltpu.roll(x, k, 1), 0.0)
        k *= 2
    return x


PROJ_TM = 1024
PROJ_TN = 1280
P_QKV, P_ZA, P_UB, P_ZC, P_XBC = 0, 3072, 4096, 4608, 5120
P_MAIN = 6144
P_TOTAL = 6400
P_SMALL = P_TOTAL - LANES


def _proj_kernel(x_ref, nw_ref, w_ref, o_ref, h_scr):
    @pl.when(pl.program_id(1) == 0)
    def _():
        x = x_ref[...]
        ms = jnp.mean(x * x, axis=-1, keepdims=True)
        h_scr[...] = (x * lax.rsqrt(ms + EPS) * nw_ref[...]).astype(BF16)

    o_ref[...] = _dot(h_scr[...], w_ref[...])


def _proj(x, nw, w):
    t, d = x.shape
    n = w.shape[1]
    tm = min(PROJ_TM, t)
    assert t % tm == 0 and n % PROJ_TN == 0
    return pl.pallas_call(
        _proj_kernel,
        grid=(t // tm, n // PROJ_TN),
        in_specs=[pl.BlockSpec((tm, d), lambda i, j: (i, 0)),
                  pl.BlockSpec((1, d), lambda i, j: (0, 0)),
                  pl.BlockSpec((d, PROJ_TN), lambda i, j: (0, j))],
        out_specs=pl.BlockSpec((tm, PROJ_TN), lambda i, j: (i, j)),
        out_shape=jax.ShapeDtypeStruct((t, n), F32),
        scratch_shapes=[pltpu.VMEM((tm, d), BF16)],
        compiler_params=_cparams(("parallel", "arbitrary")),
        name="proj_in",
    )(x, nw, w)


def _rearrange_w_in(w_in):
    d = w_in.shape[0]
    off_za = GDN_CONV
    off_ba = off_za + GDN_KD
    off_s5 = off_ba + 2 * GDN_HEADS
    off_zc = off_s5 + S5_WIDTH
    off_xbc = off_zc + SSD_WIDTH
    off_dt = off_xbc + SSD_CONV
    small = jnp.concatenate([w_in[:, off_ba:off_s5], w_in[:, off_dt:],
                             jnp.zeros((d, LANES - 3 * GDN_HEADS), w_in.dtype)], axis=1)
    w = jnp.concatenate([w_in[:, :off_ba], w_in[:, off_s5:off_dt],
                         jnp.zeros((d, P_TOTAL - P_MAIN - LANES), w_in.dtype), small], axis=1)
    return w.astype(BF16)


def _chunk_tables(nbp, seq, nbs, dseq):
    assert seq % MIX_TILE == 0 and dseq == CHUNK and (nbs * dseq) % MIX_TILE == 0
    cps = seq // CHUNK
    ncp = nbp * cps
    nc = ncp + nbs
    kind = np.zeros((nc,), np.int32)
    emit = np.full((nc,), -1, np.int32)
    for c in range(nc):
        if c < ncp:
            kind[c] = 1 if c % cps == 0 else 0
            if c % TILE_CHUNKS == TILE_CHUNKS - 1:
                emit[c] = 0
        else:
            kind[c] = 1
            emit[c] = (c - ncp) % TILE_CHUNKS
    ntp = ncp // TILE_CHUNKS
    nts = nbs // TILE_CHUNKS
    in_blk = np.concatenate([np.zeros((ntp,), np.int32), 1 + np.arange(nts, dtype=np.int32)])
    out_blk = np.concatenate([np.arange(ntp, dtype=np.int32) // (cps // TILE_CHUNKS),
                              nbp + np.arange(nts, dtype=np.int32)])
    return kind, emit, in_blk, out_blk


def _init_slots(x):
    return jnp.concatenate([jnp.zeros((TILE_CHUNKS,) + x.shape[1:], x.dtype), x], axis=0)


def _conv_cache_slots(cache):
    nbs, k1, c = cache.shape
    padded = jnp.concatenate([jnp.zeros((nbs, SUBLANES - k1, c), cache.dtype), cache], axis=1)
    return _init_slots(padded)


def _conv_silu_tile(i, kind_ref, in_ref, cache_ref, cw_ref, tail, act, tmp, bias_row):
    l, c = in_ref.shape
    cb = 512
    k1 = CONV_K - 1
    nch = l // CHUNK

    @pl.when(i == 0)
    def _():
        tail[...] = jnp.zeros(tail.shape, F32)

    def taps(src, lo, hi, c0):
        acc = src[lo - k1:hi - k1, c0:c0 + cb] * cw_ref[0:1, c0:c0 + cb]
        for j in range(1, CONV_K):
            acc = acc + src[lo - k1 + j:hi - k1 + j, c0:c0 + cb] * cw_ref[j:j + 1, c0:c0 + cb]
        if bias_row is not None:
            acc = acc + cw_ref[bias_row:bias_row + 1, c0:c0 + cb]
        return _silu(acc)

    def head_rows(prev, r0):
        tmp[0:SUBLANES, :] = prev
        tmp[SUBLANES:2 * SUBLANES, :] = in_ref[r0:r0 + SUBLANES, :]
        for c0 in range(0, c, cb):
            act[r0:r0 + SUBLANES, c0:c0 + cb] = taps(tmp, SUBLANES, 2 * SUBLANES, c0)

    def taps_rolled(lo, hi, c0):
        a = max(lo - SUBLANES, 0)
        x = in_ref[a:hi, c0:c0 + cb]
        acc = x * cw_ref[0:1, c0:c0 + cb]
        for j in range(1, CONV_K):
            acc = pltpu.roll(acc, 1, 0) + x * cw_ref[j:j + 1, c0:c0 + cb]
        acc = acc[lo - a:, :]
        if bias_row is not None:
            acc = acc + cw_ref[bias_row:bias_row + 1, c0:c0 + cb]
        return _silu(acc)

    for rb in range(nch):
        for c0 in range(0, c, cb):
            act[rb * CHUNK:(rb + 1) * CHUNK, c0:c0 + cb] = taps_rolled(rb * CHUNK, (rb + 1) * CHUNK, c0)
    head_rows(tail[...], 0)

    for cl in range(nch):
        @pl.when(kind_ref[i * nch + cl] == 1)
        def _():
            head_rows(cache_ref[cl], cl * CHUNK)

    tail[...] = in_ref[l - SUBLANES:l, :]


def _gdn_kernel(kind_ref, emit_ref, inb_ref, outb_ref,
                qkv_ref, z_ref, sm_ref, cache_ref, s0_ref, cw_ref, parr_ref, parc_ref, nw_ref,
                o_ref, sout_ref,
                tail, act, tmp, state, gcol, beta_s, grow, u_s, w_s, qk_s, qd_s, kdt_s):
    del inb_ref, outb_ref
    i = pl.program_id(0)
    l = qkv_ref.shape[0]
    nch = l // CHUNK
    h_n, dk = GDN_HEADS, GDN_DK
    gh = GDN_GROUP
    ng = h_n // gh
    gr = gh * CHUNK

    _conv_silu_tile(i, kind_ref, qkv_ref, cache_ref, cw_ref, tail, act, tmp, None)

    sm = sm_ref[...]
    beta_s[...] = jax.nn.sigmoid(sm)
    g = parr_ref[0:1, :] * _softplus(sm + parr_ref[1:2, :])
    gcol[...] = _cumsum_rows(g)
    a_t = sm.T[h_n:2 * h_n, :]
    g_t = parc_ref[0:h_n, 0:1] * _softplus(a_t + parc_ref[h_n:2 * h_n, 0:1])
    g_t = _cumsum_lanes(g_t)
    for cl in range(nch):
        for h in range(h_n):
            grow[cl, h // gh:h // gh + 1, (h % gh) * CHUNK:(h % gh + 1) * CHUNK] = (
                g_t[h:h + 1, cl * CHUNK:(cl + 1) * CHUNK])

    sout_ref[1:, :, :, :] = jnp.zeros((nch - 1,) + tuple(sout_ref.shape[1:]), F32)

    ri = lax.broadcasted_iota(I32, (gr, gr), 0)
    ci = lax.broadcasted_iota(I32, (gr, gr), 1)
    same = (ri >> CHUNK_SHIFT) == (ci >> CHUNK_SHIFT)
    causal = same & (ri >= ci)
    strict = same & (ri > ci)

    def row0(c):
        return c * CHUNK if isinstance(c, int) else pl.multiple_of(c * CHUNK, CHUNK)

    def solve_chunk(c):
        base = row0(c)
        gc_blk = gcol[pl.ds(base, CHUNK), :]
        bt_blk = beta_s[pl.ds(base, CHUNK), :]
        for g in range(ng):
            qs, ks, vs, gcs, bts, gls = [], [], [], [], [], []
            for j in range(gh):
                h = g * gh + j
                q = act[pl.ds(base, CHUNK), h * dk:(h + 1) * dk]
                k = act[pl.ds(base, CHUNK), GDN_KD + h * dk:GDN_KD + (h + 1) * dk]
                qs.append(q * lax.rsqrt(jnp.sum(q * q, axis=-1, keepdims=True) + 1e-6) * (dk ** -0.5))
                ks.append(k * lax.rsqrt(jnp.sum(k * k, axis=-1, keepdims=True) + 1e-6))
                vs.append(act[pl.ds(base, CHUNK), 2 * GDN_KD + h * dk:2 * GDN_KD + (h + 1) * dk])
                gcs.append(gc_blk[:, h_n + h:h_n + h + 1])
                bts.append(bt_blk[:, h:h + 1])
                gls.append(jnp.broadcast_to(gc_blk[CHUNK - 1:CHUNK, h_n + h:h_n + h + 1], (CHUNK, 1)))
            q4 = jnp.concatenate(qs, axis=0)
            k4 = jnp.concatenate(ks, axis=0)
            v4 = jnp.concatenate(vs, axis=0)
            gc4 = jnp.concatenate(gcs, axis=0)
            bt4 = jnp.concatenate(bts, axis=0)
            gl4 = jnp.concatenate(gls, axis=0)
            eg4 = jnp.exp(gc4)
            decay = jnp.where(causal, jnp.exp(jnp.minimum(gc4 - grow[c, g:g + 1, :], 0.0)), 0.0)
            kb = k4.astype(BF16)
            xm = jnp.where(strict, -(bt4 * _dot_nt(kb, kb) * decay), 0.0)
            y = jnp.concatenate([v4 * bt4, k4 * (bt4 * eg4)], axis=1)
            p = xm
            for step in range(6):
                if step < GDN_SPLIT_STEPS:
                    y = y + _dot3(p, y)
                else:
                    y = y + _dot(p.astype(BF16), y.astype(BF16))
                if step < GDN_SPLIT_STEPS - 1:
                    p = _dot3(p, p)
                elif step < 5:
                    pb = p.astype(BF16)
                    p = _dot(pb, pb)
            idx = c * ng + g
            u_s[idx] = y[:, :GDN_DV]
            w_s[idx] = y[:, GDN_DV:].astype(BF16)
            qk_s[idx] = (_dot_nt(q4.astype(BF16), kb) * decay).astype(BF16)
            qd_s[idx] = (q4 * eg4).astype(BF16)
            kdt_s[idx] = (k4 * jnp.exp(gl4 - gc4)).T.astype(BF16)

    rblk = lax.broadcasted_iota(I32, (gr, GDN_DV), 0) >> CHUNK_SHIFT

    def recur_chunk(c, solve_next):
        base = row0(c)
        gi = i * nch + c

        @pl.when(kind_ref[gi] == 1)
        def _():
            for h in range(h_n):
                state[h // gh, :, (h % gh) * GDN_DV:(h % gh + 1) * GDN_DV] = s0_ref[c, h]

        last8 = base + CHUNK - SUBLANES
        if not isinstance(c, int):
            last8 = pl.multiple_of(last8, SUBLANES)
        gc_last = gcol[pl.ds(last8, SUBLANES), :][SUBLANES - 1:SUBLANES]
        for g in range(ng):
            idx = c * ng + g
            s4 = state[g]
            w4 = w_s[idx]
            qd4 = qd_s[idx]
            ws, os_ = [], []
            for j in range(gh):
                sb = s4[:, j * GDN_DV:(j + 1) * GDN_DV].astype(BF16)
                lhs = jnp.concatenate([w4[j * CHUNK:(j + 1) * CHUNK], qd4[j * CHUNK:(j + 1) * CHUNK]], axis=0)
                r = _dot(lhs, sb)
                ws.append(r[:CHUNK])
                os_.append(r[CHUNK:])
            v_new = u_s[idx] - jnp.concatenate(ws, axis=0)
            vb = v_new.astype(BF16)
            o4 = jnp.concatenate(os_, axis=0) + _dot(qk_s[idx], vb)
            vbd = jnp.concatenate([jnp.where(rblk == j, v_new, 0.0) for j in range(gh)], axis=1).astype(BF16)
            egl = jnp.concatenate(
                [jnp.broadcast_to(jnp.exp(gc_last[:, h_n + g * gh + j:h_n + g * gh + j + 1]), (1, GDN_DV))
                 for j in range(gh)], axis=1)
            state[g] = s4 * egl + _dot(kdt_s[idx], vbd)
            for j in range(gh):
                h = g * gh + j
                o = o4[j * CHUNK:(j + 1) * CHUNK]
                zz = z_ref[pl.ds(base, CHUNK), h * GDN_DV:(h + 1) * GDN_DV]
                on = o * lax.rsqrt(jnp.mean(o * o, axis=-1, keepdims=True) + EPS) * nw_ref[...] * _silu(zz)
                o_ref[pl.ds(base, CHUNK), h * GDN_DV:(h + 1) * GDN_DV] = on.astype(BF16)

        if solve_next:
            solve_chunk(c + 1)

        @pl.when(emit_ref[gi] >= 0)
        def _():
            for h in range(h_n):
                sout_ref[emit_ref[gi], h] = state[h // gh, :, (h % gh) * GDN_DV:(h % gh + 1) * GDN_DV]

    solve_chunk(0)
    lax.fori_loop(0, nch - 1, lambda c, carry: (recur_chunk(c, True), carry)[1], 0)
    recur_chunk(nch - 1, False)


def _gdn(proj, cache_slots, s0_slots, conv_w, a_log, dt_bias, norm_w, tables, n_out_blk):
    t = proj.shape[0]
    l = MIX_TILE
    nch = TILE_CHUNKS
    kind, emit, in_blk, out_blk = tables
    ng = GDN_HEADS // GDN_GROUP
    gr = GDN_GROUP * CHUNK
    cw = jnp.concatenate([conv_w, jnp.zeros((SUBLANES - CONV_K, GDN_CONV), F32)], axis=0)
    a_neg = -jnp.exp(a_log.astype(F32))
    lane_pad = jnp.zeros((LANES - 2 * GDN_HEADS,), F32)
    parr = jnp.zeros((SUBLANES, LANES), F32)
    parr = parr.at[0].set(jnp.concatenate([jnp.zeros((GDN_HEADS,), F32), a_neg, lane_pad]))
    parr = parr.at[1].set(jnp.concatenate([jnp.zeros((GDN_HEADS,), F32), dt_bias.astype(F32), lane_pad]))
    parc = jnp.broadcast_to(jnp.concatenate([a_neg, dt_bias.astype(F32)])[:, None], (2 * GDN_HEADS, LANES))
    grid_spec = pltpu.PrefetchScalarGridSpec(
        num_scalar_prefetch=4,
        grid=(t // l,),
        in_specs=[
            pl.BlockSpec((l, GDN_CONV), lambda i, *_: (i, 0)),
            pl.BlockSpec((l, GDN_KD), lambda i, *_: (i, P_ZA // GDN_KD)),
            pl.BlockSpec((l, LANES), lambda i, *_: (i, P_SMALL // LANES)),
            pl.BlockSpec((nch, SUBLANES, GDN_CONV), lambda i, k, e, ib, ob: (ib[i], 0, 0)),
            pl.BlockSpec((nch, GDN_HEADS, GDN_DK, GDN_DV), lambda i, k, e, ib, ob: (ib[i], 0, 0, 0)),
            pl.BlockSpec((SUBLANES, GDN_CONV), lambda i, *_: (0, 0)),
            pl.BlockSpec((SUBLANES, LANES), lambda i, *_: (0, 0)),
            pl.BlockSpec((2 * GDN_HEADS, LANES), lambda i, *_: (0, 0)),
            pl.BlockSpec((1, GDN_DV), lambda i, *_: (0, 0)),
        ],
        out_specs=[
            pl.BlockSpec((l, GDN_KD), lambda i, *_: (i, 0)),
            pl.BlockSpec((nch, GDN_HEADS, GDN_DK, GDN_DV), lambda i, k, e, ib, ob: (ob[i], 0, 0, 0)),
        ],
        scratch_shapes=[
            pltpu.VMEM((SUBLANES, GDN_CONV), F32),
            pltpu.VMEM((l, GDN_CONV), F32),
            pltpu.VMEM((2 * SUBLANES, GDN_CONV), F32),
            pltpu.VMEM((ng, GDN_DK, GDN_GROUP * GDN_DV), F32),
            pltpu.VMEM((l, LANES), F32),
            pltpu.VMEM((l, LANES), F32),
            pltpu.VMEM((nch, ng, gr), F32),
            pltpu.VMEM((nch * ng, gr, GDN_DV), F32),
            pltpu.VMEM((nch * ng, gr, GDN_DK), BF16),
            pltpu.VMEM((nch * ng, gr, gr), BF16),
            pltpu.VMEM((nch * ng, gr, GDN_DK), BF16),
            pltpu.VMEM((nch * ng, GDN_DK, gr), BF16),
        ],
    )
    return pl.pallas_call(
        _gdn_kernel,
        grid_spec=grid_spec,
        out_shape=[jax.ShapeDtypeStruct((t, GDN_KD), BF16),
                   jax.ShapeDtypeStruct((n_out_blk * nch, GDN_HEADS, GDN_DK, GDN_DV), F32)],
        compiler_params=_cparams(("arbitrary",)),
        name="gdn_mixer",
    )(jnp.asarray(kind), jnp.asarray(emit), jnp.asarray(in_blk), jnp.asarray(out_blk),
      proj, proj, proj, cache_slots, s0_slots, cw, parr, parc, norm_w.reshape(1, GDN_DV).astype(F32))


SSD_PAIRS = SSD_HEADS // 2
SM_DT = 2 * GDN_HEADS


def _ssd_kernel(kind_ref, emit_ref, inb_ref, outb_ref,
                xbc_ref, z_ref, sm_ref, cache_ref, s0_ref, cw_ref, parr_ref, parc_ref, nw_ref, dsk_ref,
                o_ref, sout_ref,
                tail, act, tmp, state, cscol, dtcol, csrow):
    del inb_ref, outb_ref
    i = pl.program_id(0)
    l = xbc_ref.shape[0]
    nch = l // CHUNK
    hp = 2 * SSD_HEADDIM

    _conv_silu_tile(i, kind_ref, xbc_ref, cache_ref, cw_ref, tail, act, tmp, CONV_K)

    sm = sm_ref[...]
    dtc = _softplus(sm + parr_ref[1:2, :])
    dtcol[...] = dtc
    cscol[...] = _cumsum_rows(parr_ref[0:1, :] * dtc)
    dt_t = _softplus(sm.T[SM_DT:SM_DT + SSD_HEADS, :] + parc_ref[SSD_HEADS:2 * SSD_HEADS, 0:1])
    cs_t = _cumsum_lanes(parc_ref[0:SSD_HEADS, 0:1] * dt_t)
    left_row = (lax.broadcasted_iota(I32, (1, l), 1) & (hp - 1)) < SSD_HEADDIM
    for p in range(SSD_PAIRS):
        ra = cs_t[2 * p:2 * p + 1, :]
        rb = cs_t[2 * p + 1:2 * p + 2, :]
        even = jnp.where(left_row, ra, pltpu.roll(rb, SSD_HEADDIM, 1))
        odd = jnp.where(left_row, pltpu.roll(ra, l - SSD_HEADDIM, 1), rb)
        for cl in range(nch):
            src = even if cl % 2 == 0 else odd
            v0 = (cl // 2) * hp
            csrow[cl, p:p + 1, :] = src[:, v0:v0 + hp]

    sout_ref[1:, :, :, :] = jnp.zeros((nch - 1,) + tuple(sout_ref.shape[1:]), F32)

    ri = lax.broadcasted_iota(I32, (CHUNK, hp), 0)
    li = lax.broadcasted_iota(I32, (CHUNK, hp), 1)
    left = li < SSD_HEADDIM
    causal = ri >= (li & (SSD_HEADDIM - 1))
    top = lax.broadcasted_iota(I32, (hp, 1), 0) < SSD_HEADDIM
    zpad_b = jnp.zeros((CHUNK, SSD_STATE), BF16)
    zpad_f = jnp.zeros((CHUNK, hp), F32)

    def chunk(c, carry):
        base = pl.multiple_of(c * CHUNK, CHUNK)
        gi = i * nch + c

        @pl.when(kind_ref[gi] == 1)
        def _():
            state[...] = s0_ref[c]

        dt_blk = dtcol[pl.ds(base, CHUNK), :]
        cs_blk = cscol[pl.ds(base, CHUNK), :]
        csr = csrow[c]
        ys = []
        for g in range(SSD_NGROUPS):
            b0 = SSD_WIDTH + g * SSD_STATE
            c0 = SSD_WIDTH + SSD_NGROUPS * SSD_STATE + g * SSD_STATE
            bf = act[pl.ds(base, CHUNK), b0:b0 + SSD_STATE]
            cf = act[pl.ds(base, CHUNK), c0:c0 + SSD_STATE]
            bg = bf.astype(BF16)
            cg = cf.astype(BF16)
            cbw = _dot_nt(cg, jnp.concatenate([bg, bg], axis=0))
            for q in range(SSD_PAIRS // SSD_NGROUPS):
                p = g * (SSD_PAIRS // SSD_NGROUPS) + q
                h0 = SM_DT + 2 * p
                xp = act[pl.ds(base, CHUNK), p * hp:(p + 1) * hp]
                dtp = jnp.where(left, dt_blk[:, h0:h0 + 1], dt_blk[:, h0 + 1:h0 + 2])
                csp = jnp.where(left, cs_blk[:, h0:h0 + 1], cs_blk[:, h0 + 1:h0 + 2])
                cl0 = cs_blk[CHUNK - 1:CHUNK, h0:h0 + 1]
                cl1 = cs_blk[CHUNK - 1:CHUNK, h0 + 1:h0 + 2]
                xdt = xp * dtp
                seg = jnp.where(causal, jnp.exp(jnp.minimum(csp - csr[p:p + 1, :], 0.0)), 0.0)
                scores = (cbw * seg).astype(BF16)
                bd = jnp.concatenate([jnp.where(left, xdt, 0.0), jnp.where(left, 0.0, xdt)], axis=0)
                y = _dot(scores, bd.astype(BF16))
                sp = state[p]
                sb = sp.astype(BF16)
                xdt_t = jnp.concatenate([xdt, zpad_f], axis=0).T.astype(BF16)
                yo, st = [], []
                for hh in range(2):
                    ecs = jnp.exp(cs_blk[:, h0 + hh:h0 + hh + 1])
                    cl = cl0 if hh == 0 else cl1
                    dec = jnp.exp(cl - cs_blk[:, h0 + hh:h0 + hh + 1])
                    yo.append(_dot_nt((cf * ecs).astype(BF16), sb))
                    st.append(_dot(xdt_t, jnp.concatenate([(bf * dec).astype(BF16), zpad_b], axis=0)))
                y = y + jnp.where(left, yo[0], yo[1])
                y = y + xp * dsk_ref[:, p * hp:(p + 1) * hp]
                state[p] = sp * jnp.where(top, jnp.exp(cl0), jnp.exp(cl1)) + jnp.where(top, st[0], st[1])
                ys.append(y)
        yf = jnp.concatenate(ys, axis=1)
        yg = yf * _silu(z_ref[pl.ds(base, CHUNK), :].astype(F32))
        out = yg * lax.rsqrt(jnp.mean(yg * yg, axis=-1, keepdims=True) + EPS) * nw_ref[...]
        o_ref[pl.ds(base, CHUNK), :] = out.astype(BF16)

        @pl.when(emit_ref[gi] >= 0)
        def _():
            sout_ref[emit_ref[gi]] = state[...]

        return carry

    lax.fori_loop(0, nch, chunk, 0)


def _ssd(proj, cache_slots, s0_slots, conv_w, conv_b, a_log, dt_bias, d_skip, norm_w, tables, n_out_blk):
    t = proj.shape[0]
    l = MIX_TILE
    nch = TILE_CHUNKS
    hp = 2 * SSD_HEADDIM
    kind, emit, in_blk, out_blk = tables
    cw = jnp.concatenate([conv_w, conv_b[None, :], jnp.zeros((SUBLANES - CONV_K - 1, SSD_CONV), F32)], axis=0)
    a_neg = -jnp.exp(a_log.astype(F32))
    pre = jnp.zeros((SM_DT,), F32)
    post = jnp.zeros((LANES - SM_DT - SSD_HEADS,), F32)
    parr = jnp.zeros((SUBLANES, LANES), F32)
    parr = parr.at[0].set(jnp.concatenate([pre, a_neg, post]))
    parr = parr.at[1].set(jnp.concatenate([pre, dt_bias.astype(F32), post]))
    parc = jnp.broadcast_to(jnp.concatenate([a_neg, dt_bias.astype(F32)])[:, None], (2 * SSD_HEADS, LANES))
    dsk = jnp.repeat(d_skip.astype(F32), SSD_HEADDIM).reshape(1, SSD_WIDTH)
    grid_spec = pltpu.PrefetchScalarGridSpec(
        num_scalar_prefetch=4,
        grid=(t // l,),
        in_specs=[
            pl.BlockSpec((l, SSD_CONV), lambda i, *_: (i, P_XBC // SSD_CONV)),
            pl.BlockSpec((l, SSD_WIDTH), lambda i, *_: (i, P_ZC // SSD_WIDTH)),
            pl.BlockSpec((l, LANES), lambda i, *_: (i, P_SMALL // LANES)),
            pl.BlockSpec((nch, SUBLANES, SSD_CONV), lambda i, k, e, ib, ob: (ib[i], 0, 0)),
            pl.BlockSpec((nch, SSD_PAIRS, hp, SSD_STATE), lambda i, k, e, ib, ob: (ib[i], 0, 0, 0)),
            pl.BlockSpec((SUBLANES, SSD_CONV), lambda i, *_: (0, 0)),
            pl.BlockSpec((SUBLANES, LANES), lambda i, *_: (0, 0)),
            pl.BlockSpec((2 * SSD_HEADS, LANES), lambda i, *_: (0, 0)),
            pl.BlockSpec((1, SSD_WIDTH), lambda i, *_: (0, 0)),
            pl.BlockSpec((1, SSD_WIDTH), lambda i, *_: (0, 0)),
        ],
        out_specs=[
            pl.BlockSpec((l, SSD_WIDTH), lambda i, *_: (i, 0)),
            pl.BlockSpec((nch, SSD_PAIRS, hp, SSD_STATE), lambda i, k, e, ib, ob: (ob[i], 0, 0, 0)),
        ],
        scratch_shapes=[
            pltpu.VMEM((SUBLANES, SSD_CONV), F32),
            pltpu.VMEM((l, SSD_CONV), F32),
            pltpu.VMEM((2 * SUBLANES, SSD_CONV), F32),
            pltpu.VMEM((SSD_PAIRS, hp, SSD_STATE), F32),
            pltpu.VMEM((l, LANES), F32),
            pltpu.VMEM((l, LANES), F32),
            pltpu.VMEM((nch, SSD_PAIRS, hp), F32),
        ],
    )
    return pl.pallas_call(
        _ssd_kernel,
        grid_spec=grid_spec,
        out_shape=[jax.ShapeDtypeStruct((t, SSD_WIDTH), BF16),
                   jax.ShapeDtypeStruct((n_out_blk * nch, SSD_PAIRS, hp, SSD_STATE), F32)],
        compiler_params=_cparams(("arbitrary",)),
        name="ssd_mixer",
    )(jnp.asarray(kind), jnp.asarray(emit), jnp.asarray(in_blk), jnp.asarray(out_blk),
      proj, proj, proj, cache_slots, s0_slots, cw, parr, parc,
      norm_w.reshape(1, SSD_WIDTH).astype(F32), dsk)


S5_SB = 2
S5_LB = 512


def _s5_kernel(kind_ref, u_ref, h0r_ref, h0i_ref, perm_ref, permt_ref, bre_ref, bim_ref, cre_ref, cim_ref,
               ar_ref, ai_ref, dsk_ref, wglu_ref, bglu_ref,
               o_ref, hfr_ref, hfi_ref,
               bur, bui, pre, pim, cr, ci, inr, ini):
    i = pl.program_id(0)
    l = u_ref.shape[0]
    nch = l // CHUNK
    n = S5_N
    usb = S5_WIDTH // S5_SB
    nsb = n // S5_SB

    @pl.when(i == 0)
    def _():
        pre[0:1, :] = ar_ref[...]
        pim[0:1, :] = ai_ref[...]

        def pw(t, carry):
            pr = pre[pl.ds(t - 1, 1), :]
            pi = pim[pl.ds(t - 1, 1), :]
            pre[pl.ds(t, 1), :] = pr * ar_ref[...] - pi * ai_ref[...]
            pim[pl.ds(t, 1), :] = pr * ai_ref[...] + pi * ar_ref[...]
            return carry

        lax.fori_loop(1, CHUNK, pw, 0)
        cr[...] = jnp.zeros((1, n), F32)
        ci[...] = jnp.zeros((1, n), F32)

    up = _dot(perm_ref[...], u_ref[...].astype(BF16)).astype(BF16)
    for sb in range(S5_SB):
        us = up[:, sb * usb:(sb + 1) * usb]
        bur[:, sb * nsb:(sb + 1) * nsb] = _dot(us, bre_ref[sb])
        bui[:, sb * nsb:(sb + 1) * nsb] = _dot(us, bim_ref[sb])

    for c0 in range(0, n, S5_LB):
        a_r = jnp.broadcast_to(ar_ref[:, c0:c0 + S5_LB], (nch, S5_LB))
        a_i = jnp.broadcast_to(ai_ref[:, c0:c0 + S5_LB], (nch, S5_LB))

        def step(t, carry):
            hr, hi = carry
            r0 = pl.multiple_of(t * nch, nch)
            nr = a_r * hr - a_i * hi + bur[pl.ds(r0, nch), c0:c0 + S5_LB]
            ni = a_r * hi + a_i * hr + bui[pl.ds(r0, nch), c0:c0 + S5_LB]
            bur[pl.ds(r0, nch), c0:c0 + S5_LB] = nr
            bui[pl.ds(r0, nch), c0:c0 + S5_LB] = ni
            return nr, ni

        z = jnp.zeros((nch, S5_LB), F32)
        lax.fori_loop(0, CHUNK, step, (z, z))

    a64r = pre[CHUNK - 1:CHUNK, :]
    a64i = pim[CHUNK - 1:CHUNK, :]
    c_r = cr[...]
    c_i = ci[...]
    for s in range(nch):
        start = kind_ref[i * nch + s] == 1
        i_r = jnp.where(start, h0r_ref[s:s + 1, :], c_r)
        i_i = jnp.where(start, h0i_ref[s:s + 1, :], c_i)
        inr[s:s + 1, :] = i_r
        ini[s:s + 1, :] = i_i
        e_r = bur[l - nch + s:l - nch + s + 1, :]
        e_i = bui[l - nch + s:l - nch + s + 1, :]
        c_r = a64r * i_r - a64i * i_i + e_r
        c_i = a64r * i_i + a64i * i_r + e_i
        hfr_ref[s:s + 1, :] = c_r
        hfi_ref[s:s + 1, :] = c_i
    cr[...] = c_r
    ci[...] = c_i

    for c0 in range(0, n, S5_LB):
        n_r = inr[:, c0:c0 + S5_LB]
        n_i = ini[:, c0:c0 + S5_LB]

        def fix(t, carry):
            r0 = pl.multiple_of(t * nch, nch)
            p_r = pre[pl.ds(t, 1), c0:c0 + S5_LB]
            p_i = pim[pl.ds(t, 1), c0:c0 + S5_LB]
            bur[pl.ds(r0, nch), c0:c0 + S5_LB] += p_r * n_r - p_i * n_i
            bui[pl.ds(r0, nch), c0:c0 + S5_LB] += p_r * n_i + p_i * n_r
            return carry

        lax.fori_loop(0, CHUNK, fix, 0)

    ys = []
    for sb in range(S5_SB):
        hr = bur[:, sb * nsb:(sb + 1) * nsb].astype(BF16)
        hi = bui[:, sb * nsb:(sb + 1) * nsb].astype(BF16)
        ys.append(_dot(hr, cre_ref[sb]) - _dot(hi, cim_ref[sb]))
    yp = jnp.concatenate(ys, axis=1)
    y_hi = yp.astype(BF16)
    r1 = yp - y_hi.astype(F32)
    y_mid = r1.astype(BF16)
    y_lo = (r1 - y_mid.astype(F32)).astype(BF16)
    pt = permt_ref[...]
    y = (_dot(pt, y_hi) + _dot(pt, y_mid)) + _dot(pt, y_lo)
    y = y + u_ref[...] * dsk_ref[...]
    y = y * (0.5 * (1.0 + jnp.tanh(math.sqrt(2.0 / math.pi) * (y + 0.044715 * (y * y * y)))))
    out = y * jax.nn.sigmoid(_dot(y.astype(BF16), wglu_ref[...]) + bglu_ref[...])
    o_ref[...] = out.astype(BF16)


def _s5_tables(a_re, a_im, b_re, b_im, c_re, c_im, log_dt):
    a_re, a_im = a_re.astype(F32), a_im.astype(F32)
    dt = jnp.exp(log_dt.astype(F32))[:, None]
    mag = jnp.exp(dt * a_re)
    abar_re, abar_im = mag * jnp.cos(dt * a_im), mag * jnp.sin(dt * a_im)
    den = a_re * a_re + a_im * a_im
    num_re, num_im = abar_re - 1.0, abar_im
    zoh_re = (num_re * a_re + num_im * a_im) / den
    zoh_im = (num_im * a_re - num_re * a_im) / den
    b_re, b_im = b_re.astype(F32), b_im.astype(F32)
    bbar_re = zoh_re[..., None] * b_re - zoh_im[..., None] * b_im
    bbar_im = zoh_re[..., None] * b_im + zoh_im[..., None] * b_re
    gsb = S5_GROUPS // S5_SB
    eye = jnp.eye(gsb, dtype=F32)

    def bblk(b):
        b = b.reshape(S5_SB, gsb, S5_STATE, S5_GROUP)
        return jnp.einsum('sgpc,gh->sgchp', b, eye).reshape(S5_SB, gsb * S5_GROUP, gsb * S5_STATE).astype(BF16)

    def cblk(c):
        c = c.astype(F32).reshape(S5_SB, gsb, S5_GROUP, S5_STATE)
        return jnp.einsum('sgcp,gh->sgphc', c, eye).reshape(S5_SB, gsb * S5_STATE, gsb * S5_GROUP).astype(BF16)

    return (abar_re.reshape(1, S5_N), abar_im.reshape(1, S5_N),
            bblk(bbar_re), bblk(bbar_im), cblk(c_re), cblk(c_im))


def _s5_perm():
    r_new = np.arange(MIX_TILE)
    r_old = (r_new % TILE_CHUNKS) * CHUNK + r_new // TILE_CHUNKS
    p = np.zeros((MIX_TILE, MIX_TILE), np.float32)
    p[r_new, r_old] = 1.0
    return jnp.asarray(p, BF16), jnp.asarray(p.T, BF16)


def _s5(proj, h0r, h0i, tabs, d_skip, w_glu, b_glu, kind):
    t = proj.shape[0]
    l = MIX_TILE
    nch = TILE_CHUNKS
    abr, abi, bre, bim, cre, cim = tabs
    perm, permt = _s5_perm()
    full2 = lambda a: pl.BlockSpec(a.shape, lambda i, *_: (0, 0))
    full3 = lambda a: pl.BlockSpec(a.shape, lambda i, *_: (0, 0, 0))
    dsk = d_skip.astype(F32).reshape(1, S5_WIDTH)
    wg = w_glu.astype(BF16)
    bg = b_glu.astype(F32).reshape(1, S5_WIDTH)
    grid_spec = pltpu.PrefetchScalarGridSpec(
        num_scalar_prefetch=1,
        grid=(t // l,),
        in_specs=[
            pl.BlockSpec((l, S5_WIDTH), lambda i, *_: (i, P_UB // S5_WIDTH)),
            pl.BlockSpec((nch, S5_N), lambda i, *_: (i, 0)),
            pl.BlockSpec((nch, S5_N), lambda i, *_: (i, 0)),
            full2(perm), full2(permt), full3(bre), full3(bim), full3(cre), full3(cim),
            full2(abr), full2(abi), full2(dsk), full2(wg), full2(bg),
        ],
        out_specs=[
            pl.BlockSpec((l, S5_WIDTH), lambda i, *_: (i, 0)),
            pl.BlockSpec((nch, S5_N), lambda i, *_: (i, 0)),
            pl.BlockSpec((nch, S5_N), lambda i, *_: (i, 0)),
        ],
        scratch_shapes=[
            pltpu.VMEM((l, S5_N), F32), pltpu.VMEM((l, S5_N), F32),
            pltpu.VMEM((CHUNK, S5_N), F32), pltpu.VMEM((CHUNK, S5_N), F32),
            pltpu.VMEM((1, S5_N), F32), pltpu.VMEM((1, S5_N), F32),
            pltpu.VMEM((nch, S5_N), F32), pltpu.VMEM((nch, S5_N), F32),
        ],
    )
    nseg = t // CHUNK
    return pl.pallas_call(
        _s5_kernel,
        grid_spec=grid_spec,
        out_shape=[jax.ShapeDtypeStruct((t, S5_WIDTH), BF16),
                   jax.ShapeDtypeStruct((nseg, S5_N), F32),
                   jax.ShapeDtypeStruct((nseg, S5_N), F32)],
        compiler_params=_cparams(("arbitrary",)),
        name="s5_mixer",
    )(jnp.asarray(kind), proj, h0r, h0i, perm, permt, bre, bim, cre, cim, abr, abi, dsk, wg, bg)


MIXOUT_TM = 512
R_GRP = 0
R_EXP = SUBLANES
NEG_BIG = -1e30


def _mixout_kernel(oa_ref, ob_ref, oc_ref, w_ref, x_ref, nw_ref, wr_ref, rb_ref, x1_ref, h2_ref, lg_ref):
    acc = _dot(oa_ref[...], w_ref[0:GDN_KD, :])
    acc = acc + _dot(ob_ref[...], w_ref[GDN_KD:GDN_KD + S5_WIDTH, :])
    acc = acc + _dot(oc_ref[...], w_ref[GDN_KD + S5_WIDTH:, :])
    x1 = x_ref[...] + acc
    x1_ref[...] = x1
    h = x1 * lax.rsqrt(jnp.mean(x1 * x1, axis=-1, keepdims=True) + EPS) * nw_ref[...]
    hb = h.astype(BF16)
    h2_ref[...] = _pack_bf16_pairs(hb)
    lg_ref[...] = _dot(hb, wr_ref[...]) + rb_ref[...]


def _mixout(oa, ob, oc, w_out, x, nw, wr, rb):
    t, d = x.shape
    tm = MIXOUT_TM
    row = lambda w: pl.BlockSpec((tm, w), lambda i: (i, 0))
    full = lambda a: pl.BlockSpec(a.shape, lambda i: (0, 0))
    return pl.pallas_call(
        _mixout_kernel,
        grid=(t // tm,),
        in_specs=[row(GDN_KD), row(S5_WIDTH), row(SSD_WIDTH), full(w_out), row(d), full(nw), full(wr), full(rb)],
        out_specs=[row(d), row(d // 2), row(LANES)],
        out_shape=[jax.ShapeDtypeStruct((t, d), F32), jax.ShapeDtypeStruct((t, d // 2), U32),
                   jax.ShapeDtypeStruct((t, LANES), F32)],
        compiler_params=_cparams(("parallel",)),
        name="mix_out",
    )(oa, ob, oc, w_out, x, nw, wr, rb)


def _router_weights(rg_w, rg_b, re_w, re_b):
    d = rg_w.shape[0]
    wr = jnp.concatenate([rg_w, jnp.zeros((d, R_EXP - N_GROUPS), F32), re_w,
                          jnp.zeros((d, LANES - R_EXP - N_EXPERTS), F32)], axis=1).astype(BF16)
    rb = jnp.concatenate([rg_b.astype(F32), jnp.full((R_EXP - N_GROUPS,), NEG_BIG, F32), re_b.astype(F32),
                          jnp.zeros((LANES - R_EXP - N_EXPERTS,), F32)]).reshape(1, LANES)
    return wr, rb


ROUTE_TM = 512


def _router_kernel(lg_ref, tri_ref, idx_ref, gate_ref, cnt_ref, run):
    i = pl.program_id(0)
    tm = lg_ref.shape[0]

    @pl.when(i == 0)
    def _():
        run[...] = jnp.zeros(run.shape, F32)

    lt = lg_ref[...].T
    row8 = lax.broadcasted_iota(I32, (SUBLANES, tm), 0)
    grp = lt[R_GRP:R_GRP + SUBLANES, :]
    gm = jnp.max(grp, axis=0, keepdims=True)
    gp_top = 1.0 / jnp.sum(jnp.exp(grp - gm), axis=0, keepdims=True)
    g_top = jnp.min(jnp.where(grp == gm, row8, SUBLANES), axis=0, keepdims=True)
    ing = jnp.zeros((EPG, tm), F32)
    for g in range(N_GROUPS):
        ing = jnp.where(g_top == g, lt[R_EXP + g * EPG:R_EXP + (g + 1) * EPG, :], ing)
    em = jnp.max(ing, axis=0, keepdims=True)
    ee = jnp.exp(ing - em)
    p = ee / jnp.sum(ee, axis=0, keepdims=True)
    v1 = jnp.max(p, axis=0, keepdims=True)
    i1 = jnp.min(jnp.where(p == v1, row8, EPG), axis=0, keepdims=True)
    p2 = jnp.where(row8 == i1, -1.0, p)
    v2 = jnp.max(p2, axis=0, keepdims=True)
    i2 = jnp.min(jnp.where(p2 == v2, row8, EPG), axis=0, keepdims=True)
    den = v1 + v2
    gate1 = gp_top * v1 / den
    gate2 = gp_top * v2 / den
    e1 = g_top * EPG + i1
    e2 = g_top * EPG + i2

    erow = lax.broadcasted_iota(I32, (N_EXPERTS, tm), 0)
    hit1 = erow == e1
    hit2 = erow == e2
    oh = jnp.where(hit1 | hit2, 1.0, 0.0)
    before = _dot(oh.astype(BF16), tri_ref[...]) + run[:, 0:1]
    rank1 = jnp.sum(jnp.where(hit1, before, 0.0), axis=0, keepdims=True).astype(I32)
    rank2 = jnp.sum(jnp.where(hit2, before, 0.0), axis=0, keepdims=True).astype(I32)
    run[...] = run[...] + jnp.sum(oh, axis=1, keepdims=True)
    cnt_ref[...] = run[...].astype(I32)

    zi = jnp.zeros((SUBLANES - 4, tm), I32)
    idx_ref[...] = jnp.concatenate([e1, e2, rank1, rank2, zi], axis=0)
    r128 = lax.broadcasted_iota(I32, (LANES, tm), 0)
    gt = jnp.where(r128 == 0, gate1, jnp.where(r128 == 1, gate2, 0.0))
    gate_ref[...] = gt.T


def _router(logits):
    t = logits.shape[0]
    tm = ROUTE_TM
    tri = jnp.asarray(np.triu(np.ones((tm, tm), np.float32), 1), BF16)
    return pl.pallas_call(
        _router_kernel,
        grid=(t // tm,),
        in_specs=[pl.BlockSpec((tm, LANES), lambda i: (i, 0)),
                  pl.BlockSpec((tm, tm), lambda i: (0, 0))],
        out_specs=[pl.BlockSpec((SUBLANES, tm), lambda i: (0, i)),
                   pl.BlockSpec((tm, LANES), lambda i: (i, 0)),
                   pl.BlockSpec((N_EXPERTS, LANES), lambda i: (0, 0))],
        out_shape=[jax.ShapeDtypeStruct((SUBLANES, t), I32),
                   jax.ShapeDtypeStruct((t, LANES), F32),
                   jax.ShapeDtypeStruct((N_EXPERTS, LANES), I32)],
        scratch_shapes=[pltpu.VMEM((N_EXPERTS, LANES), F32)],
        compiler_params=_cparams(("arbitrary",)),
        name="router",
    )(logits, tri)


MOE_BLOCK = 256
MOE_BLOCK_SHIFT = 8
DMA_UNROLL = 8
DISPATCH_TM = 512
COMBINE_TM = 256


def _dispatch_kernel(pstart_ref, cnt_ref, nv_ref, idx_ref, h2_ref, xs_ref, zbuf, sem):
    i = pl.program_id(0)
    tm = idx_ref.shape[1]
    nb = xs_ref.shape[0] // MOE_BLOCK

    @pl.when(i == 0)
    def _():
        zbuf[...] = jnp.zeros(zbuf.shape, zbuf.dtype)

        def pad_copy(e, r):
            return pltpu.make_async_copy(zbuf.at[pl.ds(0, 1)], xs_ref.at[pl.ds(pstart_ref[e] + r, 1)], sem)

        def per_expert(e, carry):
            n = cnt_ref[e]
            padded = ((n + MOE_BLOCK - 1) >> MOE_BLOCK_SHIFT) << MOE_BLOCK_SHIFT
            lax.fori_loop(n, padded, lambda r, c: (pad_copy(e, r).start(), c)[1], 0)
            lax.fori_loop(n, padded, lambda r, c: (pad_copy(e, r).wait(), c)[1], 0)
            return carry

        lax.fori_loop(0, N_EXPERTS, per_expert, 0)

        def blk_copy(b):
            return pltpu.make_async_copy(zbuf, xs_ref.at[pl.ds(b * MOE_BLOCK, MOE_BLOCK)], sem)

        lax.fori_loop(nv_ref[0], nb, lambda b, c: (blk_copy(b).start(), c)[1], 0)
        lax.fori_loop(nv_ref[0], nb, lambda b, c: (blk_copy(b).wait(), c)[1], 0)

    def copy(t, k):
        slot = pstart_ref[idx_ref[k, t]] + idx_ref[2 + k, t]
        return pltpu.make_async_copy(h2_ref.at[pl.ds(t, 1)], xs_ref.at[pl.ds(slot, 1)], sem)

    def issue(t, carry):
        copy(t, 0).start(priority=0)
        copy(t, 1).start(priority=1)
        return carry

    def drain(t, carry):
        copy(t, 0).wait()
        copy(t, 1).wait()
        return carry

    lax.fori_loop(0, tm, issue, 0, unroll=DMA_UNROLL)
    lax.fori_loop(0, tm, drain, 0, unroll=DMA_UNROLL)


def _dispatch(pad_start, counts, n_valid, idx, h2, n_slots):
    t, d = h2.shape
    tm = DISPATCH_TM
    grid_spec = pltpu.PrefetchScalarGridSpec(
        num_scalar_prefetch=3,
        grid=(t // tm,),
        in_specs=[pl.BlockSpec((SUBLANES, tm), lambda i, *_: (0, i), memory_space=pltpu.SMEM),
                  pl.BlockSpec((tm, d), lambda i, *_: (i, 0))],
        out_specs=pl.BlockSpec(memory_space=pl.ANY),
        scratch_shapes=[pltpu.VMEM((MOE_BLOCK, d), h2.dtype), pltpu.SemaphoreType.DMA(())],
    )
    return pl.pallas_call(
        _dispatch_kernel,
        grid_spec=grid_spec,
        out_shape=jax.ShapeDtypeStruct((n_slots, d), h2.dtype),
        compiler_params=_cparams(("arbitrary",)),
        name="moe_dispatch",
    )(pad_start, counts, n_valid, idx, h2)


def _expert_kernel(be_ref, nv_ref, xs_ref, wg_ref, wu_ref, wd_ref, ys_ref, wg_s, wu_s, wd_s):
    b = pl.program_id(0)
    valid = b < nv_ref[0]

    @pl.when(valid & ((b == 0) | (be_ref[b] != be_ref[jnp.maximum(b - 1, 0)])))
    def _():
        wg_s[...] = wg_ref[0].astype(BF16)
        wu_s[...] = wu_ref[0].astype(BF16)
        wd_s[...] = wd_ref[0].astype(BF16)

    @pl.when(valid)
    def _():
        x_lo, x_hi = _unpack_bf16_pairs(xs_ref[...])
        kh = wg_s.shape[0] // 2
        g = _dot(x_lo, wg_s[0:kh, :]) + _dot(x_hi, wg_s[kh:, :])
        u = _dot(x_lo, wu_s[0:kh, :]) + _dot(x_hi, wu_s[kh:, :])
        h = (_silu(g) * u).astype(BF16)
        ys_ref[...] = _dot(h, wd_s[...])

    @pl.when(jnp.logical_not(valid))
    def _():
        ys_ref[...] = jnp.zeros(ys_ref.shape, F32)


def _experts(block_expert, n_valid, xs, wg, wu, wd, layer):
    n_slots, dp = xs.shape
    d, de = wg.shape[2], wg.shape[3]
    assert d == 2 * dp
    nb = n_slots // MOE_BLOCK
    blk = lambda b, be, nv: (jnp.minimum(b, nv[0] - 1), 0)
    wsel = lambda b, be, nv: (layer, be[jnp.minimum(b, nv[0] - 1)], 0, 0)
    grid_spec = pltpu.PrefetchScalarGridSpec(
        num_scalar_prefetch=2,
        grid=(nb,),
        in_specs=[pl.BlockSpec((MOE_BLOCK, dp), blk),
                  pl.BlockSpec((None, 1, d, de), wsel),
                  pl.BlockSpec((None, 1, d, de), wsel),
                  pl.BlockSpec((None, 1, de, d), wsel)],
        out_specs=pl.BlockSpec((MOE_BLOCK, d), lambda b, be, nv: (b, 0)),
        scratch_shapes=[pltpu.VMEM((d, de), BF16), pltpu.VMEM((d, de), BF16), pltpu.VMEM((de, d), BF16)],
    )
    return pl.pallas_call(
        _expert_kernel,
        grid_spec=grid_spec,
        out_shape=jax.ShapeDtypeStruct((n_slots, d), F32),
        compiler_params=_cparams(("arbitrary",)),
        name="moe_experts",
    )(block_expert, n_valid, xs, wg, wu, wd)


def _combine_kernel(pstart_ref, idx_ref, x1_ref, gate_ref, ys_ref, nw_ref, *rest, n_first):
    if n_first is None:
        out_ref, ybuf, sem = rest
    else:
        out_a_ref, out_b_ref, ybuf, sem = rest
    tm = x1_ref.shape[0]

    def copy(t, k):
        slot = pstart_ref[idx_ref[k, t]] + idx_ref[2 + k, t]
        return pltpu.make_async_copy(ys_ref.at[pl.ds(slot, 1)], ybuf.at[k, pl.ds(t, 1)], sem)

    def issue(t, carry):
        copy(t, 0).start(priority=0)
        copy(t, 1).start(priority=1)
        return carry

    def drain(t, carry):
        copy(t, 0).wait()
        copy(t, 1).wait()
        return carry

    lax.fori_loop(0, tm, issue, 0, unroll=DMA_UNROLL)
    lax.fori_loop(0, tm, drain, 0, unroll=DMA_UNROLL)
    g = gate_ref[...]
    y = ybuf[0] * g[:, 0:1] + ybuf[1] * g[:, 1:2]
    x2 = x1_ref[...] + y
    if n_first is None:
        out_ref[...] = x2
    else:
        x2 = x2 * lax.rsqrt(jnp.mean(x2 * x2, axis=-1, keepdims=True) + EPS) * nw_ref[...]
        i = pl.program_id(0)

        @pl.when(i < n_first)
        def _():
            out_a_ref[...] = x2

        @pl.when(i >= n_first)
        def _():
            out_b_ref[...] = x2


def _combine(pad_start, idx, x1, gates, ys, nw, t_first):
    t, d = x1.shape
    tm = COMBINE_TM
    row = pl.BlockSpec((tm, d), lambda i, *_: (i, 0))
    if t_first is None:
        n_first = None
        out_specs = row
        out_shape = jax.ShapeDtypeStruct((t, d), F32)
    else:
        assert t_first % tm == 0
        n_first = t_first // tm
        out_specs = [pl.BlockSpec((tm, d), lambda i, *_: (jnp.minimum(i, n_first - 1), 0)),
                     pl.BlockSpec((tm, d), lambda i, *_: (jnp.maximum(i - n_first, 0), 0))]
        out_shape = [jax.ShapeDtypeStruct((t_first, d), F32), jax.ShapeDtypeStruct((t - t_first, d), F32)]
    grid_spec = pltpu.PrefetchScalarGridSpec(
        num_scalar_prefetch=1,
        grid=(t // tm,),
        in_specs=[pl.BlockSpec((SUBLANES, tm), lambda i, *_: (0, i), memory_space=pltpu.SMEM),
                  row,
                  pl.BlockSpec((tm, LANES), lambda i, *_: (i, 0)),
                  pl.BlockSpec(memory_space=pl.ANY),
                  pl.BlockSpec((1, d), lambda i, *_: (0, 0))],
        out_specs=out_specs,
        scratch_shapes=[pltpu.VMEM((2, tm, d), F32), pltpu.SemaphoreType.DMA(())],
    )
    return pl.pallas_call(
        functools.partial(_combine_kernel, n_first=n_first),
        grid_spec=grid_spec,
        out_shape=out_shape,
        compiler_params=_cparams(("arbitrary",)),
        name="moe_combine",
    )(pad_start, idx, x1, gates, ys, nw)


def _moe(x1, h2, logits, wg, wu, wd, layer, norm_final, t_first):
    t, d = x1.shape
    idx, gates, cnt = _router(logits)
    counts = cnt[:, 0]
    padded = ((counts + MOE_BLOCK - 1) >> MOE_BLOCK_SHIFT) << MOE_BLOCK_SHIFT
    pad_end = jnp.cumsum(padded)
    pad_start = (pad_end - padded).astype(I32)
    nb = (2 * t + N_EXPERTS * (MOE_BLOCK - 1) + MOE_BLOCK - 1) // MOE_BLOCK
    n_valid = (pad_end[-1] >> MOE_BLOCK_SHIFT).astype(I32).reshape(1)
    starts = jnp.arange(nb, dtype=I32) * MOE_BLOCK
    block_expert = jnp.minimum(jnp.sum((pad_end[None, :] <= starts[:, None]).astype(I32), axis=1),
                               N_EXPERTS - 1).astype(I32)
    xs = _dispatch(pad_start, counts, n_valid, idx, h2, nb * MOE_BLOCK)
    ys = _experts(block_expert, n_valid, xs, wg, wu, wd, layer)
    return _combine(pad_start, idx, x1, gates, ys, norm_final, t_first)


def _stream_ends(nbp, seq, nbs, dseq):
    ends = [(b + 1) * seq for b in range(nbp)] + [nbp * seq + (s + 1) * dseq for s in range(nbs)]
    return np.asarray(ends)


def kernel(x_prompt, x_sample, cache_conv_gdn, state_gdn, state_s5, cache_conv_ssd, state_ssd, norm_mix, w_in, gdn_conv_w, gdn_a_log, gdn_dt_bias, gdn_norm, s5_a_re, s5_a_im, s5_b_re, s5_b_im, s5_c_re, s5_c_im, s5_log_dt, s5_d, s5_w_glu, s5_b_glu, ssd_conv_w, ssd_conv_b, ssd_a_log, ssd_dt_bias, ssd_d, ssd_norm, w_out, norm_ffn, router_group_w, router_group_b, router_expert_w, router_expert_b, expert_w_gate, expert_w_up, expert_w_down, norm_final):
    nbp, seq, d = x_prompt.shape
    nbs, dseq, _ = x_sample.shape
    depth = w_in.shape[0]
    tp = nbp * seq
    t = tp + nbs * dseq
    x = jnp.concatenate([x_prompt.reshape(tp, d), x_sample.reshape(nbs * dseq, d)], axis=0)

    tables = _chunk_tables(nbp, seq, nbs, dseq)
    kind = tables[0]
    n_out_blk = nbp + nbs // TILE_CHUNKS
    ends = _stream_ends(nbp, seq, nbs, dseq)
    tail_rows = (ends[:, None] + np.arange(-(CONV_K - 1), 0)[None, :]).reshape(-1)
    end_seg = ends // CHUNK - 1
    ncp = tp // CHUNK
    state_rows = np.concatenate([np.arange(nbp) * TILE_CHUNKS, nbp * TILE_CHUNKS + np.arange(nbs)])

    new_conv_gdn, new_gdn, new_s5, new_conv_ssd, new_ssd = [], [], [], [], []
    for l in range(depth):
        proj = _proj(x, norm_mix[l].reshape(1, d).astype(F32), _rearrange_w_in(w_in[l]))

        oa, sg = _gdn(proj, _conv_cache_slots(cache_conv_gdn[l].astype(F32)),
                      _init_slots(state_gdn[l].astype(F32)), gdn_conv_w[l].astype(F32), gdn_a_log[l],
                      gdn_dt_bias[l], gdn_norm[l], tables, n_out_blk)

        h0 = state_s5[l].astype(F32).reshape(nbs, S5_N, 2)
        zeros_p = jnp.zeros((ncp, S5_N), F32)
        ob, hfr, hfi = _s5(proj, jnp.concatenate([zeros_p, h0[..., 0]], axis=0),
                           jnp.concatenate([zeros_p, h0[..., 1]], axis=0),
                           _s5_tables(s5_a_re[l], s5_a_im[l], s5_b_re[l], s5_b_im[l], s5_c_re[l], s5_c_im[l],
                                      s5_log_dt[l]),
                           s5_d[l], s5_w_glu[l], s5_b_glu[l], kind)

        oc, ss = _ssd(proj, _conv_cache_slots(cache_conv_ssd[l].astype(F32)),
                      _init_slots(state_ssd[l].astype(F32).reshape(nbs, SSD_PAIRS, 2 * SSD_HEADDIM, SSD_STATE)),
                      ssd_conv_w[l].astype(F32), ssd_conv_b[l].astype(F32), ssd_a_log[l], ssd_dt_bias[l],
                      ssd_d[l], ssd_norm[l], tables, n_out_blk)

        wr, rb = _router_weights(router_group_w[l].astype(F32), router_group_b[l],
                                 router_expert_w[l].astype(F32), router_expert_b[l])
        x1, h2, logits = _mixout(oa, ob, oc, w_out[l].astype(BF16), x,
                                 norm_ffn[l].reshape(1, d).astype(F32), wr, rb)
        x = _moe(x1, h2, logits, expert_w_gate, expert_w_up, expert_w_down, l,
                 norm_final.reshape(1, d).astype(F32), tp if l == depth - 1 else None)

        tails = proj[tail_rows]
        new_conv_gdn.append(tails[:, P_QKV:P_QKV + GDN_CONV].reshape(nbp + nbs, CONV_K - 1, GDN_CONV))
        new_conv_ssd.append(tails[:, P_XBC:P_XBC + SSD_CONV].reshape(nbp + nbs, CONV_K - 1, SSD_CONV))
        new_gdn.append(sg[state_rows])
        new_ssd.append(ss[state_rows].reshape(nbp + nbs, SSD_HEADS, SSD_HEADDIM, SSD_STATE))
        new_s5.append(jnp.stack([hfr[end_seg], hfi[end_seg]], axis=-1)
                      .reshape(nbp + nbs, S5_GROUPS, S5_STATE, 2))

    def split(parts):
        a = jnp.stack(parts)
        return a[:, :nbp], a[:, nbp:]

    cg_p, cg_s = split(new_conv_gdn)
    sg_p, sg_s = split(new_gdn)
    s5_p, s5_s = split(new_s5)
    cs_p, cs_s = split(new_conv_ssd)
    ss_p, ss_s = split(new_ssd)
    y_prompt = x[0].reshape(nbp, seq, d)
    y_sample = x[1].reshape(nbs, dseq, d)
    return (y_prompt, y_sample, cg_p, sg_p, s5_p, cs_p, ss_p, cg_s, sg_s, s5_s, cs_s, ss_s)
```

```python
import functools
import math

import numpy as np
import jax
import jax.numpy as jnp
from jax import lax
from jax.experimental import pallas as pl
from jax.experimental.pallas import tpu as pltpu

F32 = jnp.float32
BF16 = jnp.bfloat16
I32 = jnp.int32
U32 = jnp.uint32

EPS = 1e-6
CHUNK = 64
CHUNK_SHIFT = 6
CONV_K = 4
LANES = 128
SUBLANES = 8
VMEM_LIMIT = 56 * 1024 * 1024

GDN_HEADS = 8
GDN_DK = 128
GDN_DV = 128
GDN_KD = GDN_HEADS * GDN_DK
GDN_CONV = 3 * GDN_KD
GDN_GROUP = 4
GDN_SPLIT_STEPS = 2
S5_WIDTH = 512
S5_GROUPS = 32
S5_GROUP = 16
S5_STATE = 64
S5_N = S5_GROUPS * S5_STATE
SSD_WIDTH = 512
SSD_HEADS = 8
SSD_HEADDIM = 64
SSD_NGROUPS = 2
SSD_STATE = 128
SSD_CONV = SSD_WIDTH + 2 * SSD_NGROUPS * SSD_STATE
N_GROUPS = 4
EPG = 8
N_EXPERTS = 32

MIX_TILE = 512
TILE_CHUNKS = MIX_TILE // CHUNK


def _cparams(sem, vmem=VMEM_LIMIT):
    return pltpu.CompilerParams(dimension_semantics=sem, vmem_limit_bytes=vmem)


def _silu(x):
    return x * jax.nn.sigmoid(x)


def _softplus(x):
    return jnp.maximum(x, 0.0) + jnp.log1p(jnp.exp(-jnp.abs(x)))


def _dot(a, b):
    return jnp.dot(a, b, preferred_element_type=F32)


def _dot3(a, b):
    ah = a.astype(BF16)
    bh = b.astype(BF16)
    al = (a - ah.astype(F32)).astype(BF16)
    bl = (b - bh.astype(F32)).astype(BF16)
    return _dot(ah, bh) + (_dot(ah, bl) + _dot(al, bh))


def _pack_bf16_pairs(xb):
    n = xb.shape[1] // 2
    lo = lax.bitcast_convert_type(xb[:, :n].astype(F32), U32)
    hi = lax.bitcast_convert_type(xb[:, n:].astype(F32), U32)
    return (hi & jnp.uint32(0xFFFF0000)) | (lo >> 16)


def _unpack_bf16_pairs(w):
    lo = lax.bitcast_convert_type(w << 16, F32).astype(BF16)
    hi = lax.bitcast_convert_type(w & jnp.uint32(0xFFFF0000), F32).astype(BF16)
    return lo, hi


def _dot_nt(a, b):
    return lax.dot_general(a, b, (((1,), (1,)), ((), ())), preferred_element_type=F32)


def _dot_tn(a, b):
    return lax.dot_general(a, b, (((0,), (0,)), ((), ())), preferred_element_type=F32)


def _cumsum_rows(x):
    row = lax.broadcasted_iota(I32, x.shape, 0) & (CHUNK - 1)
    k = 1
    while k < CHUNK:
        x = x + jnp.where(row >= k, pltpu.roll(x, k, 0), 0.0)
        k *= 2
    return x


def _cumsum_lanes(x):
    lane = lax.broadcasted_iota(I32, x.shape, 1) & (CHUNK - 1)
    k = 1
    while k < CHUNK:
        x = x + jnp.where(lane >= k, pltpu.roll(x, k, 1), 0.0)
        k *= 2
    return x


PROJ_TM = 1024
PROJ_TN = 1280
P_QKV, P_ZA, P_UB, P_ZC, P_XBC = 0, 3072, 4096, 4608, 5120
P_MAIN = 6144
P_TOTAL = 6400
P_SMALL = P_TOTAL - LANES


def _proj_kernel(x_ref, nw_ref, w_ref, o_ref, h_scr):
    @pl.when(pl.program_id(1) == 0)
    def _():
        x = x_ref[...]
        ms = jnp.mean(x * x, axis=-1, keepdims=True)
        h_scr[...] = (x * lax.rsqrt(ms + EPS) * nw_ref[...]).astype(BF16)

    o_ref[...] = _dot(h_scr[...], w_ref[...])


def _proj(x, nw, w):
    t, d = x.shape
    n = w.shape[1]
    tm = min(PROJ_TM, t)
    assert t % tm == 0 and n % PROJ_TN == 0
    return pl.pallas_call(
        _proj_kernel,
        grid=(t // tm, n // PROJ_TN),
        in_specs=[pl.BlockSpec((tm, d), lambda i, j: (i, 0)),
                  pl.BlockSpec((1, d), lambda i, j: (0, 0)),
                  pl.BlockSpec((d, PROJ_TN), lambda i, j: (0, j))],
        out_specs=pl.BlockSpec((tm, PROJ_TN), lambda i, j: (i, j)),
        out_shape=jax.ShapeDtypeStruct((t, n), F32),
        scratch_shapes=[pltpu.VMEM((tm, d), BF16)],
        compiler_params=_cparams(("parallel", "arbitrary")),
        name="proj_in",
    )(x, nw, w)


def _rearrange_w_in(w_in):
    d = w_in.shape[0]
    off_za = GDN_CONV
    off_ba = off_za + GDN_KD
    off_s5 = off_ba + 2 * GDN_HEADS
    off_zc = off_s5 + S5_WIDTH
    off_xbc = off_zc + SSD_WIDTH
    off_dt = off_xbc + SSD_CONV
    small = jnp.concatenate([w_in[:, off_ba:off_s5], w_in[:, off_dt:],
                             jnp.zeros((d, LANES - 3 * GDN_HEADS), w_in.dtype)], axis=1)
    w = jnp.concatenate([w_in[:, :off_ba], w_in[:, off_s5:off_dt],
                         jnp.zeros((d, P_TOTAL - P_MAIN - LANES), w_in.dtype), small], axis=1)
    return w.astype(BF16)


def _chunk_tables(nbp, seq, nbs, dseq):
    assert seq % MIX_TILE == 0 and dseq == CHUNK and (nbs * dseq) % MIX_TILE == 0
    cps = seq // CHUNK
    ncp = nbp * cps
    nc = ncp + nbs
    kind = np.zeros((nc,), np.int32)
    emit = np.full((nc,), -1, np.int32)
    for c in range(nc):
        if c < ncp:
            kind[c] = 1 if c % cps == 0 else 0
            if c % TILE_CHUNKS == TILE_CHUNKS - 1:
                emit[c] = 0
        else:
            kind[c] = 1
            emit[c] = (c - ncp) % TILE_CHUNKS
    ntp = ncp // TILE_CHUNKS
    nts = nbs // TILE_CHUNKS
    in_blk = np.concatenate([np.zeros((ntp,), np.int32), 1 + np.arange(nts, dtype=np.int32)])
    out_blk = np.concatenate([np.arange(ntp, dtype=np.int32) // (cps // TILE_CHUNKS),
                              nbp + np.arange(nts, dtype=np.int32)])
    return kind, emit, in_blk, out_blk


def _init_slots(x):
    return jnp.concatenate([jnp.zeros((TILE_CHUNKS,) + x.shape[1:], x.dtype), x], axis=0)


def _conv_cache_slots(cache):
    nbs, k1, c = cache.shape
    padded = jnp.concatenate([jnp.zeros((nbs, SUBLANES - k1, c), cache.dtype), cache], axis=1)
    return _init_slots(padded)


def _conv_silu_tile(i, kind_ref, in_ref, cache_ref, cw_ref, tail, act, tmp, bias_row):
    l, c = in_ref.shape
    cb = 512
    k1 = CONV_K - 1
    nch = l // CHUNK

    @pl.when(i == 0)
    def _():
        tail[...] = jnp.zeros(tail.shape, F32)

    def taps(src, lo, hi, c0):
        acc = src[lo - k1:hi - k1, c0:c0 + cb] * cw_ref[0:1, c0:c0 + cb]
        for j in range(1, CONV_K):
            acc = acc + src[lo - k1 + j:hi - k1 + j, c0:c0 + cb] * cw_ref[j:j + 1, c0:c0 + cb]
        if bias_row is not None:
            acc = acc + cw_ref[bias_row:bias_row + 1, c0:c0 + cb]
        return _silu(acc)

    def head_rows(prev, r0):
        tmp[0:SUBLANES, :] = prev
        tmp[SUBLANES:2 * SUBLANES, :] = in_ref[r0:r0 + SUBLANES, :]
        for c0 in range(0, c, cb):
            act[r0:r0 + SUBLANES, c0:c0 + cb] = taps(tmp, SUBLANES, 2 * SUBLANES, c0)

    def taps_rolled(lo, hi, c0):
        a = max(lo - SUBLANES, 0)
        x = in_ref[a:hi, c0:c0 + cb]
        acc = x * cw_ref[0:1, c0:c0 + cb]
        for j in range(1, CONV_K):
            acc = pltpu.roll(acc, 1, 0) + x * cw_ref[j:j + 1, c0:c0 + cb]
        acc = acc[lo - a:, :]
        if bias_row is not None:
            acc = acc + cw_ref[bias_row:bias_row + 1, c0:c0 + cb]
        return _silu(acc)

    for rb in range(nch):
        for c0 in range(0, c, cb):
            act[rb * CHUNK:(rb + 1) * CHUNK, c0:c0 + cb] = taps_rolled(rb * CHUNK, (rb + 1) * CHUNK, c0)
    head_rows(tail[...], 0)

    for cl in range(nch):
        @pl.when(kind_ref[i * nch + cl] == 1)
        def _():
            head_rows(cache_ref[cl], cl * CHUNK)

    tail[...] = in_ref[l - SUBLANES:l, :]


def _gdn_kernel(kind_ref, emit_ref, inb_ref, outb_ref,
                qkv_ref, z_ref, sm_ref, cache_ref, s0_ref, cw_ref, parr_ref, parc_ref, nw_ref,
                o_ref, sout_ref,
                tail, act, tmp, state, gcol, beta_s, grow, u_s, w_s, qk_s, qd_s, kdt_s):
    del inb_ref, outb_ref
    i = pl.program_id(0)
    l = qkv_ref.shape[0]
    nch = l // CHUNK
    h_n, dk = GDN_HEADS, GDN_DK
    gh = GDN_GROUP
    ng = h_n // gh
    gr = gh * CHUNK

    _conv_silu_tile(i, kind_ref, qkv_ref, cache_ref, cw_ref, tail, act, tmp, None)

    sm = sm_ref[...]
    beta_s[...] = jax.nn.sigmoid(sm)
    g = parr_ref[0:1, :] * _softplus(sm + parr_ref[1:2, :])
    gcol[...] = _cumsum_rows(g)
    a_t = sm.T[h_n:2 * h_n, :]
    g_t = parc_ref[0:h_n, 0:1] * _softplus(a_t + parc_ref[h_n:2 * h_n, 0:1])
    g_t = _cumsum_lanes(g_t)
    for cl in range(nch):
        for h in range(h_n):
            grow[cl, h // gh:h // gh + 1, (h % gh) * CHUNK:(h % gh + 1) * CHUNK] = (
                g_t[h:h + 1, cl * CHUNK:(cl + 1) * CHUNK])

    sout_ref[1:, :, :, :] = jnp.zeros((nch - 1,) + tuple(sout_ref.shape[1:]), F32)

    ri = lax.broadcasted_iota(I32, (gr, gr), 0)
    ci = lax.broadcasted_iota(I32, (gr, gr), 1)
    same = (ri >> CHUNK_SHIFT) == (ci >> CHUNK_SHIFT)
    causal = same & (ri >= ci)
    strict = same & (ri > ci)

    def row0(c):
        return c * CHUNK if isinstance(c, int) else pl.multiple_of(c * CHUNK, CHUNK)

    def solve_chunk(c):
        base = row0(c)
        gc_blk = gcol[pl.ds(base, CHUNK), :]
        bt_blk = beta_s[pl.ds(base, CHUNK), :]
        for g in range(ng):
            qs, ks, vs, gcs, bts, gls = [], [], [], [], [], []
            for j in range(gh):
                h = g * gh + j
                q = act[pl.ds(base, CHUNK), h * dk:(h + 1) * dk]
                k = act[pl.ds(base, CHUNK), GDN_KD + h * dk:GDN_KD + (h + 1) * dk]
                qs.append(q * lax.rsqrt(jnp.sum(q * q, axis=-1, keepdims=True) + 1e-6) * (dk ** -0.5))
                ks.append(k * lax.rsqrt(jnp.sum(k * k, axis=-1, keepdims=True) + 1e-6))
                vs.append(act[pl.ds(base, CHUNK), 2 * GDN_KD + h * dk:2 * GDN_KD + (h + 1) * dk])
                gcs.append(gc_blk[:, h_n + h:h_n + h + 1])
                bts.append(bt_blk[:, h:h + 1])
                gls.append(jnp.broadcast_to(gc_blk[CHUNK - 1:CHUNK, h_n + h:h_n + h + 1], (CHUNK, 1)))
            q4 = jnp.concatenate(qs, axis=0)
            k4 = jnp.concatenate(ks, axis=0)
            v4 = jnp.concatenate(vs, axis=0)
            gc4 = jnp.concatenate(gcs, axis=0)
            bt4 = jnp.concatenate(bts, axis=0)
            gl4 = jnp.concatenate(gls, axis=0)
            eg4 = jnp.exp(gc4)
            decay = jnp.where(causal, jnp.exp(jnp.minimum(gc4 - grow[c, g:g + 1, :], 0.0)), 0.0)
            kb = k4.astype(BF16)
            xm = jnp.where(strict, -(bt4 * _dot_nt(kb, kb) * decay), 0.0)
            y = jnp.concatenate([v4 * bt4, k4 * (bt4 * eg4)], axis=1)
            p = xm
            for step in range(6):
                if step < GDN_SPLIT_STEPS:
                    y = y + _dot3(p, y)
                else:
                    y = y + _dot(p.astype(BF16), y.astype(BF16))
                if step < GDN_SPLIT_STEPS - 1:
                    p = _dot3(p, p)
                elif step < 5:
                    pb = p.astype(BF16)
                    p = _dot(pb, pb)
            idx = c * ng + g
            u_s[idx] = y[:, :GDN_DV]
            w_s[idx] = y[:, GDN_DV:].astype(BF16)
            qk_s[idx] = (_dot_nt(q4.astype(BF16), kb) * decay).astype(BF16)
            qd_s[idx] = (q4 * eg4).astype(BF16)
            kdt_s[idx] = (k4 * jnp.exp(gl4 - gc4)).T.astype(BF16)

    rblk = lax.broadcasted_iota(I32, (gr, GDN_DV), 0) >> CHUNK_SHIFT

    def recur_chunk(c, solve_next):
        base = row0(c)
        gi = i * nch + c

        @pl.when(kind_ref[gi] == 1)
        def _():
            for h in range(h_n):
                state[h // gh, :, (h % gh) * GDN_DV:(h % gh + 1) * GDN_DV] = s0_ref[c, h]

        last8 = base + CHUNK - SUBLANES
        if not isinstance(c, int):
            last8 = pl.multiple_of(last8, SUBLANES)
        gc_last = gcol[pl.ds(last8, SUBLANES), :][SUBLANES - 1:SUBLANES]
        for g in range(ng):
            idx = c * ng + g
            s4 = state[g]
            w4 = w_s[idx]
            qd4 = qd_s[idx]
            ws, os_ = [], []
            for j in range(gh):
                sb = s4[:, j * GDN_DV:(j + 1) * GDN_DV].astype(BF16)
                lhs = jnp.concatenate([w4[j * CHUNK:(j + 1) * CHUNK], qd4[j * CHUNK:(j + 1) * CHUNK]], axis=0)
                r = _dot(lhs, sb)
                ws.append(r[:CHUNK])
                os_.append(r[CHUNK:])
            v_new = u_s[idx] - jnp.concatenate(ws, axis=0)
            vb = v_new.astype(BF16)
            o4 = jnp.concatenate(os_, axis=0) + _dot(qk_s[idx], vb)
            vbd = jnp.concatenate([jnp.where(rblk == j, v_new, 0.0) for j in range(gh)], axis=1).astype(BF16)
            egl = jnp.concatenate(
                [jnp.broadcast_to(jnp.exp(gc_last[:, h_n + g * gh + j:h_n + g * gh + j + 1]), (1, GDN_DV))
                 for j in range(gh)], axis=1)
            state[g] = s4 * egl + _dot(kdt_s[idx], vbd)
            for j in range(gh):
                h = g * gh + j
                o = o4[j * CHUNK:(j + 1) * CHUNK]
                zz = z_ref[pl.ds(base, CHUNK), h * GDN_DV:(h + 1) * GDN_DV]
                on = o * lax.rsqrt(jnp.mean(o * o, axis=-1, keepdims=True) + EPS) * nw_ref[...] * _silu(zz)
                o_ref[pl.ds(base, CHUNK), h * GDN_DV:(h + 1) * GDN_DV] = on.astype(BF16)

        if solve_next:
            solve_chunk(c + 1)

        @pl.when(emit_ref[gi] >= 0)
        def _():
            for h in range(h_n):
                sout_ref[emit_ref[gi], h] = state[h // gh, :, (h % gh) * GDN_DV:(h % gh + 1) * GDN_DV]

    solve_chunk(0)
    lax.fori_loop(0, nch - 1, lambda c, carry: (recur_chunk(c, True), carry)[1], 0)
    recur_chunk(nch - 1, False)


def _gdn(proj, cache_slots, s0_slots, conv_w, a_log, dt_bias, norm_w, tables, n_out_blk):
    t = proj.shape[0]
    l = MIX_TILE
    nch = TILE_CHUNKS
    kind, emit, in_blk, out_blk = tables
    ng = GDN_HEADS // GDN_GROUP
    gr = GDN_GROUP * CHUNK
    cw = jnp.concatenate([conv_w, jnp.zeros((SUBLANES - CONV_K, GDN_CONV), F32)], axis=0)
    a_neg = -jnp.exp(a_log.astype(F32))
    lane_pad = jnp.zeros((LANES - 2 * GDN_HEADS,), F32)
    parr = jnp.zeros((SUBLANES, LANES), F32)
    parr = parr.at[0].set(jnp.concatenate([jnp.zeros((GDN_HEADS,), F32), a_neg, lane_pad]))
    parr = parr.at[1].set(jnp.concatenate([jnp.zeros((GDN_HEADS,), F32), dt_bias.astype(F32), lane_pad]))
    parc = jnp.broadcast_to(jnp.concatenate([a_neg, dt_bias.astype(F32)])[:, None], (2 * GDN_HEADS, LANES))
    grid_spec = pltpu.PrefetchScalarGridSpec(
        num_scalar_prefetch=4,
        grid=(t // l,),
        in_specs=[
            pl.BlockSpec((l, GDN_CONV), lambda i, *_: (i, 0)),
            pl.BlockSpec((l, GDN_KD), lambda i, *_: (i, P_ZA // GDN_KD)),
            pl.BlockSpec((l, LANES), lambda i, *_: (i, P_SMALL // LANES)),
            pl.BlockSpec((nch, SUBLANES, GDN_CONV), lambda i, k, e, ib, ob: (ib[i], 0, 0)),
            pl.BlockSpec((nch, GDN_HEADS, GDN_DK, GDN_DV), lambda i, k, e, ib, ob: (ib[i], 0, 0, 0)),
            pl.BlockSpec((SUBLANES, GDN_CONV), lambda i, *_: (0, 0)),
            pl.BlockSpec((SUBLANES, LANES), lambda i, *_: (0, 0)),
            pl.BlockSpec((2 * GDN_HEADS, LANES), lambda i, *_: (0, 0)),
            pl.BlockSpec((1, GDN_DV), lambda i, *_: (0, 0)),
        ],
        out_specs=[
            pl.BlockSpec((l, GDN_KD), lambda i, *_: (i, 0)),
            pl.BlockSpec((nch, GDN_HEADS, GDN_DK, GDN_DV), lambda i, k, e, ib, ob: (ob[i], 0, 0, 0)),
        ],
        scratch_shapes=[
            pltpu.VMEM((SUBLANES, GDN_CONV), F32),
            pltpu.VMEM((l, GDN_CONV), F32),
            pltpu.VMEM((2 * SUBLANES, GDN_CONV), F32),
            pltpu.VMEM((ng, GDN_DK, GDN_GROUP * GDN_DV), F32),
            pltpu.VMEM((l, LANES), F32),
            pltpu.VMEM((l, LANES), F32),
            pltpu.VMEM((nch, ng, gr), F32),
            pltpu.VMEM((nch * ng, gr, GDN_DV), F32),
            pltpu.VMEM((nch * ng, gr, GDN_DK), BF16),
            pltpu.VMEM((nch * ng, gr, gr), BF16),
            pltpu.VMEM((nch * ng, gr, GDN_DK), BF16),
            pltpu.VMEM((nch * ng, GDN_DK, gr), BF16),
        ],
    )
    return pl.pallas_call(
        _gdn_kernel,
        grid_spec=grid_spec,
        out_shape=[jax.ShapeDtypeStruct((t, GDN_KD), BF16),
                   jax.ShapeDtypeStruct((n_out_blk * nch, GDN_HEADS, GDN_DK, GDN_DV), F32)],
        compiler_params=_cparams(("arbitrary",)),
        name="gdn_mixer",
    )(jnp.asarray(kind), jnp.asarray(emit), jnp.asarray(in_blk), jnp.asarray(out_blk),
      proj, proj, proj, cache_slots, s0_slots, cw, parr, parc, norm_w.reshape(1, GDN_DV).astype(F32))


SSD_PAIRS = SSD_HEADS // 2
SM_DT = 2 * GDN_HEADS


def _ssd_kernel(kind_ref, emit_ref, inb_ref, outb_ref,
                xbc_ref, z_ref, sm_ref, cache_ref, s0_ref, cw_ref, parr_ref, parc_ref, nw_ref, dsk_ref,
                o_ref, sout_ref,
                tail, act, tmp, state, cscol, dtcol, csrow):
    del inb_ref, outb_ref
    i = pl.program_id(0)
    l = xbc_ref.shape[0]
    nch = l // CHUNK
    hp = 2 * SSD_HEADDIM

    _conv_silu_tile(i, kind_ref, xbc_ref, cache_ref, cw_ref, tail, act, tmp, CONV_K)

    sm = sm_ref[...]
    dtc = _softplus(sm + parr_ref[1:2, :])
    dtcol[...] = dtc
    cscol[...] = _cumsum_rows(parr_ref[0:1, :] * dtc)
    dt_t = _softplus(sm.T[SM_DT:SM_DT + SSD_HEADS, :] + parc_ref[SSD_HEADS:2 * SSD_HEADS, 0:1])
    cs_t = _cumsum_lanes(parc_ref[0:SSD_HEADS, 0:1] * dt_t)
    left_row = (lax.broadcasted_iota(I32, (1, l), 1) & (hp - 1)) < SSD_HEADDIM
    for p in range(SSD_PAIRS):
        ra = cs_t[2 * p:2 * p + 1, :]
        rb = cs_t[2 * p + 1:2 * p + 2, :]
        even = jnp.where(left_row, ra, pltpu.roll(rb, SSD_HEADDIM, 1))
        odd = jnp.where(left_row, pltpu.roll(ra, l - SSD_HEADDIM, 1), rb)
        for cl in range(nch):
            src = even if cl % 2 == 0 else odd
            v0 = (cl // 2) * hp
            csrow[cl, p:p + 1, :] = src[:, v0:v0 + hp]

    sout_ref[1:, :, :, :] = jnp.zeros((nch - 1,) + tuple(sout_ref.shape[1:]), F32)

    ri = lax.broadcasted_iota(I32, (CHUNK, hp), 0)
    li = lax.broadcasted_iota(I32, (CHUNK, hp), 1)
    left = li < SSD_HEADDIM
    causal = ri >= (li & (SSD_HEADDIM - 1))
    top = lax.broadcasted_iota(I32, (hp, 1), 0) < SSD_HEADDIM
    zpad_b = jnp.zeros((CHUNK, SSD_STATE), BF16)
    zpad_f = jnp.zeros((CHUNK, hp), F32)

    def chunk(c, carry):
        base = pl.multiple_of(c * CHUNK, CHUNK)
        gi = i * nch + c

        @pl.when(kind_ref[gi] == 1)
        def _():
            state[...] = s0_ref[c]

        dt_blk = dtcol[pl.ds(base, CHUNK), :]
        cs_blk = cscol[pl.ds(base, CHUNK), :]
        csr = csrow[c]
        ys = []
        for g in range(SSD_NGROUPS):
            b0 = SSD_WIDTH + g * SSD_STATE
            c0 = SSD_WIDTH + SSD_NGROUPS * SSD_STATE + g * SSD_STATE
            bf = act[pl.ds(base, CHUNK), b0:b0 + SSD_STATE]
            cf = act[pl.ds(base, CHUNK), c0:c0 + SSD_STATE]
            bg = bf.astype(BF16)
            cg = cf.astype(BF16)
            cbw = _dot_nt(cg, jnp.concatenate([bg, bg], axis=0))
            for q in range(SSD_PAIRS // SSD_NGROUPS):
                p = g * (SSD_PAIRS // SSD_NGROUPS) + q
                h0 = SM_DT + 2 * p
                xp = act[pl.ds(base, CHUNK), p * hp:(p + 1) * hp]
                dtp = jnp.where(left, dt_blk[:, h0:h0 + 1], dt_blk[:, h0 + 1:h0 + 2])
                csp = jnp.where(left, cs_blk[:, h0:h0 + 1], cs_blk[:, h0 + 1:h0 + 2])
                cl0 = cs_blk[CHUNK - 1:CHUNK, h0:h0 + 1]
                cl1 = cs_blk[CHUNK - 1:CHUNK, h0 + 1:h0 + 2]
                xdt = xp * dtp
                seg = jnp.where(causal, jnp.exp(jnp.minimum(csp - csr[p:p + 1, :], 0.0)), 0.0)
                scores = (cbw * seg).astype(BF16)
                bd = jnp.concatenate([jnp.where(left, xdt, 0.0), jnp.where(left, 0.0, xdt)], axis=0)
                y = _dot(scores, bd.astype(BF16))
                sp = state[p]
                sb = sp.astype(BF16)
                xdt_t = jnp.concatenate([xdt, zpad_f], axis=0).T.astype(BF16)
                yo, st = [], []
                for hh in range(2):
                    ecs = jnp.exp(cs_blk[:, h0 + hh:h0 + hh + 1])
                    cl = cl0 if hh == 0 else cl1
                    dec = jnp.exp(cl - cs_blk[:, h0 + hh:h0 + hh + 1])
                    yo.append(_dot_nt((cf * ecs).astype(BF16), sb))
                    st.append(_dot(xdt_t, jnp.concatenate([(bf * dec).astype(BF16), zpad_b], axis=0)))
                y = y + jnp.where(left, yo[0], yo[1])
                y = y + xp * dsk_ref[:, p * hp:(p + 1) * hp]
                state[p] = sp * jnp.where(top, jnp.exp(cl0), jnp.exp(cl1)) + jnp.where(top, st[0], st[1])
                ys.append(y)
        yf = jnp.concatenate(ys, axis=1)
        yg = yf * _silu(z_ref[pl.ds(base, CHUNK), :].astype(F32))
        out = yg * lax.rsqrt(jnp.mean(yg * yg, axis=-1, keepdims=True) + EPS) * nw_ref[...]
        o_ref[pl.ds(base, CHUNK), :] = out.astype(BF16)

        @pl.when(emit_ref[gi] >= 0)
        def _():
            sout_ref[emit_ref[gi]] = state[...]

        return carry

    lax.fori_loop(0, nch, chunk, 0)


def _ssd(proj, cache_slots, s0_slots, conv_w, conv_b, a_log, dt_bias, d_skip, norm_w, tables, n_out_blk):
    t = proj.shape[0]
    l = MIX_TILE
    nch = TILE_CHUNKS
    hp = 2 * SSD_HEADDIM
    kind, emit, in_blk, out_blk = tables
    cw = jnp.concatenate([conv_w, conv_b[None, :], jnp.zeros((SUBLANES - CONV_K - 1, SSD_CONV), F32)], axis=0)
    a_neg = -jnp.exp(a_log.astype(F32))
    pre = jnp.zeros((SM_DT,), F32)
    post = jnp.zeros((LANES - SM_DT - SSD_HEADS,), F32)
    parr = jnp.zeros((SUBLANES, LANES), F32)
    parr = parr.at[0].set(jnp.concatenate([pre, a_neg, post]))
    parr = parr.at[1].set(jnp.concatenate([pre, dt_bias.astype(F32), post]))
    parc = jnp.broadcast_to(jnp.concatenate([a_neg, dt_bias.astype(F32)])[:, None], (2 * SSD_HEADS, LANES))
    dsk = jnp.repeat(d_skip.astype(F32), SSD_HEADDIM).reshape(1, SSD_WIDTH)
    grid_spec = pltpu.PrefetchScalarGridSpec(
        num_scalar_prefetch=4,
        grid=(t // l,),
        in_specs=[
            pl.BlockSpec((l, SSD_CONV), lambda i, *_: (i, P_XBC // SSD_CONV)),
            pl.BlockSpec((l, SSD_WIDTH), lambda i, *_: (i, P_ZC // SSD_WIDTH)),
            pl.BlockSpec((l, LANES), lambda i, *_: (i, P_SMALL // LANES)),
            pl.BlockSpec((nch, SUBLANES, SSD_CONV), lambda i, k, e, ib, ob: (ib[i], 0, 0)),
            pl.BlockSpec((nch, SSD_PAIRS, hp, SSD_STATE), lambda i, k, e, ib, ob: (ib[i], 0, 0, 0)),
            pl.BlockSpec((SUBLANES, SSD_CONV), lambda i, *_: (0, 0)),
            pl.BlockSpec((SUBLANES, LANES), lambda i, *_: (0, 0)),
            pl.BlockSpec((2 * SSD_HEADS, LANES), lambda i, *_: (0, 0)),
            pl.BlockSpec((1, SSD_WIDTH), lambda i, *_: (0, 0)),
            pl.BlockSpec((1, SSD_WIDTH), lambda i, *_: (0, 0)),
        ],
        out_specs=[
            pl.BlockSpec((l, SSD_WIDTH), lambda i, *_: (i, 0)),
            pl.BlockSpec((nch, SSD_PAIRS, hp, SSD_STATE), lambda i, k, e, ib, ob: (ob[i], 0, 0, 0)),
        ],
        scratch_shapes=[
            pltpu.VMEM((SUBLANES, SSD_CONV), F32),
            pltpu.VMEM((l, SSD_CONV), F32),
            pltpu.VMEM((2 * SUBLANES, SSD_CONV), F32),
            pltpu.VMEM((SSD_PAIRS, hp, SSD_STATE), F32),
            pltpu.VMEM((l, LANES), F32),
            pltpu.VMEM((l, LANES), F32),
            pltpu.VMEM((nch, SSD_PAIRS, hp), F32),
        ],
    )
    return pl.pallas_call(
        _ssd_kernel,
        grid_spec=grid_spec,
        out_shape=[jax.ShapeDtypeStruct((t, SSD_WIDTH), BF16),
                   jax.ShapeDtypeStruct((n_out_blk * nch, SSD_PAIRS, hp, SSD_STATE), F32)],
        compiler_params=_cparams(("arbitrary",)),
        name="ssd_mixer",
    )(jnp.asarray(kind), jnp.asarray(emit), jnp.asarray(in_blk), jnp.asarray(out_blk),
      proj, proj, proj, cache_slots, s0_slots, cw, parr, parc,
      norm_w.reshape(1, SSD_WIDTH).astype(F32), dsk)


S5_SB = 2
S5_LB = 512


def _s5_kernel(kind_ref, u_ref, h0r_ref, h0i_ref, perm_ref, permt_ref, bre_ref, bim_ref, cre_ref, cim_ref,
               ar_ref, ai_ref, dsk_ref, wglu_ref, bglu_ref,
               o_ref, hfr_ref, hfi_ref,
               bur, bui, pre, pim, cr, ci, inr, ini):
    i = pl.program_id(0)
    l = u_ref.shape[0]
    nch = l // CHUNK
    n = S5_N
    usb = S5_WIDTH // S5_SB
    nsb = n // S5_SB

    @pl.when(i == 0)
    def _():
        pre[0:1, :] = ar_ref[...]
        pim[0:1, :] = ai_ref[...]

        def pw(t, carry):
            pr = pre[pl.ds(t - 1, 1), :]
            pi = pim[pl.ds(t - 1, 1), :]
            pre[pl.ds(t, 1), :] = pr * ar_ref[...] - pi * ai_ref[...]
            pim[pl.ds(t, 1), :] = pr * ai_ref[...] + pi * ar_ref[...]
            return carry

        lax.fori_loop(1, CHUNK, pw, 0)
        cr[...] = jnp.zeros((1, n), F32)
        ci[...] = jnp.zeros((1, n), F32)

    up = _dot(perm_ref[...], u_ref[...].astype(BF16)).astype(BF16)
    for sb in range(S5_SB):
        us = up[:, sb * usb:(sb + 1) * usb]
        bur[:, sb * nsb:(sb + 1) * nsb] = _dot(us, bre_ref[sb])
        bui[:, sb * nsb:(sb + 1) * nsb] = _dot(us, bim_ref[sb])

    for c0 in range(0, n, S5_LB):
        a_r = jnp.broadcast_to(ar_ref[:, c0:c0 + S5_LB], (nch, S5_LB))
        a_i = jnp.broadcast_to(ai_ref[:, c0:c0 + S5_LB], (nch, S5_LB))

        def step(t, carry):
            hr, hi = carry
            r0 = pl.multiple_of(t * nch, nch)
            nr = a_r * hr - a_i * hi + bur[pl.ds(r0, nch), c0:c0 + S5_LB]
            ni = a_r * hi + a_i * hr + bui[pl.ds(r0, nch), c0:c0 + S5_LB]
            bur[pl.ds(r0, nch), c0:c0 + S5_LB] = nr
            bui[pl.ds(r0, nch), c0:c0 + S5_LB] = ni
            return nr, ni

        z = jnp.zeros((nch, S5_LB), F32)
        lax.fori_loop(0, CHUNK, step, (z, z))

    a64r = pre[CHUNK - 1:CHUNK, :]
    a64i = pim[CHUNK - 1:CHUNK, :]
    c_r = cr[...]
    c_i = ci[...]
    for s in range(nch):
        start = kind_ref[i * nch + s] == 1
        i_r = jnp.where(start, h0r_ref[s:s + 1, :], c_r)
        i_i = jnp.where(start, h0i_ref[s:s + 1, :], c_i)
        inr[s:s + 1, :] = i_r
        ini[s:s + 1, :] = i_i
        e_r = bur[l - nch + s:l - nch + s + 1, :]
        e_i = bui[l - nch + s:l - nch + s + 1, :]
        c_r = a64r * i_r - a64i * i_i + e_r
        c_i = a64r * i_i + a64i * i_r + e_i
        hfr_ref[s:s + 1, :] = c_r
        hfi_ref[s:s + 1, :] = c_i
    cr[...] = c_r
    ci[...] = c_i

    for c0 in range(0, n, S5_LB):
        n_r = inr[:, c0:c0 + S5_LB]
        n_i = ini[:, c0:c0 + S5_LB]

        def fix(t, carry):
            r0 = pl.multiple_of(t * nch, nch)
            p_r = pre[pl.ds(t, 1), c0:c0 + S5_LB]
            p_i = pim[pl.ds(t, 1), c0:c0 + S5_LB]
            bur[pl.ds(r0, nch), c0:c0 + S5_LB] += p_r * n_r - p_i * n_i
            bui[pl.ds(r0, nch), c0:c0 + S5_LB] += p_r * n_i + p_i * n_r
            return carry

        lax.fori_loop(0, CHUNK, fix, 0)

    ys = []
    for sb in range(S5_SB):
        hr = bur[:, sb * nsb:(sb + 1) * nsb].astype(BF16)
        hi = bui[:, sb * nsb:(sb + 1) * nsb].astype(BF16)
        ys.append(_dot(hr, cre_ref[sb]) - _dot(hi, cim_ref[sb]))
    yp = jnp.concatenate(ys, axis=1)
    y_hi = yp.astype(BF16)
    r1 = yp - y_hi.astype(F32)
    y_mid = r1.astype(BF16)
    y_lo = (r1 - y_mid.astype(F32)).astype(BF16)
    pt = permt_ref[...]
    y = (_dot(pt, y_hi) + _dot(pt, y_mid)) + _dot(pt, y_lo)
    y = y + u_ref[...] * dsk_ref[...]
    y = y * (0.5 * (1.0 + jnp.tanh(math.sqrt(2.0 / math.pi) * (y + 0.044715 * (y * y * y)))))
    out = y * jax.nn.sigmoid(_dot(y.astype(BF16), wglu_ref[...]) + bglu_ref[...])
    o_ref[...] = out.astype(BF16)


def _s5_tables(a_re, a_im, b_re, b_im, c_re, c_im, log_dt):
    a_re, a_im = a_re.astype(F32), a_im.astype(F32)
    dt = jnp.exp(log_dt.astype(F32))[:, None]
    mag = jnp.exp(dt * a_re)
    abar_re, abar_im = mag * jnp.cos(dt * a_im), mag * jnp.sin(dt * a_im)
    den = a_re * a_re + a_im * a_im
    num_re, num_im = abar_re - 1.0, abar_im
    zoh_re = (num_re * a_re + num_im * a_im) / den
    zoh_im = (num_im * a_re - num_re * a_im) / den
    b_re, b_im = b_re.astype(F32), b_im.astype(F32)
    bbar_re = zoh_re[..., None] * b_re - zoh_im[..., None] * b_im
    bbar_im = zoh_re[..., None] * b_im + zoh_im[..., None] * b_re
    gsb = S5_GROUPS // S5_SB
    eye = jnp.eye(gsb, dtype=F32)

    def bblk(b):
        b = b.reshape(S5_SB, gsb, S5_STATE, S5_GROUP)
        return jnp.einsum('sgpc,gh->sgchp', b, eye).reshape(S5_SB, gsb * S5_GROUP, gsb * S5_STATE).astype(BF16)

    def cblk(c):
        c = c.astype(F32).reshape(S5_SB, gsb, S5_GROUP, S5_STATE)
        return jnp.einsum('sgcp,gh->sgphc', c, eye).reshape(S5_SB, gsb * S5_STATE, gsb * S5_GROUP).astype(BF16)

    return (abar_re.reshape(1, S5_N), abar_im.reshape(1, S5_N),
            bblk(bbar_re), bblk(bbar_im), cblk(c_re), cblk(c_im))


def _s5_perm():
    r_new = np.arange(MIX_TILE)
    r_old = (r_new % TILE_CHUNKS) * CHUNK + r_new // TILE_CHUNKS
    p = np.zeros((MIX_TILE, MIX_TILE), np.float32)
    p[r_new, r_old] = 1.0
    return jnp.asarray(p, BF16), jnp.asarray(p.T, BF16)


def _s5(proj, h0r, h0i, tabs, d_skip, w_glu, b_glu, kind):
    t = proj.shape[0]
    l = MIX_TILE
    nch = TILE_CHUNKS
    abr, abi, bre, bim, cre, cim = tabs
    perm, permt = _s5_perm()
    full2 = lambda a: pl.BlockSpec(a.shape, lambda i, *_: (0, 0))
    full3 = lambda a: pl.BlockSpec(a.shape, lambda i, *_: (0, 0, 0))
    dsk = d_skip.astype(F32).reshape(1, S5_WIDTH)
    wg = w_glu.astype(BF16)
    bg = b_glu.astype(F32).reshape(1, S5_WIDTH)
    grid_spec = pltpu.PrefetchScalarGridSpec(
        num_scalar_prefetch=1,
        grid=(t // l,),
        in_specs=[
            pl.BlockSpec((l, S5_WIDTH), lambda i, *_: (i, P_UB // S5_WIDTH)),
            pl.BlockSpec((nch, S5_N), lambda i, *_: (i, 0)),
            pl.BlockSpec((nch, S5_N), lambda i, *_: (i, 0)),
            full2(perm), full2(permt), full3(bre), full3(bim), full3(cre), full3(cim),
            full2(abr), full2(abi), full2(dsk), full2(wg), full2(bg),
        ],
        out_specs=[
            pl.BlockSpec((l, S5_WIDTH), lambda i, *_: (i, 0)),
            pl.BlockSpec((nch, S5_N), lambda i, *_: (i, 0)),
            pl.BlockSpec((nch, S5_N), lambda i, *_: (i, 0)),
        ],
        scratch_shapes=[
            pltpu.VMEM((l, S5_N), F32), pltpu.VMEM((l, S5_N), F32),
            pltpu.VMEM((CHUNK, S5_N), F32), pltpu.VMEM((CHUNK, S5_N), F32),
            pltpu.VMEM((1, S5_N), F32), pltpu.VMEM((1, S5_N), F32),
            pltpu.VMEM((nch, S5_N), F32), pltpu.VMEM((nch, S5_N), F32),
        ],
    )
    nseg = t // CHUNK
    return pl.pallas_call(
        _s5_kernel,
        grid_spec=grid_spec,
        out_shape=[jax.ShapeDtypeStruct((t, S5_WIDTH), BF16),
                   jax.ShapeDtypeStruct((nseg, S5_N), F32),
                   jax.ShapeDtypeStruct((nseg, S5_N), F32)],
        compiler_params=_cparams(("arbitrary",)),
        name="s5_mixer",
    )(jnp.asarray(kind), proj, h0r, h0i, perm, permt, bre, bim, cre, cim, abr, abi, dsk, wg, bg)


MIXOUT_TM = 512
R_GRP = 0
R_EXP = SUBLANES
NEG_BIG = -1e30


def _mixout_kernel(oa_ref, ob_ref, oc_ref, w_ref, x_ref, nw_ref, wr_ref, rb_ref, x1_ref, h2_ref, lg_ref):
    acc = _dot(oa_ref[...], w_ref[0:GDN_KD, :])
    acc = acc + _dot(ob_ref[...], w_ref[GDN_KD:GDN_KD + S5_WIDTH, :])
    acc = acc + _dot(oc_ref[...], w_ref[GDN_KD + S5_WIDTH:, :])
    x1 = x_ref[...] + acc
    x1_ref[...] = x1
    h = x1 * lax.rsqrt(jnp.mean(x1 * x1, axis=-1, keepdims=True) + EPS) * nw_ref[...]
    hb = h.astype(BF16)
    h2_ref[...] = _pack_bf16_pairs(hb)
    lg_ref[...] = _dot(hb, wr_ref[...]) + rb_ref[...]


def _mixout(oa, ob, oc, w_out, x, nw, wr, rb):
    t, d = x.shape
    tm = MIXOUT_TM
    row = lambda w: pl.BlockSpec((tm, w), lambda i: (i, 0))
    full = lambda a: pl.BlockSpec(a.shape, lambda i: (0, 0))
    return pl.pallas_call(
        _mixout_kernel,
        grid=(t // tm,),
        in_specs=[row(GDN_KD), row(S5_WIDTH), row(SSD_WIDTH), full(w_out), row(d), full(nw), full(wr), full(rb)],
        out_specs=[row(d), row(d // 2), row(LANES)],
        out_shape=[jax.ShapeDtypeStruct((t, d), F32), jax.ShapeDtypeStruct((t, d // 2), U32),
                   jax.ShapeDtypeStruct((t, LANES), F32)],
        compiler_params=_cparams(("parallel",)),
        name="mix_out",
    )(oa, ob, oc, w_out, x, nw, wr, rb)


def _router_weights(rg_w, rg_b, re_w, re_b):
    d = rg_w.shape[0]
    wr = jnp.concatenate([rg_w, jnp.zeros((d, R_EXP - N_GROUPS), F32), re_w,
                          jnp.zeros((d, LANES - R_EXP - N_EXPERTS), F32)], axis=1).astype(BF16)
    rb = jnp.concatenate([rg_b.astype(F32), jnp.full((R_EXP - N_GROUPS,), NEG_BIG, F32), re_b.astype(F32),
                          jnp.zeros((LANES - R_EXP - N_EXPERTS,), F32)]).reshape(1, LANES)
    return wr, rb


ROUTE_TM = 512


def _router_kernel(lg_ref, tri_ref, idx_ref, gate_ref, cnt_ref, run):
    i = pl.program_id(0)
    tm = lg_ref.shape[0]

    @pl.when(i == 0)
    def _():
        run[...] = jnp.zeros(run.shape, F32)

    lt = lg_ref[...].T
    row8 = lax.broadcasted_iota(I32, (SUBLANES, tm), 0)
    grp = lt[R_GRP:R_GRP + SUBLANES, :]
    gm = jnp.max(grp, axis=0, keepdims=True)
    gp_top = 1.0 / jnp.sum(jnp.exp(grp - gm), axis=0, keepdims=True)
    g_top = jnp.min(jnp.where(grp == gm, row8, SUBLANES), axis=0, keepdims=True)
    ing = jnp.zeros((EPG, tm), F32)
    for g in range(N_GROUPS):
        ing = jnp.where(g_top == g, lt[R_EXP + g * EPG:R_EXP + (g + 1) * EPG, :], ing)
    em = jnp.max(ing, axis=0, keepdims=True)
    ee = jnp.exp(ing - em)
    p = ee / jnp.sum(ee, axis=0, keepdims=True)
    v1 = jnp.max(p, axis=0, keepdims=True)
    i1 = jnp.min(jnp.where(p == v1, row8, EPG), axis=0, keepdims=True)
    p2 = jnp.where(row8 == i1, -1.0, p)
    v2 = jnp.max(p2, axis=0, keepdims=True)
    i2 = jnp.min(jnp.where(p2 == v2, row8, EPG), axis=0, keepdims=True)
    den = v1 + v2
    gate1 = gp_top * v1 / den
    gate2 = gp_top * v2 / den
    e1 = g_top * EPG + i1
    e2 = g_top * EPG + i2

    erow = lax.broadcasted_iota(I32, (N_EXPERTS, tm), 0)
    hit1 = erow == e1
    hit2 = erow == e2
    oh = jnp.where(hit1 | hit2, 1.0, 0.0)
    before = _dot(oh.astype(BF16), tri_ref[...]) + run[:, 0:1]
    rank1 = jnp.sum(jnp.where(hit1, before, 0.0), axis=0, keepdims=True).astype(I32)
    rank2 = jnp.sum(jnp.where(hit2, before, 0.0), axis=0, keepdims=True).astype(I32)
    run[...] = run[...] + jnp.sum(oh, axis=1, keepdims=True)
    cnt_ref[...] = run[...].astype(I32)

    zi = jnp.zeros((SUBLANES - 4, tm), I32)
    idx_ref[...] = jnp.concatenate([e1, e2, rank1, rank2, zi], axis=0)
    r128 = lax.broadcasted_iota(I32, (LANES, tm), 0)
    gt = jnp.where(r128 == 0, gate1, jnp.where(r128 == 1, gate2, 0.0))
    gate_ref[...] = gt.T


def _router(logits):
    t = logits.shape[0]
    tm = ROUTE_TM
    tri = jnp.asarray(np.triu(np.ones((tm, tm), np.float32), 1), BF16)
    return pl.pallas_call(
        _router_kernel,
        grid=(t // tm,),
        in_specs=[pl.BlockSpec((tm, LANES), lambda i: (i, 0)),
                  pl.BlockSpec((tm, tm), lambda i: (0, 0))],
        out_specs=[pl.BlockSpec((SUBLANES, tm), lambda i: (0, i)),
                   pl.BlockSpec((tm, LANES), lambda i: (i, 0)),
                   pl.BlockSpec((N_EXPERTS, LANES), lambda i: (0, 0))],
        out_shape=[jax.ShapeDtypeStruct((SUBLANES, t), I32),
                   jax.ShapeDtypeStruct((t, LANES), F32),
                   jax.ShapeDtypeStruct((N_EXPERTS, LANES), I32)],
        scratch_shapes=[pltpu.VMEM((N_EXPERTS, LANES), F32)],
        compiler_params=_cparams(("arbitrary",)),
        name="router",
    )(logits, tri)


MOE_BLOCK = 512
MOE_BLOCK_SHIFT = 9
DMA_UNROLL = 16
TOP_K = 2
DISPATCH_TM = 512
COMBINE_TM = 256


def _dispatch_kernel(pstart_ref, cnt_ref, nv_ref, idx_ref, h2_ref, xs_ref, zbuf, sem):
    i = pl.program_id(0)
    tm = idx_ref.shape[1]
    nb = xs_ref.shape[0] // MOE_BLOCK

    @pl.when(i == 0)
    def _():
        zbuf[...] = jnp.zeros(zbuf.shape, zbuf.dtype)

        def pad_copy(e, r):
            return pltpu.make_async_copy(zbuf.at[pl.ds(0, 1)], xs_ref.at[pl.ds(pstart_ref[e] + r, 1)], sem)

        def per_expert(e, carry):
            n = cnt_ref[e]
            padded = ((n + MOE_BLOCK - 1) >> MOE_BLOCK_SHIFT) << MOE_BLOCK_SHIFT
            lax.fori_loop(n, padded, lambda r, c: (pad_copy(e, r).start(), c)[1], 0)
            lax.fori_loop(n, padded, lambda r, c: (pad_copy(e, r).wait(), c)[1], 0)
            return carry

        lax.fori_loop(0, N_EXPERTS, per_expert, 0)

        def blk_copy(b):
            return pltpu.make_async_copy(zbuf, xs_ref.at[pl.ds(b * MOE_BLOCK, MOE_BLOCK)], sem)

        lax.fori_loop(nv_ref[0], nb, lambda b, c: (blk_copy(b).start(), c)[1], 0)
        lax.fori_loop(nv_ref[0], nb, lambda b, c: (blk_copy(b).wait(), c)[1], 0)

    def copy(t, k):
        slot = idx_ref[k, t]
        return pltpu.make_async_copy(h2_ref.at[pl.ds(t, 1)], xs_ref.at[pl.ds(slot, 1)], sem)

    def issue(t, carry):
        copy(t, 0).start(priority=0)
        copy(t, 1).start(priority=1)
        return carry

    def drain(t, carry):
        copy(t, 0).wait()
        copy(t, 1).wait()
        return carry

    lax.fori_loop(0, tm, issue, 0, unroll=DMA_UNROLL)
    lax.fori_loop(0, tm, drain, 0, unroll=DMA_UNROLL)


def _dispatch(pad_start, counts, n_valid, idx, h2, n_slots):
    t, d = h2.shape
    tm = DISPATCH_TM
    grid_spec = pltpu.PrefetchScalarGridSpec(
        num_scalar_prefetch=3,
        grid=(t // tm,),
        in_specs=[pl.BlockSpec((SUBLANES, tm), lambda i, *_: (0, i), memory_space=pltpu.SMEM),
                  pl.BlockSpec((tm, d), lambda i, *_: (i, 0))],
        out_specs=pl.BlockSpec(memory_space=pl.ANY),
        scratch_shapes=[pltpu.VMEM((MOE_BLOCK, d), h2.dtype), pltpu.SemaphoreType.DMA(())],
    )
    return pl.pallas_call(
        _dispatch_kernel,
        grid_spec=grid_spec,
        out_shape=jax.ShapeDtypeStruct((n_slots, d), h2.dtype),
        compiler_params=_cparams(("arbitrary",)),
        name="moe_dispatch",
    )(pad_start, counts, n_valid, idx, h2)


def _expert_kernel(be_ref, nv_ref, xs_ref, wg_ref, wu_ref, wd_ref, ys_ref, wg_s, wu_s, wd_s):
    b = pl.program_id(0)
    valid = b < nv_ref[0]

    @pl.when(valid & ((b == 0) | (be_ref[b] != be_ref[jnp.maximum(b - 1, 0)])))
    def _():
        wg_s[...] = wg_ref[0].astype(BF16)
        wu_s[...] = wu_ref[0].astype(BF16)
        wd_s[...] = wd_ref[0].astype(BF16)

    @pl.when(valid)
    def _():
        x_lo, x_hi = _unpack_bf16_pairs(xs_ref[...])
        kh = wg_s.shape[0] // 2
        g = _dot(x_lo, wg_s[0:kh, :]) + _dot(x_hi, wg_s[kh:, :])
        u = _dot(x_lo, wu_s[0:kh, :]) + _dot(x_hi, wu_s[kh:, :])
        h = (_silu(g) * u).astype(BF16)
        ys_ref[...] = _dot(h, wd_s[...])

    @pl.when(jnp.logical_not(valid))
    def _():
        ys_ref[...] = jnp.zeros(ys_ref.shape, F32)


def _experts(block_expert, n_valid, xs, wg, wu, wd, layer):
    n_slots, dp = xs.shape
    d, de = wg.shape[2], wg.shape[3]
    assert d == 2 * dp
    nb = n_slots // MOE_BLOCK
    blk = lambda b, be, nv: (jnp.minimum(b, nv[0] - 1), 0)
    wsel = lambda b, be, nv: (layer, be[jnp.minimum(b, nv[0] - 1)], 0, 0)
    grid_spec = pltpu.PrefetchScalarGridSpec(
        num_scalar_prefetch=2,
        grid=(nb,),
        in_specs=[pl.BlockSpec((MOE_BLOCK, dp), blk),
                  pl.BlockSpec((None, 1, d, de), wsel),
                  pl.BlockSpec((None, 1, d, de), wsel),
                  pl.BlockSpec((None, 1, de, d), wsel)],
        out_specs=pl.BlockSpec((MOE_BLOCK, d), lambda b, be, nv: (b, 0)),
        scratch_shapes=[pltpu.VMEM((d, de), BF16), pltpu.VMEM((d, de), BF16), pltpu.VMEM((de, d), BF16)],
    )
    return pl.pallas_call(
        _expert_kernel,
        grid_spec=grid_spec,
        out_shape=jax.ShapeDtypeStruct((n_slots, d), F32),
        compiler_params=_cparams(("arbitrary",)),
        name="moe_experts",
    )(block_expert, n_valid, xs, wg, wu, wd)


def _combine_kernel(idx_ref, x1_ref, gate_ref, ys_ref, nw_ref, *rest, n_first):
    if n_first is None:
        out_ref, ybuf, sem = rest
    else:
        out_a_ref, out_b_ref, ybuf, sem = rest
    tm = x1_ref.shape[0]

    def copy(t, k):
        slot = idx_ref[k, t]
        return pltpu.make_async_copy(ys_ref.at[pl.ds(slot, 1)], ybuf.at[k, pl.ds(t, 1)], sem)

    def issue(t, carry):
        copy(t, 0).start(priority=0)
        copy(t, 1).start(priority=1)
        return carry

    def drain(t, carry):
        copy(t, 0).wait()
        copy(t, 1).wait()
        return carry

    lax.fori_loop(0, tm, issue, 0, unroll=DMA_UNROLL)
    lax.fori_loop(0, tm, drain, 0, unroll=DMA_UNROLL)
    g = gate_ref[...]
    y = ybuf[0] * g[:, 0:1] + ybuf[1] * g[:, 1:2]
    x2 = x1_ref[...] + y
    if n_first is None:
        out_ref[...] = x2
    else:
        x2 = x2 * lax.rsqrt(jnp.mean(x2 * x2, axis=-1, keepdims=True) + EPS) * nw_ref[...]
        i = pl.program_id(0)

        @pl.when(i < n_first)
        def _():
            out_a_ref[...] = x2

        @pl.when(i >= n_first)
        def _():
            out_b_ref[...] = x2


def _combine(idx, x1, gates, ys, nw, t_first):
    t, d = x1.shape
    tm = COMBINE_TM
    row = pl.BlockSpec((tm, d), lambda i, *_: (i, 0))
    if t_first is None:
        n_first = None
        out_specs = row
        out_shape = jax.ShapeDtypeStruct((t, d), F32)
    else:
        assert t_first % tm == 0
        n_first = t_first // tm
        out_specs = [pl.BlockSpec((tm, d), lambda i, *_: (jnp.minimum(i, n_first - 1), 0)),
                     pl.BlockSpec((tm, d), lambda i, *_: (jnp.maximum(i - n_first, 0), 0))]
        out_shape = [jax.ShapeDtypeStruct((t_first, d), F32), jax.ShapeDtypeStruct((t - t_first, d), F32)]
    grid_spec = pltpu.PrefetchScalarGridSpec(
        num_scalar_prefetch=0,
        grid=(t // tm,),
        in_specs=[pl.BlockSpec((SUBLANES, tm), lambda i, *_: (0, i), memory_space=pltpu.SMEM),
                  row,
                  pl.BlockSpec((tm, LANES), lambda i, *_: (i, 0)),
                  pl.BlockSpec(memory_space=pl.ANY),
                  pl.BlockSpec((1, d), lambda i, *_: (0, 0))],
        out_specs=out_specs,
        scratch_shapes=[pltpu.VMEM((2, tm, d), F32), pltpu.SemaphoreType.DMA(())],
    )
    return pl.pallas_call(
        functools.partial(_combine_kernel, n_first=n_first),
        grid_spec=grid_spec,
        out_shape=out_shape,
        compiler_params=_cparams(("arbitrary",)),
        name="moe_combine",
    )(idx, x1, gates, ys, nw)


def _moe(x1, h2, logits, wg, wu, wd, layer, norm_final, t_first):
    t, d = x1.shape
    idx, gates, cnt = _router(logits)
    counts = cnt[:, 0]
    padded = ((counts + MOE_BLOCK - 1) >> MOE_BLOCK_SHIFT) << MOE_BLOCK_SHIFT
    pad_end = jnp.cumsum(padded)
    pad_start = (pad_end - padded).astype(I32)
    nb = (2 * t + N_EXPERTS * (MOE_BLOCK - 1) + MOE_BLOCK - 1) // MOE_BLOCK
    n_valid = (pad_end[-1] >> MOE_BLOCK_SHIFT).astype(I32).reshape(1)
    starts = jnp.arange(nb, dtype=I32) * MOE_BLOCK
    block_expert = jnp.minimum(jnp.sum((pad_end[None, :] <= starts[:, None]).astype(I32), axis=1),
                               N_EXPERTS - 1).astype(I32)
    slots = pad_start[idx[0:TOP_K]] + idx[TOP_K:2 * TOP_K]
    slots = jnp.concatenate([slots, jnp.zeros((SUBLANES - TOP_K, t), I32)], axis=0)
    xs = _dispatch(pad_start, counts, n_valid, slots, h2, nb * MOE_BLOCK)
    ys = _experts(block_expert, n_valid, xs, wg, wu, wd, layer)
    return _combine(slots, x1, gates, ys, norm_final, t_first)


def _stream_ends(nbp, seq, nbs, dseq):
    ends = [(b + 1) * seq for b in range(nbp)] + [nbp * seq + (s + 1) * dseq for s in range(nbs)]
    return np.asarray(ends)


def kernel(x_prompt, x_sample, cache_conv_gdn, state_gdn, state_s5, cache_conv_ssd, state_ssd, norm_mix, w_in, gdn_conv_w, gdn_a_log, gdn_dt_bias, gdn_norm, s5_a_re, s5_a_im, s5_b_re, s5_b_im, s5_c_re, s5_c_im, s5_log_dt, s5_d, s5_w_glu, s5_b_glu, ssd_conv_w, ssd_conv_b, ssd_a_log, ssd_dt_bias, ssd_d, ssd_norm, w_out, norm_ffn, router_group_w, router_group_b, router_expert_w, router_expert_b, expert_w_gate, expert_w_up, expert_w_down, norm_final):
    nbp, seq, d = x_prompt.shape
    nbs, dseq, _ = x_sample.shape
    depth = w_in.shape[0]
    tp = nbp * seq
    t = tp + nbs * dseq
    x = jnp.concatenate([x_prompt.reshape(tp, d), x_sample.reshape(nbs * dseq, d)], axis=0)

    tables = _chunk_tables(nbp, seq, nbs, dseq)
    kind = tables[0]
    n_out_blk = nbp + nbs // TILE_CHUNKS
    ends = _stream_ends(nbp, seq, nbs, dseq)
    tail_rows = (ends[:, None] + np.arange(-(CONV_K - 1), 0)[None, :]).reshape(-1)
    end_seg = ends // CHUNK - 1
    ncp = tp // CHUNK
    state_rows = np.concatenate([np.arange(nbp) * TILE_CHUNKS, nbp * TILE_CHUNKS + np.arange(nbs)])

    new_conv_gdn, new_gdn, new_s5, new_conv_ssd, new_ssd = [], [], [], [], []
    for l in range(depth):
        proj = _proj(x, norm_mix[l].reshape(1, d).astype(F32), _rearrange_w_in(w_in[l]))

        oa, sg = _gdn(proj, _conv_cache_slots(cache_conv_gdn[l].astype(F32)),
                      _init_slots(state_gdn[l].astype(F32)), gdn_conv_w[l].astype(F32), gdn_a_log[l],
                      gdn_dt_bias[l], gdn_norm[l], tables, n_out_blk)

        h0 = state_s5[l].astype(F32).reshape(nbs, S5_N, 2)
        zeros_p = jnp.zeros((ncp, S5_N), F32)
        ob, hfr, hfi = _s5(proj, jnp.concatenate([zeros_p, h0[..., 0]], axis=0),
                           jnp.concatenate([zeros_p, h0[..., 1]], axis=0),
                           _s5_tables(s5_a_re[l], s5_a_im[l], s5_b_re[l], s5_b_im[l], s5_c_re[l], s5_c_im[l],
                                      s5_log_dt[l]),
                           s5_d[l], s5_w_glu[l], s5_b_glu[l], kind)

        oc, ss = _ssd(proj, _conv_cache_slots(cache_conv_ssd[l].astype(F32)),
                      _init_slots(state_ssd[l].astype(F32).reshape(nbs, SSD_PAIRS, 2 * SSD_HEADDIM, SSD_STATE)),
                      ssd_conv_w[l].astype(F32), ssd_conv_b[l].astype(F32), ssd_a_log[l], ssd_dt_bias[l],
                      ssd_d[l], ssd_norm[l], tables, n_out_blk)

        wr, rb = _router_weights(router_group_w[l].astype(F32), router_group_b[l],
                                 router_expert_w[l].astype(F32), router_expert_b[l])
        x1, h2, logits = _mixout(oa, ob, oc, w_out[l].astype(BF16), x,
                                 norm_ffn[l].reshape(1, d).astype(F32), wr, rb)
        x = _moe(x1, h2, logits, expert_w_gate, expert_w_up, expert_w_down, l,
                 norm_final.reshape(1, d).astype(F32), tp if l == depth - 1 else None)

        tails = proj[tail_rows]
        new_conv_gdn.append(tails[:, P_QKV:P_QKV + GDN_CONV].reshape(nbp + nbs, CONV_K - 1, GDN_CONV))
        new_conv_ssd.append(tails[:, P_XBC:P_XBC + SSD_CONV].reshape(nbp + nbs, CONV_K - 1, SSD_CONV))
        new_gdn.append(sg[state_rows])
        new_ssd.append(ss[state_rows].reshape(nbp + nbs, SSD_HEADS, SSD_HEADDIM, SSD_STATE))
        new_s5.append(jnp.stack([hfr[end_seg], hfi[end_seg]], axis=-1)
                      .reshape(nbp + nbs, S5_GROUPS, S5_STATE, 2))

    def split(parts):
        a = jnp.stack(parts)
        return a[:, :nbp], a[:, nbp:]

    cg_p, cg_s = split(new_conv_gdn)
    sg_p, sg_s = split(new_gdn)
    s5_p, s5_s = split(new_s5)
    cs_p, cs_s = split(new_conv_ssd)
    ss_p, ss_s = split(new_ssd)
    y_prompt = x[0].reshape(nbp, seq, d)
    y_sample = x[1].reshape(nbs, dseq, d)
    return (y_prompt, y_sample, cg_p, sg_p, s5_p, cs_p, ss_p, cg_s, sg_s, s5_s, cs_s, ss_s)
```

```python
import functools
import math

import numpy as np
import jax
import jax.numpy as jnp
from jax import lax
from jax.experimental import pallas as pl
from jax.experimental.pallas import tpu as pltpu

F32 = jnp.float32
BF16 = jnp.bfloat16
I32 = jnp.int32
U32 = jnp.uint32

EPS = 1e-6
CHUNK = 64
CHUNK_SHIFT = 6
CONV_K = 4
LANES = 128
SUBLANES = 8
VMEM_LIMIT = 56 * 1024 * 1024

GDN_HEADS = 8
GDN_DK = 128
GDN_DV = 128
GDN_KD = GDN_HEADS * GDN_DK
GDN_CONV = 3 * GDN_KD
GDN_GROUP = 4
GDN_SPLIT_STEPS = 2
S5_WIDTH = 512
S5_GROUPS = 32
S5_GROUP = 16
S5_STATE = 64
S5_N = S5_GROUPS * S5_STATE
SSD_WIDTH = 512
SSD_HEADS = 8
SSD_HEADDIM = 64
SSD_NGROUPS = 2
SSD_STATE = 128
SSD_CONV = SSD_WIDTH + 2 * SSD_NGROUPS * SSD_STATE
N_GROUPS = 4
EPG = 8
N_EXPERTS = 32

MIX_TILE = 512
TILE_CHUNKS = MIX_TILE // CHUNK


def _cparams(sem, vmem=VMEM_LIMIT):
    return pltpu.CompilerParams(dimension_semantics=sem, vmem_limit_bytes=vmem)


def _silu(x):
    return x * jax.nn.sigmoid(x)


def _softplus(x):
    return jnp.maximum(x, 0.0) + jnp.log1p(jnp.exp(-jnp.abs(x)))


def _dot(a, b):
    return jnp.dot(a, b, preferred_element_type=F32)


def _dot3(a, b):
    ah = a.astype(BF16)
    bh = b.astype(BF16)
    al = (a - ah.astype(F32)).astype(BF16)
    bl = (b - bh.astype(F32)).astype(BF16)
    return _dot(ah, bh) + (_dot(ah, bl) + _dot(al, bh))


def _pack_bf16_pairs(xb):
    n = xb.shape[1] // 2
    lo = lax.bitcast_convert_type(xb[:, :n].astype(F32), U32)
    hi = lax.bitcast_convert_type(xb[:, n:].astype(F32), U32)
    return (hi & jnp.uint32(0xFFFF0000)) | (lo >> 16)


def _unpack_bf16_pairs(w):
    lo = lax.bitcast_convert_type(w << 16, F32).astype(BF16)
    hi = lax.bitcast_convert_type(w & jnp.uint32(0xFFFF0000), F32).astype(BF16)
    return lo, hi


def _dot_nt(a, b):
    return lax.dot_general(a, b, (((1,), (1,)), ((), ())), preferred_element_type=F32)


def _dot_tn(a, b):
    return lax.dot_general(a, b, (((0,), (0,)), ((), ())), preferred_element_type=F32)


def _cumsum_rows(x):
    row = lax.broadcasted_iota(I32, x.shape, 0) & (CHUNK - 1)
    k = 1
    while k < CHUNK:
        x = x + jnp.where(row >= k, pltpu.roll(x, k, 0), 0.0)
        k *= 2
    return x


def _cumsum_lanes(x):
    lane = lax.broadcasted_iota(I32, x.shape, 1) & (CHUNK - 1)
    k = 1
    while k < CHUNK:
        x = x + jnp.where(lane >= k, pltpu.roll(x, k, 1), 0.0)
        k *= 2
    return x


PROJ_TM = 1024
PROJ_TN = 1280
P_QKV, P_ZA, P_UB, P_ZC, P_XBC = 0, 3072, 4096, 4608, 5120
P_MAIN = 6144
P_TOTAL = 6400
P_SMALL = P_TOTAL - LANES


def _proj_kernel(x_ref, nw_ref, w_ref, o_ref, h_scr):
    @pl.when(pl.program_id(1) == 0)
    def _():
        x = x_ref[...]
        ms = jnp.mean(x * x, axis=-1, keepdims=True)
        h_scr[...] = (x * lax.rsqrt(ms + EPS) * nw_ref[...]).astype(BF16)

    o_ref[...] = _dot(h_scr[...], w_ref[...])


def _proj(x, nw, w):
    t, d = x.shape
    n = w.shape[1]
    tm = min(PROJ_TM, t)
    assert t % tm == 0 and n % PROJ_TN == 0
    return pl.pallas_call(
        _proj_kernel,
        grid=(t // tm, n // PROJ_TN),
        in_specs=[pl.BlockSpec((tm, d), lambda i, j: (i, 0)),
                  pl.BlockSpec((1, d), lambda i, j: (0, 0)),
                  pl.BlockSpec((d, PROJ_TN), lambda i, j: (0, j))],
        out_specs=pl.BlockSpec((tm, PROJ_TN), lambda i, j: (i, j)),
        out_shape=jax.ShapeDtypeStruct((t, n), F32),
        scratch_shapes=[pltpu.VMEM((tm, d), BF16)],
        compiler_params=_cparams(("parallel", "arbitrary")),
        name="proj_in",
    )(x, nw, w)


def _rearrange_w_in(w_in):
    d = w_in.shape[0]
    off_za = GDN_CONV
    off_ba = off_za + GDN_KD
    off_s5 = off_ba + 2 * GDN_HEADS
    off_zc = off_s5 + S5_WIDTH
    off_xbc = off_zc + SSD_WIDTH
    off_dt = off_xbc + SSD_CONV
    small = jnp.concatenate([w_in[:, off_ba:off_s5], w_in[:, off_dt:],
                             jnp.zeros((d, LANES - 3 * GDN_HEADS), w_in.dtype)], axis=1)
    w = jnp.concatenate([w_in[:, :off_ba], w_in[:, off_s5:off_dt],
                         jnp.zeros((d, P_TOTAL - P_MAIN - LANES), w_in.dtype), small], axis=1)
    return w.astype(BF16)


def _chunk_tables(nbp, seq, nbs, dseq):
    assert seq % MIX_TILE == 0 and dseq == CHUNK and (nbs * dseq) % MIX_TILE == 0
    cps = seq // CHUNK
    ncp = nbp * cps
    nc = ncp + nbs
    kind = np.zeros((nc,), np.int32)
    emit = np.full((nc,), -1, np.int32)
    for c in range(nc):
        if c < ncp:
            kind[c] = 1 if c % cps == 0 else 0
            if c % TILE_CHUNKS == TILE_CHUNKS - 1:
                emit[c] = 0
        else:
            kind[c] = 1
            emit[c] = (c - ncp) % TILE_CHUNKS
    ntp = ncp // TILE_CHUNKS
    nts = nbs // TILE_CHUNKS
    in_blk = np.concatenate([np.zeros((ntp,), np.int32), 1 + np.arange(nts, dtype=np.int32)])
    out_blk = np.concatenate([np.arange(ntp, dtype=np.int32) // (cps // TILE_CHUNKS),
                              nbp + np.arange(nts, dtype=np.int32)])
    return kind, emit, in_blk, out_blk


def _init_slots(x):
    return jnp.concatenate([jnp.zeros((TILE_CHUNKS,) + x.shape[1:], x.dtype), x], axis=0)


def _conv_cache_slots(cache):
    nbs, k1, c = cache.shape
    padded = jnp.concatenate([jnp.zeros((nbs, SUBLANES - k1, c), cache.dtype), cache], axis=1)
    return _init_slots(padded)


def _conv_silu_tile(i, kind_ref, in_ref, cache_ref, cw_ref, tail, act, tmp, bias_row):
    l, c = in_ref.shape
    cb = 512
    k1 = CONV_K - 1
    nch = l // CHUNK

    @pl.when(i == 0)
    def _():
        tail[...] = jnp.zeros(tail.shape, F32)

    def taps(src, lo, hi, c0):
        acc = src[lo - k1:hi - k1, c0:c0 + cb] * cw_ref[0:1, c0:c0 + cb]
        for j in range(1, CONV_K):
            acc = acc + src[lo - k1 + j:hi - k1 + j, c0:c0 + cb] * cw_ref[j:j + 1, c0:c0 + cb]
        if bias_row is not None:
            acc = acc + cw_ref[bias_row:bias_row + 1, c0:c0 + cb]
        return _silu(acc)

    def head_rows(prev, r0):
        tmp[0:SUBLANES, :] = prev
        tmp[SUBLANES:2 * SUBLANES, :] = in_ref[r0:r0 + SUBLANES, :]
        for c0 in range(0, c, cb):
            act[r0:r0 + SUBLANES, c0:c0 + cb] = taps(tmp, SUBLANES, 2 * SUBLANES, c0)

    def taps_rolled(lo, hi, c0):
        a = max(lo - SUBLANES, 0)
        x = in_ref[a:hi, c0:c0 + cb]
        acc = x * cw_ref[0:1, c0:c0 + cb]
        for j in range(1, CONV_K):
            acc = pltpu.roll(acc, 1, 0) + x * cw_ref[j:j + 1, c0:c0 + cb]
        acc = acc[lo - a:, :]
        if bias_row is not None:
            acc = acc + cw_ref[bias_row:bias_row + 1, c0:c0 + cb]
        return _silu(acc)

    for rb in range(nch):
        for c0 in range(0, c, cb):
            act[rb * CHUNK:(rb + 1) * CHUNK, c0:c0 + cb] = taps_rolled(rb * CHUNK, (rb + 1) * CHUNK, c0)
    head_rows(tail[...], 0)

    for cl in range(nch):
        @pl.when(kind_ref[i * nch + cl] == 1)
        def _():
            head_rows(cache_ref[cl], cl * CHUNK)

    tail[...] = in_ref[l - SUBLANES:l, :]


def _gdn_kernel(kind_ref, emit_ref, inb_ref, outb_ref,
                qkv_ref, z_ref, sm_ref, cache_ref, s0_ref, cw_ref, parr_ref, parc_ref, nw_ref,
                o_ref, sout_ref,
                tail, act, tmp, state, gcol, beta_s, grow, u_s, w_s, qk_s, qd_s, kdt_s):
    del inb_ref, outb_ref
    i = pl.program_id(0)
    l = qkv_ref.shape[0]
    nch = l // CHUNK
    h_n, dk = GDN_HEADS, GDN_DK
    gh = GDN_GROUP
    ng = h_n // gh
    gr = gh * CHUNK

    _conv_silu_tile(i, kind_ref, qkv_ref, cache_ref, cw_ref, tail, act, tmp, None)

    sm = sm_ref[...]
    beta_s[...] = jax.nn.sigmoid(sm)
    g = parr_ref[0:1, :] * _softplus(sm + parr_ref[1:2, :])
    gcol[...] = _cumsum_rows(g)
    a_t = sm.T[h_n:2 * h_n, :]
    g_t = parc_ref[0:h_n, 0:1] * _softplus(a_t + parc_ref[h_n:2 * h_n, 0:1])
    g_t = _cumsum_lanes(g_t)
    for cl in range(nch):
        for h in range(h_n):
            grow[cl, h // gh:h // gh + 1, (h % gh) * CHUNK:(h % gh + 1) * CHUNK] = (
                g_t[h:h + 1, cl * CHUNK:(cl + 1) * CHUNK])

    sout_ref[1:, :, :, :] = jnp.zeros((nch - 1,) + tuple(sout_ref.shape[1:]), F32)

    ri = lax.broadcasted_iota(I32, (gr, gr), 0)
    ci = lax.broadcasted_iota(I32, (gr, gr), 1)
    same = (ri >> CHUNK_SHIFT) == (ci >> CHUNK_SHIFT)
    causal = same & (ri >= ci)
    strict = same & (ri > ci)

    def row0(c):
        return c * CHUNK if isinstance(c, int) else pl.multiple_of(c * CHUNK, CHUNK)

    def solve_chunk(c):
        base = row0(c)
        gc_blk = gcol[pl.ds(base, CHUNK), :]
        bt_blk = beta_s[pl.ds(base, CHUNK), :]
        for g in range(ng):
            qs, ks, vs, gcs, bts, gls = [], [], [], [], [], []
            for j in range(gh):
                h = g * gh + j
                q = act[pl.ds(base, CHUNK), h * dk:(h + 1) * dk]
                k = act[pl.ds(base, CHUNK), GDN_KD + h * dk:GDN_KD + (h + 1) * dk]
                qs.append(q * lax.rsqrt(jnp.sum(q * q, axis=-1, keepdims=True) + 1e-6) * (dk ** -0.5))
                ks.append(k * lax.rsqrt(jnp.sum(k * k, axis=-1, keepdims=True) + 1e-6))
                vs.append(act[pl.ds(base, CHUNK), 2 * GDN_KD + h * dk:2 * GDN_KD + (h + 1) * dk])
                gcs.append(gc_blk[:, h_n + h:h_n + h + 1])
                bts.append(bt_blk[:, h:h + 1])
                gls.append(jnp.broadcast_to(gc_blk[CHUNK - 1:CHUNK, h_n + h:h_n + h + 1], (CHUNK, 1)))
            q4 = jnp.concatenate(qs, axis=0)
            k4 = jnp.concatenate(ks, axis=0)
            v4 = jnp.concatenate(vs, axis=0)
            gc4 = jnp.concatenate(gcs, axis=0)
            bt4 = jnp.concatenate(bts, axis=0)
            gl4 = jnp.concatenate(gls, axis=0)
            eg4 = jnp.exp(gc4)
            decay = jnp.where(causal, jnp.exp(jnp.minimum(gc4 - grow[c, g:g + 1, :], 0.0)), 0.0)
            kb = k4.astype(BF16)
            xm = jnp.where(strict, -(bt4 * _dot_nt(kb, kb) * decay), 0.0)
            y = jnp.concatenate([v4 * bt4, k4 * (bt4 * eg4)], axis=1)
            p = xm
            for step in range(6):
                if step < GDN_SPLIT_STEPS:
                    y = y + _dot3(p, y)
                else:
                    y = y + _dot(p.astype(BF16), y.astype(BF16))
                if step < GDN_SPLIT_STEPS - 1:
                    p = _dot3(p, p)
                elif step < 5:
                    pb = p.astype(BF16)
                    p = _dot(pb, pb)
            idx = c * ng + g
            u_s[idx] = y[:, :GDN_DV]
            w_s[idx] = y[:, GDN_DV:].astype(BF16)
            qk_s[idx] = (_dot_nt(q4.astype(BF16), kb) * decay).astype(BF16)
            qd_s[idx] = (q4 * eg4).astype(BF16)
            kdt_s[idx] = (k4 * jnp.exp(gl4 - gc4)).T.astype(BF16)

    rblk = lax.broadcasted_iota(I32, (gr, GDN_DV), 0) >> CHUNK_SHIFT

    def recur_chunk(c, solve_next):
        base = row0(c)
        gi = i * nch + c

        @pl.when(kind_ref[gi] == 1)
        def _():
            for h in range(h_n):
                state[h // gh, :, (h % gh) * GDN_DV:(h % gh + 1) * GDN_DV] = s0_ref[c, h]

        last8 = base + CHUNK - SUBLANES
        if not isinstance(c, int):
            last8 = pl.multiple_of(last8, SUBLANES)
        gc_last = gcol[pl.ds(last8, SUBLANES), :][SUBLANES - 1:SUBLANES]
        for g in range(ng):
            idx = c * ng + g
            s4 = state[g]
            w4 = w_s[idx]
            qd4 = qd_s[idx]
            ws, os_ = [], []
            for j in range(gh):
                sb = s4[:, j * GDN_DV:(j + 1) * GDN_DV].astype(BF16)
                lhs = jnp.concatenate([w4[j * CHUNK:(j + 1) * CHUNK], qd4[j * CHUNK:(j + 1) * CHUNK]], axis=0)
                r = _dot(lhs, sb)
                ws.append(r[:CHUNK])
                os_.append(r[CHUNK:])
            v_new = u_s[idx] - jnp.concatenate(ws, axis=0)
            vb = v_new.astype(BF16)
            o4 = jnp.concatenate(os_, axis=0) + _dot(qk_s[idx], vb)
            vbd = jnp.concatenate([jnp.where(rblk == j, v_new, 0.0) for j in range(gh)], axis=1).astype(BF16)
            egl = jnp.concatenate(
                [jnp.broadcast_to(jnp.exp(gc_last[:, h_n + g * gh + j:h_n + g * gh + j + 1]), (1, GDN_DV))
                 for j in range(gh)], axis=1)
            state[g] = s4 * egl + _dot(kdt_s[idx], vbd)
            for j in range(gh):
                h = g * gh + j
                o = o4[j * CHUNK:(j + 1) * CHUNK]
                zz = z_ref[pl.ds(base, CHUNK), h * GDN_DV:(h + 1) * GDN_DV]
                on = o * lax.rsqrt(jnp.mean(o * o, axis=-1, keepdims=True) + EPS) * nw_ref[...] * _silu(zz)
                o_ref[pl.ds(base, CHUNK), h * GDN_DV:(h + 1) * GDN_DV] = on.astype(BF16)

        if solve_next:
            solve_chunk(c + 1)

        @pl.when(emit_ref[gi] >= 0)
        def _():
            for h in range(h_n):
                sout_ref[emit_ref[gi], h] = state[h // gh, :, (h % gh) * GDN_DV:(h % gh + 1) * GDN_DV]

    solve_chunk(0)
    lax.fori_loop(0, nch - 1, lambda c, carry: (recur_chunk(c, True), carry)[1], 0)
    recur_chunk(nch - 1, False)


def _gdn(proj, cache_slots, s0_slots, conv_w, a_log, dt_bias, norm_w, tables, n_out_blk):
    t = proj.shape[0]
    l = MIX_TILE
    nch = TILE_CHUNKS
    kind, emit, in_blk, out_blk = tables
    ng = GDN_HEADS // GDN_GROUP
    gr = GDN_GROUP * CHUNK
    cw = jnp.concatenate([conv_w, jnp.zeros((SUBLANES - CONV_K, GDN_CONV), F32)], axis=0)
    a_neg = -jnp.exp(a_log.astype(F32))
    lane_pad = jnp.zeros((LANES - 2 * GDN_HEADS,), F32)
    parr = jnp.zeros((SUBLANES, LANES), F32)
    parr = parr.at[0].set(jnp.concatenate([jnp.zeros((GDN_HEADS,), F32), a_neg, lane_pad]))
    parr = parr.at[1].set(jnp.concatenate([jnp.zeros((GDN_HEADS,), F32), dt_bias.astype(F32), lane_pad]))
    parc = jnp.broadcast_to(jnp.concatenate([a_neg, dt_bias.astype(F32)])[:, None], (2 * GDN_HEADS, LANES))
    grid_spec = pltpu.PrefetchScalarGridSpec(
        num_scalar_prefetch=4,
        grid=(t // l,),
        in_specs=[
            pl.BlockSpec((l, GDN_CONV), lambda i, *_: (i, 0)),
            pl.BlockSpec((l, GDN_KD), lambda i, *_: (i, P_ZA // GDN_KD)),
            pl.BlockSpec((l, LANES), lambda i, *_: (i, P_SMALL // LANES)),
            pl.BlockSpec((nch, SUBLANES, GDN_CONV), lambda i, k, e, ib, ob: (ib[i], 0, 0)),
            pl.BlockSpec((nch, GDN_HEADS, GDN_DK, GDN_DV), lambda i, k, e, ib, ob: (ib[i], 0, 0, 0)),
            pl.BlockSpec((SUBLANES, GDN_CONV), lambda i, *_: (0, 0)),
            pl.BlockSpec((SUBLANES, LANES), lambda i, *_: (0, 0)),
            pl.BlockSpec((2 * GDN_HEADS, LANES), lambda i, *_: (0, 0)),
            pl.BlockSpec((1, GDN_DV), lambda i, *_: (0, 0)),
        ],
        out_specs=[
            pl.BlockSpec((l, GDN_KD), lambda i, *_: (i, 0)),
            pl.BlockSpec((nch, GDN_HEADS, GDN_DK, GDN_DV), lambda i, k, e, ib, ob: (ob[i], 0, 0, 0)),
        ],
        scratch_shapes=[
            pltpu.VMEM((SUBLANES, GDN_CONV), F32),
            pltpu.VMEM((l, GDN_CONV), F32),
            pltpu.VMEM((2 * SUBLANES, GDN_CONV), F32),
            pltpu.VMEM((ng, GDN_DK, GDN_GROUP * GDN_DV), F32),
            pltpu.VMEM((l, LANES), F32),
            pltpu.VMEM((l, LANES), F32),
            pltpu.VMEM((nch, ng, gr), F32),
            pltpu.VMEM((nch * ng, gr, GDN_DV), F32),
            pltpu.VMEM((nch * ng, gr, GDN_DK), BF16),
            pltpu.VMEM((nch * ng, gr, gr), BF16),
            pltpu.VMEM((nch * ng, gr, GDN_DK), BF16),
            pltpu.VMEM((nch * ng, GDN_DK, gr), BF16),
        ],
    )
    return pl.pallas_call(
        _gdn_kernel,
        grid_spec=grid_spec,
        out_shape=[jax.ShapeDtypeStruct((t, GDN_KD), BF16),
                   jax.ShapeDtypeStruct((n_out_blk * nch, GDN_HEADS, GDN_DK, GDN_DV), F32)],
        compiler_params=_cparams(("arbitrary",)),
        name="gdn_mixer",
    )(jnp.asarray(kind), jnp.asarray(emit), jnp.asarray(in_blk), jnp.asarray(out_blk),
      proj, proj, proj, cache_slots, s0_slots, cw, parr, parc, norm_w.reshape(1, GDN_DV).astype(F32))


SSD_PAIRS = SSD_HEADS // 2
SM_DT = 2 * GDN_HEADS


def _ssd_kernel(kind_ref, emit_ref, inb_ref, outb_ref,
                xbc_ref, z_ref, sm_ref, cache_ref, s0_ref, cw_ref, parr_ref, parc_ref, nw_ref, dsk_ref,
                o_ref, sout_ref,
                tail, act, tmp, state, cscol, dtcol, csrow):
    del inb_ref, outb_ref
    i = pl.program_id(0)
    l = xbc_ref.shape[0]
    nch = l // CHUNK
    hp = 2 * SSD_HEADDIM

    _conv_silu_tile(i, kind_ref, xbc_ref, cache_ref, cw_ref, tail, act, tmp, CONV_K)

    sm = sm_ref[...]
    dtc = _softplus(sm + parr_ref[1:2, :])
    dtcol[...] = dtc
    cscol[...] = _cumsum_rows(parr_ref[0:1, :] * dtc)
    dt_t = _softplus(sm.T[SM_DT:SM_DT + SSD_HEADS, :] + parc_ref[SSD_HEADS:2 * SSD_HEADS, 0:1])
    cs_t = _cumsum_lanes(parc_ref[0:SSD_HEADS, 0:1] * dt_t)
    left_row = (lax.broadcasted_iota(I32, (1, l), 1) & (hp - 1)) < SSD_HEADDIM
    for p in range(SSD_PAIRS):
        ra = cs_t[2 * p:2 * p + 1, :]
        rb = cs_t[2 * p + 1:2 * p + 2, :]
        even = jnp.where(left_row, ra, pltpu.roll(rb, SSD_HEADDIM, 1))
        odd = jnp.where(left_row, pltpu.roll(ra, l - SSD_HEADDIM, 1), rb)
        for cl in range(nch):
            src = even if cl % 2 == 0 else odd
            v0 = (cl // 2) * hp
            csrow[cl, p:p + 1, :] = src[:, v0:v0 + hp]

    sout_ref[1:, :, :, :] = jnp.zeros((nch - 1,) + tuple(sout_ref.shape[1:]), F32)

    ri = lax.broadcasted_iota(I32, (CHUNK, hp), 0)
    li = lax.broadcasted_iota(I32, (CHUNK, hp), 1)
    left = li < SSD_HEADDIM
    causal = ri >= (li & (SSD_HEADDIM - 1))
    top = lax.broadcasted_iota(I32, (hp, 1), 0) < SSD_HEADDIM
    zpad_b = jnp.zeros((CHUNK, SSD_STATE), BF16)
    zpad_f = jnp.zeros((CHUNK, hp), F32)

    def chunk(c, carry):
        base = pl.multiple_of(c * CHUNK, CHUNK)
        gi = i * nch + c

        @pl.when(kind_ref[gi] == 1)
        def _():
            state[...] = s0_ref[c]

        dt_blk = dtcol[pl.ds(base, CHUNK), :]
        cs_blk = cscol[pl.ds(base, CHUNK), :]
        csr = csrow[c]
        ys = []
        for g in range(SSD_NGROUPS):
            b0 = SSD_WIDTH + g * SSD_STATE
            c0 = SSD_WIDTH + SSD_NGROUPS * SSD_STATE + g * SSD_STATE
            bf = act[pl.ds(base, CHUNK), b0:b0 + SSD_STATE]
            cf = act[pl.ds(base, CHUNK), c0:c0 + SSD_STATE]
            bg = bf.astype(BF16)
            cg = cf.astype(BF16)
            cbw = _dot_nt(cg, jnp.concatenate([bg, bg], axis=0))
            for q in range(SSD_PAIRS // SSD_NGROUPS):
                p = g * (SSD_PAIRS // SSD_NGROUPS) + q
                h0 = SM_DT + 2 * p
                xp = act[pl.ds(base, CHUNK), p * hp:(p + 1) * hp]
                dtp = jnp.where(left, dt_blk[:, h0:h0 + 1], dt_blk[:, h0 + 1:h0 + 2])
                csp = jnp.where(left, cs_blk[:, h0:h0 + 1], cs_blk[:, h0 + 1:h0 + 2])
                cl0 = cs_blk[CHUNK - 1:CHUNK, h0:h0 + 1]
                cl1 = cs_blk[CHUNK - 1:CHUNK, h0 + 1:h0 + 2]
                xdt = xp * dtp
                seg = jnp.where(causal, jnp.exp(jnp.minimum(csp - csr[p:p + 1, :], 0.0)), 0.0)
                scores = (cbw * seg).astype(BF16)
                bd = jnp.concatenate([jnp.where(left, xdt, 0.0), jnp.where(left, 0.0, xdt)], axis=0)
                y = _dot(scores, bd.astype(BF16))
                sp = state[p]
                sb = sp.astype(BF16)
                xdt_t = jnp.concatenate([xdt, zpad_f], axis=0).T.astype(BF16)
                yo, st = [], []
                for hh in range(2):
                    ecs = jnp.exp(cs_blk[:, h0 + hh:h0 + hh + 1])
                    cl = cl0 if hh == 0 else cl1
                    dec = jnp.exp(cl - cs_blk[:, h0 + hh:h0 + hh + 1])
                    yo.append(_dot_nt((cf * ecs).astype(BF16), sb))
                    st.append(_dot(xdt_t, jnp.concatenate([(bf * dec).astype(BF16), zpad_b], axis=0)))
                y = y + jnp.where(left, yo[0], yo[1])
                y = y + xp * dsk_ref[:, p * hp:(p + 1) * hp]
                state[p] = sp * jnp.where(top, jnp.exp(cl0), jnp.exp(cl1)) + jnp.where(top, st[0], st[1])
                ys.append(y)
        yf = jnp.concatenate(ys, axis=1)
        yg = yf * _silu(z_ref[pl.ds(base, CHUNK), :].astype(F32))
        out = yg * lax.rsqrt(jnp.mean(yg * yg, axis=-1, keepdims=True) + EPS) * nw_ref[...]
        o_ref[pl.ds(base, CHUNK), :] = out.astype(BF16)

        @pl.when(emit_ref[gi] >= 0)
        def _():
            sout_ref[emit_ref[gi]] = state[...]

        return carry

    lax.fori_loop(0, nch, chunk, 0)


def _ssd(proj, cache_slots, s0_slots, conv_w, conv_b, a_log, dt_bias, d_skip, norm_w, tables, n_out_blk):
    t = proj.shape[0]
    l = MIX_TILE
    nch = TILE_CHUNKS
    hp = 2 * SSD_HEADDIM
    kind, emit, in_blk, out_blk = tables
    cw = jnp.concatenate([conv_w, conv_b[None, :], jnp.zeros((SUBLANES - CONV_K - 1, SSD_CONV), F32)], axis=0)
    a_neg = -jnp.exp(a_log.astype(F32))
    pre = jnp.zeros((SM_DT,), F32)
    post = jnp.zeros((LANES - SM_DT - SSD_HEADS,), F32)
    parr = jnp.zeros((SUBLANES, LANES), F32)
    parr = parr.at[0].set(jnp.concatenate([pre, a_neg, post]))
    parr = parr.at[1].set(jnp.concatenate([pre, dt_bias.astype(F32), post]))
    parc = jnp.broadcast_to(jnp.concatenate([a_neg, dt_bias.astype(F32)])[:, None], (2 * SSD_HEADS, LANES))
    dsk = jnp.repeat(d_skip.astype(F32), SSD_HEADDIM).reshape(1, SSD_WIDTH)
    grid_spec = pltpu.PrefetchScalarGridSpec(
        num_scalar_prefetch=4,
        grid=(t // l,),
        in_specs=[
            pl.BlockSpec((l, SSD_CONV), lambda i, *_: (i, P_XBC // SSD_CONV)),
            pl.BlockSpec((l, SSD_WIDTH), lambda i, *_: (i, P_ZC // SSD_WIDTH)),
            pl.BlockSpec((l, LANES), lambda i, *_: (i, P_SMALL // LANES)),
            pl.BlockSpec((nch, SUBLANES, SSD_CONV), lambda i, k, e, ib, ob: (ib[i], 0, 0)),
            pl.BlockSpec((nch, SSD_PAIRS, hp, SSD_STATE), lambda i, k, e, ib, ob: (ib[i], 0, 0, 0)),
            pl.BlockSpec((SUBLANES, SSD_CONV), lambda i, *_: (0, 0)),
            pl.BlockSpec((SUBLANES, LANES), lambda i, *_: (0, 0)),
            pl.BlockSpec((2 * SSD_HEADS, LANES), lambda i, *_: (0, 0)),
            pl.BlockSpec((1, SSD_WIDTH), lambda i, *_: (0, 0)),
            pl.BlockSpec((1, SSD_WIDTH), lambda i, *_: (0, 0)),
        ],
        out_specs=[
            pl.BlockSpec((l, SSD_WIDTH), lambda i, *_: (i, 0)),
            pl.BlockSpec((nch, SSD_PAIRS, hp, SSD_STATE), lambda i, k, e, ib, ob: (ob[i], 0, 0, 0)),
        ],
        scratch_shapes=[
            pltpu.VMEM((SUBLANES, SSD_CONV), F32),
            pltpu.VMEM((l, SSD_CONV), F32),
            pltpu.VMEM((2 * SUBLANES, SSD_CONV), F32),
            pltpu.VMEM((SSD_PAIRS, hp, SSD_STATE), F32),
            pltpu.VMEM((l, LANES), F32),
            pltpu.VMEM((l, LANES), F32),
            pltpu.VMEM((nch, SSD_PAIRS, hp), F32),
        ],
    )
    return pl.pallas_call(
        _ssd_kernel,
        grid_spec=grid_spec,
        out_shape=[jax.ShapeDtypeStruct((t, SSD_WIDTH), BF16),
                   jax.ShapeDtypeStruct((n_out_blk * nch, SSD_PAIRS, hp, SSD_STATE), F32)],
        compiler_params=_cparams(("arbitrary",)),
        name="ssd_mixer",
    )(jnp.asarray(kind), jnp.asarray(emit), jnp.asarray(in_blk), jnp.asarray(out_blk),
      proj, proj, proj, cache_slots, s0_slots, cw, parr, parc,
      norm_w.reshape(1, SSD_WIDTH).astype(F32), dsk)


S5_SB = 2
S5_LB = 512


def _s5_kernel(kind_ref, u_ref, h0r_ref, h0i_ref, perm_ref, permt_ref, bre_ref, bim_ref, cre_ref, cim_ref,
               ar_ref, ai_ref, dsk_ref, wglu_ref, bglu_ref,
               o_ref, hfr_ref, hfi_ref,
               bur, bui, pre, pim, cr, ci, inr, ini):
    i = pl.program_id(0)
    l = u_ref.shape[0]
    nch = l // CHUNK
    n = S5_N
    usb = S5_WIDTH // S5_SB
    nsb = n // S5_SB

    @pl.when(i == 0)
    def _():
        pre[0:1, :] = ar_ref[...]
        pim[0:1, :] = ai_ref[...]

        def pw(t, carry):
            pr = pre[pl.ds(t - 1, 1), :]
            pi = pim[pl.ds(t - 1, 1), :]
            pre[pl.ds(t, 1), :] = pr * ar_ref[...] - pi * ai_ref[...]
            pim[pl.ds(t, 1), :] = pr * ai_ref[...] + pi * ar_ref[...]
            return carry

        lax.fori_loop(1, CHUNK, pw, 0)
        cr[...] = jnp.zeros((1, n), F32)
        ci[...] = jnp.zeros((1, n), F32)

    up = _dot(perm_ref[...], u_ref[...].astype(BF16)).astype(BF16)
    for sb in range(S5_SB):
        us = up[:, sb * usb:(sb + 1) * usb]
        bur[:, sb * nsb:(sb + 1) * nsb] = _dot(us, bre_ref[sb])
        bui[:, sb * nsb:(sb + 1) * nsb] = _dot(us, bim_ref[sb])

    for c0 in range(0, n, S5_LB):
        a_r = jnp.broadcast_to(ar_ref[:, c0:c0 + S5_LB], (nch, S5_LB))
        a_i = jnp.broadcast_to(ai_ref[:, c0:c0 + S5_LB], (nch, S5_LB))

        def step(t, carry):
            hr, hi = carry
            r0 = pl.multiple_of(t * nch, nch)
            nr = a_r * hr - a_i * hi + bur[pl.ds(r0, nch), c0:c0 + S5_LB]
            ni = a_r * hi + a_i * hr + bui[pl.ds(r0, nch), c0:c0 + S5_LB]
            bur[pl.ds(r0, nch), c0:c0 + S5_LB] = nr
            bui[pl.ds(r0, nch), c0:c0 + S5_LB] = ni
            return nr, ni

        z = jnp.zeros((nch, S5_LB), F32)
        lax.fori_loop(0, CHUNK, step, (z, z))

    a64r = pre[CHUNK - 1:CHUNK, :]
    a64i = pim[CHUNK - 1:CHUNK, :]
    c_r = cr[...]
    c_i = ci[...]
    for s in range(nch):
        start = kind_ref[i * nch + s] == 1
        i_r = jnp.where(start, h0r_ref[s:s + 1, :], c_r)
        i_i = jnp.where(start, h0i_ref[s:s + 1, :], c_i)
        inr[s:s + 1, :] = i_r
        ini[s:s + 1, :] = i_i
        e_r = bur[l - nch + s:l - nch + s + 1, :]
        e_i = bui[l - nch + s:l - nch + s + 1, :]
        c_r = a64r * i_r - a64i * i_i + e_r
        c_i = a64r * i_i + a64i * i_r + e_i
        hfr_ref[s:s + 1, :] = c_r
        hfi_ref[s:s + 1, :] = c_i
    cr[...] = c_r
    ci[...] = c_i

    for c0 in range(0, n, S5_LB):
        n_r = inr[:, c0:c0 + S5_LB]
        n_i = ini[:, c0:c0 + S5_LB]

        def fix(t, carry):
            r0 = pl.multiple_of(t * nch, nch)
            p_r = pre[pl.ds(t, 1), c0:c0 + S5_LB]
            p_i = pim[pl.ds(t, 1), c0:c0 + S5_LB]
            bur[pl.ds(r0, nch), c0:c0 + S5_LB] += p_r * n_r - p_i * n_i
            bui[pl.ds(r0, nch), c0:c0 + S5_LB] += p_r * n_i + p_i * n_r
            return carry

        lax.fori_loop(0, CHUNK, fix, 0)

    ys = []
    for sb in range(S5_SB):
        hr = bur[:, sb * nsb:(sb + 1) * nsb].astype(BF16)
        hi = bui[:, sb * nsb:(sb + 1) * nsb].astype(BF16)
        ys.append(_dot(hr, cre_ref[sb]) - _dot(hi, cim_ref[sb]))
    yp = jnp.concatenate(ys, axis=1)
    y_hi = yp.astype(BF16)
    r1 = yp - y_hi.astype(F32)
    y_mid = r1.astype(BF16)
    y_lo = (r1 - y_mid.astype(F32)).astype(BF16)
    pt = permt_ref[...]
    y = (_dot(pt, y_hi) + _dot(pt, y_mid)) + _dot(pt, y_lo)
    y = y + u_ref[...] * dsk_ref[...]
    y = y * (0.5 * (1.0 + jnp.tanh(math.sqrt(2.0 / math.pi) * (y + 0.044715 * (y * y * y)))))
    out = y * jax.nn.sigmoid(_dot(y.astype(BF16), wglu_ref[...]) + bglu_ref[...])
    o_ref[...] = out.astype(BF16)


def _s5_tables(a_re, a_im, b_re, b_im, c_re, c_im, log_dt):
    a_re, a_im = a_re.astype(F32), a_im.astype(F32)
    dt = jnp.exp(log_dt.astype(F32))[:, None]
    mag = jnp.exp(dt * a_re)
    abar_re, abar_im = mag * jnp.cos(dt * a_im), mag * jnp.sin(dt * a_im)
    den = a_re * a_re + a_im * a_im
    num_re, num_im = abar_re - 1.0, abar_im
    zoh_re = (num_re * a_re + num_im * a_im) / den
    zoh_im = (num_im * a_re - num_re * a_im) / den
    b_re, b_im = b_re.astype(F32), b_im.astype(F32)
    bbar_re = zoh_re[..., None] * b_re - zoh_im[..., None] * b_im
    bbar_im = zoh_re[..., None] * b_im + zoh_im[..., None] * b_re
    gsb = S5_GROUPS // S5_SB
    eye = jnp.eye(gsb, dtype=F32)

    def bblk(b):
        b = b.reshape(S5_SB, gsb, S5_STATE, S5_GROUP)
        return jnp.einsum('sgpc,gh->sgchp', b, eye).reshape(S5_SB, gsb * S5_GROUP, gsb * S5_STATE).astype(BF16)

    def cblk(c):
        c = c.astype(F32).reshape(S5_SB, gsb, S5_GROUP, S5_STATE)
        return jnp.einsum('sgcp,gh->sgphc', c, eye).reshape(S5_SB, gsb * S5_STATE, gsb * S5_GROUP).astype(BF16)

    return (abar_re.reshape(1, S5_N), abar_im.reshape(1, S5_N),
            bblk(bbar_re), bblk(bbar_im), cblk(c_re), cblk(c_im))


def _s5_perm():
    r_new = np.arange(MIX_TILE)
    r_old = (r_new % TILE_CHUNKS) * CHUNK + r_new // TILE_CHUNKS
    p = np.zeros((MIX_TILE, MIX_TILE), np.float32)
    p[r_new, r_old] = 1.0
    return jnp.asarray(p, BF16), jnp.asarray(p.T, BF16)


def _s5(proj, h0r, h0i, tabs, d_skip, w_glu, b_glu, kind):
    t = proj.shape[0]
    l = MIX_TILE
    nch = TILE_CHUNKS
    abr, abi, bre, bim, cre, cim = tabs
    perm, permt = _s5_perm()
    full2 = lambda a: pl.BlockSpec(a.shape, lambda i, *_: (0, 0))
    full3 = lambda a: pl.BlockSpec(a.shape, lambda i, *_: (0, 0, 0))
    dsk = d_skip.astype(F32).reshape(1, S5_WIDTH)
    wg = w_glu.astype(BF16)
    bg = b_glu.astype(F32).reshape(1, S5_WIDTH)
    grid_spec = pltpu.PrefetchScalarGridSpec(
        num_scalar_prefetch=1,
        grid=(t // l,),
        in_specs=[
            pl.BlockSpec((l, S5_WIDTH), lambda i, *_: (i, P_UB // S5_WIDTH)),
            pl.BlockSpec((nch, S5_N), lambda i, *_: (i, 0)),
            pl.BlockSpec((nch, S5_N), lambda i, *_: (i, 0)),
            full2(perm), full2(permt), full3(bre), full3(bim), full3(cre), full3(cim),
            full2(abr), full2(abi), full2(dsk), full2(wg), full2(bg),
        ],
        out_specs=[
            pl.BlockSpec((l, S5_WIDTH), lambda i, *_: (i, 0)),
            pl.BlockSpec((nch, S5_N), lambda i, *_: (i, 0)),
            pl.BlockSpec((nch, S5_N), lambda i, *_: (i, 0)),
        ],
        scratch_shapes=[
            pltpu.VMEM((l, S5_N), F32), pltpu.VMEM((l, S5_N), F32),
            pltpu.VMEM((CHUNK, S5_N), F32), pltpu.VMEM((CHUNK, S5_N), F32),
            pltpu.VMEM((1, S5_N), F32), pltpu.VMEM((1, S5_N), F32),
            pltpu.VMEM((nch, S5_N), F32), pltpu.VMEM((nch, S5_N), F32),
        ],
    )
    nseg = t // CHUNK
    return pl.pallas_call(
        _s5_kernel,
        grid_spec=grid_spec,
        out_shape=[jax.ShapeDtypeStruct((t, S5_WIDTH), BF16),
                   jax.ShapeDtypeStruct((nseg, S5_N), F32),
                   jax.ShapeDtypeStruct((nseg, S5_N), F32)],
        compiler_params=_cparams(("arbitrary",)),
        name="s5_mixer",
    )(jnp.asarray(kind), proj, h0r, h0i, perm, permt, bre, bim, cre, cim, abr, abi, dsk, wg, bg)


MIXOUT_TM = 512
R_GRP = 0
R_EXP = SUBLANES
NEG_BIG = -1e30


def _mixout_kernel(oa_ref, ob_ref, oc_ref, w_ref, x_ref, nw_ref, wr_ref, rb_ref, x1_ref, h2_ref, lg_ref):
    acc = _dot(oa_ref[...], w_ref[0:GDN_KD, :])
    acc = acc + _dot(ob_ref[...], w_ref[GDN_KD:GDN_KD + S5_WIDTH, :])
    acc = acc + _dot(oc_ref[...], w_ref[GDN_KD + S5_WIDTH:, :])
    x1 = x_ref[...] + acc
    x1_ref[...] = x1
    h = x1 * lax.rsqrt(jnp.mean(x1 * x1, axis=-1, keepdims=True) + EPS) * nw_ref[...]
    hb = h.astype(BF16)
    h2_ref[...] = _pack_bf16_pairs(hb)
    lg_ref[...] = _dot(hb, wr_ref[...]) + rb_ref[...]


def _mixout(oa, ob, oc, w_out, x, nw, wr, rb):
    t, d = x.shape
    tm = MIXOUT_TM
    row = lambda w: pl.BlockSpec((tm, w), lambda i: (i, 0))
    full = lambda a: pl.BlockSpec(a.shape, lambda i: (0, 0))
    return pl.pallas_call(
        _mixout_kernel,
        grid=(t // tm,),
        in_specs=[row(GDN_KD), row(S5_WIDTH), row(SSD_WIDTH), full(w_out), row(d), full(nw), full(wr), full(rb)],
        out_specs=[row(d), row(d // 2), row(LANES)],
        out_shape=[jax.ShapeDtypeStruct((t, d), F32), jax.ShapeDtypeStruct((t, d // 2), U32),
                   jax.ShapeDtypeStruct((t, LANES), F32)],
        compiler_params=_cparams(("parallel",)),
        name="mix_out",
    )(oa, ob, oc, w_out, x, nw, wr, rb)


def _router_weights(rg_w, rg_b, re_w, re_b):
    d = rg_w.shape[0]
    wr = jnp.concatenate([rg_w, jnp.zeros((d, R_EXP - N_GROUPS), F32), re_w,
                          jnp.zeros((d, LANES - R_EXP - N_EXPERTS), F32)], axis=1).astype(BF16)
    rb = jnp.concatenate([rg_b.astype(F32), jnp.full((R_EXP - N_GROUPS,), NEG_BIG, F32), re_b.astype(F32),
                          jnp.zeros((LANES - R_EXP - N_EXPERTS,), F32)]).reshape(1, LANES)
    return wr, rb


ROUTE_TM = 512


def _router_kernel(lg_ref, tri_ref, idx_ref, gate_ref, cnt_ref, run):
    i = pl.program_id(0)
    tm = lg_ref.shape[0]

    @pl.when(i == 0)
    def _():
        run[...] = jnp.zeros(run.shape, F32)

    lt = lg_ref[...].T
    row8 = lax.broadcasted_iota(I32, (SUBLANES, tm), 0)
    grp = lt[R_GRP:R_GRP + SUBLANES, :]
    gm = jnp.max(grp, axis=0, keepdims=True)
    gp_top = 1.0 / jnp.sum(jnp.exp(grp - gm), axis=0, keepdims=True)
    g_top = jnp.min(jnp.where(grp == gm, row8, SUBLANES), axis=0, keepdims=True)
    ing = jnp.zeros((EPG, tm), F32)
    for g in range(N_GROUPS):
        ing = jnp.where(g_top == g, lt[R_EXP + g * EPG:R_EXP + (g + 1) * EPG, :], ing)
    em = jnp.max(ing, axis=0, keepdims=True)
    ee = jnp.exp(ing - em)
    p = ee / jnp.sum(ee, axis=0, keepdims=True)
    v1 = jnp.max(p, axis=0, keepdims=True)
    i1 = jnp.min(jnp.where(p == v1, row8, EPG), axis=0, keepdims=True)
    p2 = jnp.where(row8 == i1, -1.0, p)
    v2 = jnp.max(p2, axis=0, keepdims=True)
    i2 = jnp.min(jnp.where(p2 == v2, row8, EPG), axis=0, keepdims=True)
    den = v1 + v2
    gate1 = gp_top * v1 / den
    gate2 = gp_top * v2 / den
    e1 = g_top * EPG + i1
    e2 = g_top * EPG + i2

    erow = lax.broadcasted_iota(I32, (N_EXPERTS, tm), 0)
    hit1 = erow == e1
    hit2 = erow == e2
    oh = jnp.where(hit1 | hit2, 1.0, 0.0)
    before = _dot(oh.astype(BF16), tri_ref[...]) + run[:, 0:1]
    rank1 = jnp.sum(jnp.where(hit1, before, 0.0), axis=0, keepdims=True).astype(I32)
    rank2 = jnp.sum(jnp.where(hit2, before, 0.0), axis=0, keepdims=True).astype(I32)
    run[...] = run[...] + jnp.sum(oh, axis=1, keepdims=True)
    cnt_ref[...] = run[...].astype(I32)

    zi = jnp.zeros((SUBLANES - 4, tm), I32)
    idx_ref[...] = jnp.concatenate([e1, e2, rank1, rank2, zi], axis=0)
    r128 = lax.broadcasted_iota(I32, (LANES, tm), 0)
    gt = jnp.where(r128 == 0, gate1, jnp.where(r128 == 1, gate2, 0.0))
    gate_ref[...] = gt.T


def _router(logits):
    t = logits.shape[0]
    tm = ROUTE_TM
    tri = jnp.asarray(np.triu(np.ones((tm, tm), np.float32), 1), BF16)
    return pl.pallas_call(
        _router_kernel,
        grid=(t // tm,),
        in_specs=[pl.BlockSpec((tm, LANES), lambda i: (i, 0)),
                  pl.BlockSpec((tm, tm), lambda i: (0, 0))],
        out_specs=[pl.BlockSpec((SUBLANES, tm), lambda i: (0, i)),
                   pl.BlockSpec((tm, LANES), lambda i: (i, 0)),
                   pl.BlockSpec((N_EXPERTS, LANES), lambda i: (0, 0))],
        out_shape=[jax.ShapeDtypeStruct((SUBLANES, t), I32),
                   jax.ShapeDtypeStruct((t, LANES), F32),
                   jax.ShapeDtypeStruct((N_EXPERTS, LANES), I32)],
        scratch_shapes=[pltpu.VMEM((N_EXPERTS, LANES), F32)],
        compiler_params=_cparams(("arbitrary",)),
        name="router",
    )(logits, tri)


MOE_BLOCK = 512
MOE_BLOCK_SHIFT = 9
DMA_UNROLL = 16
TOP_K = 2
DISPATCH_TM = 512
COMBINE_TM = 256


def _dispatch_kernel(pstart_ref, cnt_ref, nv_ref, idx_ref, h2_ref, xs_ref, zbuf, sem):
    i = pl.program_id(0)
    tm = idx_ref.shape[1]
    nb = xs_ref.shape[0] // MOE_BLOCK

    @pl.when(i == 0)
    def _():
        zbuf[...] = jnp.zeros(zbuf.shape, zbuf.dtype)

        def pad_copy(e, r):
            return pltpu.make_async_copy(zbuf.at[pl.ds(0, 1)], xs_ref.at[pl.ds(pstart_ref[e] + r, 1)], sem)

        def per_expert(e, carry):
            n = cnt_ref[e]
            padded = ((n + MOE_BLOCK - 1) >> MOE_BLOCK_SHIFT) << MOE_BLOCK_SHIFT
            lax.fori_loop(n, padded, lambda r, c: (pad_copy(e, r).start(), c)[1], 0)
            lax.fori_loop(n, padded, lambda r, c: (pad_copy(e, r).wait(), c)[1], 0)
            return carry

        lax.fori_loop(0, N_EXPERTS, per_expert, 0)

        def blk_copy(b):
            return pltpu.make_async_copy(zbuf, xs_ref.at[pl.ds(b * MOE_BLOCK, MOE_BLOCK)], sem)

        lax.fori_loop(nv_ref[0], nb, lambda b, c: (blk_copy(b).start(), c)[1], 0)
        lax.fori_loop(nv_ref[0], nb, lambda b, c: (blk_copy(b).wait(), c)[1], 0)

    def copy(t, k):
        slot = idx_ref[k, t]
        return pltpu.make_async_copy(h2_ref.at[pl.ds(t, 1)], xs_ref.at[pl.ds(slot, 1)], sem)

    def issue(t, carry):
        copy(t, 0).start(priority=0)
        copy(t, 1).start(priority=1)
        return carry

    def drain(t, carry):
        copy(t, 0).wait()
        copy(t, 1).wait()
        return carry

    lax.fori_loop(0, tm, issue, 0, unroll=DMA_UNROLL)
    lax.fori_loop(0, tm, drain, 0, unroll=DMA_UNROLL)


def _dispatch(pad_start, counts, n_valid, idx, h2, n_slots):
    t, d = h2.shape
    tm = DISPATCH_TM
    grid_spec = pltpu.PrefetchScalarGridSpec(
        num_scalar_prefetch=3,
        grid=(t // tm,),
        in_specs=[pl.BlockSpec((SUBLANES, tm), lambda i, *_: (0, i), memory_space=pltpu.SMEM),
                  pl.BlockSpec((tm, d), lambda i, *_: (i, 0))],
        out_specs=pl.BlockSpec(memory_space=pl.ANY),
        scratch_shapes=[pltpu.VMEM((MOE_BLOCK, d), h2.dtype), pltpu.SemaphoreType.DMA(())],
    )
    return pl.pallas_call(
        _dispatch_kernel,
        grid_spec=grid_spec,
        out_shape=jax.ShapeDtypeStruct((n_slots, d), h2.dtype),
        compiler_params=_cparams(("arbitrary",)),
        name="moe_dispatch",
    )(pad_start, counts, n_valid, idx, h2)


def _expert_kernel(be_ref, nv_ref, xs_ref, wg_ref, wu_ref, wd_ref, ys_ref, wg_s, wu_s, wd_s):
    b = pl.program_id(0)
    valid = b < nv_ref[0]

    @pl.when(valid & ((b == 0) | (be_ref[b] != be_ref[jnp.maximum(b - 1, 0)])))
    def _():
        wg_s[...] = wg_ref[0].astype(BF16)
        wu_s[...] = wu_ref[0].astype(BF16)
        wd_s[...] = wd_ref[0].astype(BF16)

    @pl.when(valid)
    def _():
        x_lo, x_hi = _unpack_bf16_pairs(xs_ref[...])
        kh = wg_s.shape[0] // 2
        g = _dot(x_lo, wg_s[0:kh, :]) + _dot(x_hi, wg_s[kh:, :])
        u = _dot(x_lo, wu_s[0:kh, :]) + _dot(x_hi, wu_s[kh:, :])
        h = (_silu(g) * u).astype(BF16)
        ys_ref[...] = _dot(h, wd_s[...])

    @pl.when(jnp.logical_not(valid))
    def _():
        ys_ref[...] = jnp.zeros(ys_ref.shape, F32)


def _experts(block_expert, n_valid, xs, wg, wu, wd, layer):
    n_slots, dp = xs.shape
    d, de = wg.shape[2], wg.shape[3]
    assert d == 2 * dp
    nb = n_slots // MOE_BLOCK
    blk = lambda b, be, nv: (jnp.minimum(b, nv[0] - 1), 0)
    wsel = lambda b, be, nv: (layer, be[jnp.minimum(b, nv[0] - 1)], 0, 0)
    grid_spec = pltpu.PrefetchScalarGridSpec(
        num_scalar_prefetch=2,
        grid=(nb,),
        in_specs=[pl.BlockSpec((MOE_BLOCK, dp), blk),
                  pl.BlockSpec((None, 1, d, de), wsel),
                  pl.BlockSpec((None, 1, d, de), wsel),
                  pl.BlockSpec((None, 1, de, d), wsel)],
        out_specs=pl.BlockSpec((MOE_BLOCK, d), lambda b, be, nv: (b, 0)),
        scratch_shapes=[pltpu.VMEM((d, de), BF16), pltpu.VMEM((d, de), BF16), pltpu.VMEM((de, d), BF16)],
    )
    return pl.pallas_call(
        _expert_kernel,
        grid_spec=grid_spec,
        out_shape=jax.ShapeDtypeStruct((n_slots, d), F32),
        compiler_params=_cparams(("arbitrary",)),
        name="moe_experts",
    )(block_expert, n_valid, xs, wg, wu, wd)


def _combine_kernel(idx_ref, x1_ref, gate_ref, ys_ref, nw_ref, *rest, n_first):
    if n_first is None:
        out_ref, ybuf, sem = rest
    else:
        out_a_ref, out_b_ref, ybuf, sem = rest
    tm = x1_ref.shape[0]

    def copy(t, k):
        slot = idx_ref[k, t]
        return pltpu.make_async_copy(ys_ref.at[pl.ds(slot, 1)], ybuf.at[k, pl.ds(t, 1)], sem)

    def issue(t, carry):
        copy(t, 0).start(priority=0)
        copy(t, 1).start(priority=1)
        return carry

    def drain(t, carry):
        copy(t, 0).wait()
        copy(t, 1).wait()
        return carry

    lax.fori_loop(0, tm, issue, 0, unroll=DMA_UNROLL)
    lax.fori_loop(0, tm, drain, 0, unroll=DMA_UNROLL)
    g = gate_ref[...]
    y = ybuf[0] * g[:, 0:1] + ybuf[1] * g[:, 1:2]
    x2 = x1_ref[...] + y
    if n_first is None:
        out_ref[...] = x2
    else:
        x2 = x2 * lax.rsqrt(jnp.mean(x2 * x2, axis=-1, keepdims=True) + EPS) * nw_ref[...]
        i = pl.program_id(0)

        @pl.when(i < n_first)
        def _():
            out_a_ref[...] = x2

        @pl.when(i >= n_first)
        def _():
            out_b_ref[...] = x2


def _combine(idx, x1, gates, ys, nw, t_first):
    t, d = x1.shape
    tm = COMBINE_TM
    row = pl.BlockSpec((tm, d), lambda i, *_: (i, 0))
    if t_first is None:
        n_first = None
        out_specs = row
        out_shape = jax.ShapeDtypeStruct((t, d), F32)
    else:
        assert t_first % tm == 0
        n_first = t_first // tm
        out_specs = [pl.BlockSpec((tm, d), lambda i, *_: (jnp.minimum(i, n_first - 1), 0)),
                     pl.BlockSpec((tm, d), lambda i, *_: (jnp.maximum(i - n_first, 0), 0))]
        out_shape = [jax.ShapeDtypeStruct((t_first, d), F32), jax.ShapeDtypeStruct((t - t_first, d), F32)]
    grid_spec = pltpu.PrefetchScalarGridSpec(
        num_scalar_prefetch=0,
        grid=(t // tm,),
        in_specs=[pl.BlockSpec((SUBLANES, tm), lambda i, *_: (0, i), memory_space=pltpu.SMEM),
                  row,
                  pl.BlockSpec((tm, LANES), lambda i, *_: (i, 0)),
                  pl.BlockSpec(memory_space=pl.ANY),
                  pl.BlockSpec((1, d), lambda i, *_: (0, 0))],
        out_specs=out_specs,
        scratch_shapes=[pltpu.VMEM((2, tm, d), F32), pltpu.SemaphoreType.DMA(())],
    )
    return pl.pallas_call(
        functools.partial(_combine_kernel, n_first=n_first),
        grid_spec=grid_spec,
        out_shape=out_shape,
        compiler_params=_cparams(("arbitrary",)),
        name="moe_combine",
    )(idx, x1, gates, ys, nw)


def _moe(x1, h2, logits, wg, wu, wd, layer, norm_final, t_first):
    t, d = x1.shape
    idx, gates, cnt = _router(logits)
    counts = cnt[:, 0]
    padded = ((counts + MOE_BLOCK - 1) >> MOE_BLOCK_SHIFT) << MOE_BLOCK_SHIFT
    pad_end = jnp.cumsum(padded)
    pad_start = (pad_end - padded).astype(I32)
    nb = (2 * t + N_EXPERTS * (MOE_BLOCK - 1) + MOE_BLOCK - 1) // MOE_BLOCK
    n_valid = (pad_end[-1] >> MOE_BLOCK_SHIFT).astype(I32).reshape(1)
    starts = jnp.arange(nb, dtype=I32) * MOE_BLOCK
    block_expert = jnp.minimum(jnp.sum((pad_end[None, :] <= starts[:, None]).astype(I32), axis=1),
                               N_EXPERTS - 1).astype(I32)
    hit = idx[None, 0:TOP_K, :] == jnp.arange(N_EXPERTS, dtype=I32)[:, None, None]
    slots = jnp.sum(jnp.where(hit, pad_start[:, None, None], 0), axis=0) + idx[TOP_K:2 * TOP_K]
    slots = jnp.concatenate([slots, jnp.zeros((SUBLANES - TOP_K, t), I32)], axis=0)
    xs = _dispatch(pad_start, counts, n_valid, slots, h2, nb * MOE_BLOCK)
    ys = _experts(block_expert, n_valid, xs, wg, wu, wd, layer)
    return _combine(slots, x1, gates, ys, norm_final, t_first)


def _stream_ends(nbp, seq, nbs, dseq):
    ends = [(b + 1) * seq for b in range(nbp)] + [nbp * seq + (s + 1) * dseq for s in range(nbs)]
    return np.asarray(ends)


def kernel(x_prompt, x_sample, cache_conv_gdn, state_gdn, state_s5, cache_conv_ssd, state_ssd, norm_mix, w_in, gdn_conv_w, gdn_a_log, gdn_dt_bias, gdn_norm, s5_a_re, s5_a_im, s5_b_re, s5_b_im, s5_c_re, s5_c_im, s5_log_dt, s5_d, s5_w_glu, s5_b_glu, ssd_conv_w, ssd_conv_b, ssd_a_log, ssd_dt_bias, ssd_d, ssd_norm, w_out, norm_ffn, router_group_w, router_group_b, router_expert_w, router_expert_b, expert_w_gate, expert_w_up, expert_w_down, norm_final):
    nbp, seq, d = x_prompt.shape
    nbs, dseq, _ = x_sample.shape
    depth = w_in.shape[0]
    tp = nbp * seq
    t = tp + nbs * dseq
    x = jnp.concatenate([x_prompt.reshape(tp, d), x_sample.reshape(nbs * dseq, d)], axis=0)

    tables = _chunk_tables(nbp, seq, nbs, dseq)
    kind = tables[0]
    n_out_blk = nbp + nbs // TILE_CHUNKS
    ends = _stream_ends(nbp, seq, nbs, dseq)
    tail_rows = (ends[:, None] + np.arange(-(CONV_K - 1), 0)[None, :]).reshape(-1)
    end_seg = ends // CHUNK - 1
    ncp = tp // CHUNK
    state_rows = np.concatenate([np.arange(nbp) * TILE_CHUNKS, nbp * TILE_CHUNKS + np.arange(nbs)])

    new_conv_gdn, new_gdn, new_s5, new_conv_ssd, new_ssd = [], [], [], [], []
    for l in range(depth):
        proj = _proj(x, norm_mix[l].reshape(1, d).astype(F32), _rearrange_w_in(w_in[l]))

        oa, sg = _gdn(proj, _conv_cache_slots(cache_conv_gdn[l].astype(F32)),
                      _init_slots(state_gdn[l].astype(F32)), gdn_conv_w[l].astype(F32), gdn_a_log[l],
                      gdn_dt_bias[l], gdn_norm[l], tables, n_out_blk)

        h0 = state_s5[l].astype(F32).reshape(nbs, S5_N, 2)
        zeros_p = jnp.zeros((ncp, S5_N), F32)
        ob, hfr, hfi = _s5(proj, jnp.concatenate([zeros_p, h0[..., 0]], axis=0),
                           jnp.concatenate([zeros_p, h0[..., 1]], axis=0),
                           _s5_tables(s5_a_re[l], s5_a_im[l], s5_b_re[l], s5_b_im[l], s5_c_re[l], s5_c_im[l],
                                      s5_log_dt[l]),
                           s5_d[l], s5_w_glu[l], s5_b_glu[l], kind)

        oc, ss = _ssd(proj, _conv_cache_slots(cache_conv_ssd[l].astype(F32)),
                      _init_slots(state_ssd[l].astype(F32).reshape(nbs, SSD_PAIRS, 2 * SSD_HEADDIM, SSD_STATE)),
                      ssd_conv_w[l].astype(F32), ssd_conv_b[l].astype(F32), ssd_a_log[l], ssd_dt_bias[l],
                      ssd_d[l], ssd_norm[l], tables, n_out_blk)

        wr, rb = _router_weights(router_group_w[l].astype(F32), router_group_b[l],
                                 router_expert_w[l].astype(F32), router_expert_b[l])
        x1, h2, logits = _mixout(oa, ob, oc, w_out[l].astype(BF16), x,
                                 norm_ffn[l].reshape(1, d).astype(F32), wr, rb)
        x = _moe(x1, h2, logits, expert_w_gate, expert_w_up, expert_w_down, l,
                 norm_final.reshape(1, d).astype(F32), tp if l == depth - 1 else None)

        tails = proj[tail_rows]
        new_conv_gdn.append(tails[:, P_QKV:P_QKV + GDN_CONV].reshape(nbp + nbs, CONV_K - 1, GDN_CONV))
        new_conv_ssd.append(tails[:, P_XBC:P_XBC + SSD_CONV].reshape(nbp + nbs, CONV_K - 1, SSD_CONV))
        new_gdn.append(sg[state_rows])
        new_ssd.append(ss[state_rows].reshape(nbp + nbs, SSD_HEADS, SSD_HEADDIM, SSD_STATE))
        new_s5.append(jnp.stack([hfr[end_seg], hfi[end_seg]], axis=-1)
                      .reshape(nbp + nbs, S5_GROUPS, S5_STATE, 2))

    def split(parts):
        a = jnp.stack(parts)
        return a[:, :nbp], a[:, nbp:]

    cg_p, cg_s = split(new_conv_gdn)
    sg_p, sg_s = split(new_gdn)
    s5_p, s5_s = split(new_s5)
    cs_p, cs_s = split(new_conv_ssd)
    ss_p, ss_s = split(new_ssd)
    y_prompt = x[0].reshape(nbp, seq, d)
    y_sample = x[1].reshape(nbs, dseq, d)
    return (y_prompt, y_sample, cg_p, sg_p, s5_p, cs_p, ss_p, cg_s, sg_s, s5_s, cs_s, ss_s)
```

```python
import functools
import math

import numpy as np
import jax
import jax.numpy as jnp
from jax import lax
from jax.experimental import pallas as pl
from jax.experimental.pallas import tpu as pltpu

F32 = jnp.float32
BF16 = jnp.bfloat16
I32 = jnp.int32
U32 = jnp.uint32

EPS = 1e-6
CHUNK = 64
CHUNK_SHIFT = 6
CONV_K = 4
LANES = 128
SUBLANES = 8
VMEM_LIMIT = 56 * 1024 * 1024

GDN_HEADS = 8
GDN_DK = 128
GDN_DV = 128
GDN_KD = GDN_HEADS * GDN_DK
GDN_CONV = 3 * GDN_KD
GDN_GROUP = 4
GDN_SPLIT_STEPS = 2
S5_WIDTH = 512
S5_GROUPS = 32
S5_GROUP = 16
S5_STATE = 64
S5_N = S5_GROUPS * S5_STATE
SSD_WIDTH = 512
SSD_HEADS = 8
SSD_HEADDIM = 64
SSD_NGROUPS = 2
SSD_STATE = 128
SSD_CONV = SSD_WIDTH + 2 * SSD_NGROUPS * SSD_STATE
N_GROUPS = 4
EPG = 8
N_EXPERTS = 32

MIX_TILE = 512
TILE_CHUNKS = MIX_TILE // CHUNK


def _cparams(sem, vmem=VMEM_LIMIT):
    return pltpu.CompilerParams(dimension_semantics=sem, vmem_limit_bytes=vmem)


def _silu(x):
    return x * jax.nn.sigmoid(x)


def _softplus(x):
    return jnp.maximum(x, 0.0) + jnp.log1p(jnp.exp(-jnp.abs(x)))


def _dot(a, b):
    return jnp.dot(a, b, preferred_element_type=F32)


def _dot3(a, b):
    ah = a.astype(BF16)
    bh = b.astype(BF16)
    al = (a - ah.astype(F32)).astype(BF16)
    bl = (b - bh.astype(F32)).astype(BF16)
    return _dot(ah, bh) + (_dot(ah, bl) + _dot(al, bh))


def _pack_bf16_pairs(xb):
    n = xb.shape[1] // 2
    lo = lax.bitcast_convert_type(xb[:, :n].astype(F32), U32)
    hi = lax.bitcast_convert_type(xb[:, n:].astype(F32), U32)
    return (hi & jnp.uint32(0xFFFF0000)) | (lo >> 16)


def _unpack_bf16_pairs(w):
    lo = lax.bitcast_convert_type(w << 16, F32).astype(BF16)
    hi = lax.bitcast_convert_type(w & jnp.uint32(0xFFFF0000), F32).astype(BF16)
    return lo, hi


def _dot_nt(a, b):
    return lax.dot_general(a, b, (((1,), (1,)), ((), ())), preferred_element_type=F32)


def _dot_tn(a, b):
    return lax.dot_general(a, b, (((0,), (0,)), ((), ())), preferred_element_type=F32)


def _cumsum_rows(x):
    row = lax.broadcasted_iota(I32, x.shape, 0) & (CHUNK - 1)
    k = 1
    while k < CHUNK:
        x = x + jnp.where(row >= k, pltpu.roll(x, k, 0), 0.0)
        k *= 2
    return x


def _cumsum_lanes(x):
    lane = lax.broadcasted_iota(I32, x.shape, 1) & (CHUNK - 1)
    k = 1
    while k < CHUNK:
        x = x + jnp.where(lane >= k, pltpu.roll(x, k, 1), 0.0)
        k *= 2
    return x


PROJ_TM = 1024
PROJ_TN = 1280
P_QKV, P_ZA, P_UB, P_ZC, P_XBC = 0, 3072, 4096, 4608, 5120
P_MAIN = 6144
P_TOTAL = 6400
P_SMALL = P_TOTAL - LANES


def _proj_kernel(x_ref, nw_ref, w_ref, o_ref, h_scr):
    @pl.when(pl.program_id(1) == 0)
    def _():
        x = x_ref[...]
        ms = jnp.mean(x * x, axis=-1, keepdims=True)
        h_scr[...] = (x * lax.rsqrt(ms + EPS) * nw_ref[...]).astype(BF16)

    o_ref[...] = _dot(h_scr[...], w_ref[...])


def _proj(x, nw, w):
    t, d = x.shape
    n = w.shape[1]
    tm = min(PROJ_TM, t)
    assert t % tm == 0 and n % PROJ_TN == 0
    return pl.pallas_call(
        _proj_kernel,
        grid=(t // tm, n // PROJ_TN),
        in_specs=[pl.BlockSpec((tm, d), lambda i, j: (i, 0)),
                  pl.BlockSpec((1, d), lambda i, j: (0, 0)),
                  pl.BlockSpec((d, PROJ_TN), lambda i, j: (0, j))],
        out_specs=pl.BlockSpec((tm, PROJ_TN), lambda i, j: (i, j)),
        out_shape=jax.ShapeDtypeStruct((t, n), F32),
        scratch_shapes=[pltpu.VMEM((tm, d), BF16)],
        compiler_params=_cparams(("parallel", "arbitrary")),
        name="proj_in",
    )(x, nw, w)


def _rearrange_w_in(w_in):
    d = w_in.shape[0]
    off_za = GDN_CONV
    off_ba = off_za + GDN_KD
    off_s5 = off_ba + 2 * GDN_HEADS
    off_zc = off_s5 + S5_WIDTH
    off_xbc = off_zc + SSD_WIDTH
    off_dt = off_xbc + SSD_CONV
    small = jnp.concatenate([w_in[:, off_ba:off_s5], w_in[:, off_dt:],
                             jnp.zeros((d, LANES - 3 * GDN_HEADS), w_in.dtype)], axis=1)
    w = jnp.concatenate([w_in[:, :off_ba], w_in[:, off_s5:off_dt],
                         jnp.zeros((d, P_TOTAL - P_MAIN - LANES), w_in.dtype), small], axis=1)
    return w.astype(BF16)


def _chunk_tables(nbp, seq, nbs, dseq):
    assert seq % MIX_TILE == 0 and dseq == CHUNK and (nbs * dseq) % MIX_TILE == 0
    cps = seq // CHUNK
    ncp = nbp * cps
    nc = ncp + nbs
    kind = np.zeros((nc,), np.int32)
    emit = np.full((nc,), -1, np.int32)
    for c in range(nc):
        if c < ncp:
            kind[c] = 1 if c % cps == 0 else 0
            if c % TILE_CHUNKS == TILE_CHUNKS - 1:
                emit[c] = 0
        else:
            kind[c] = 1
            emit[c] = (c - ncp) % TILE_CHUNKS
    ntp = ncp // TILE_CHUNKS
    nts = nbs // TILE_CHUNKS
    in_blk = np.concatenate([np.zeros((ntp,), np.int32), 1 + np.arange(nts, dtype=np.int32)])
    out_blk = np.concatenate([np.arange(ntp, dtype=np.int32) // (cps // TILE_CHUNKS),
                              nbp + np.arange(nts, dtype=np.int32)])
    return kind, emit, in_blk, out_blk


def _init_slots(x):
    return jnp.concatenate([jnp.zeros((TILE_CHUNKS,) + x.shape[1:], x.dtype), x], axis=0)


def _conv_cache_slots(cache):
    nbs, k1, c = cache.shape
    padded = jnp.concatenate([jnp.zeros((nbs, SUBLANES - k1, c), cache.dtype), cache], axis=1)
    return _init_slots(padded)


def _conv_silu_tile(i, kind_ref, in_ref, cache_ref, cw_ref, tail, act, tmp, bias_row):
    l, c = in_ref.shape
    cb = 512
    k1 = CONV_K - 1
    nch = l // CHUNK

    @pl.when(i == 0)
    def _():
        tail[...] = jnp.zeros(tail.shape, F32)

    def taps(src, lo, hi, c0):
        acc = src[lo - k1:hi - k1, c0:c0 + cb] * cw_ref[0:1, c0:c0 + cb]
        for j in range(1, CONV_K):
            acc = acc + src[lo - k1 + j:hi - k1 + j, c0:c0 + cb] * cw_ref[j:j + 1, c0:c0 + cb]
        if bias_row is not None:
            acc = acc + cw_ref[bias_row:bias_row + 1, c0:c0 + cb]
        return _silu(acc)

    def head_rows(prev, r0):
        tmp[0:SUBLANES, :] = prev
        tmp[SUBLANES:2 * SUBLANES, :] = in_ref[r0:r0 + SUBLANES, :]
        for c0 in range(0, c, cb):
            act[r0:r0 + SUBLANES, c0:c0 + cb] = taps(tmp, SUBLANES, 2 * SUBLANES, c0)

    def taps_rolled(lo, hi, c0):
        a = max(lo - SUBLANES, 0)
        x = in_ref[a:hi, c0:c0 + cb]
        acc = x * cw_ref[0:1, c0:c0 + cb]
        for j in range(1, CONV_K):
            acc = pltpu.roll(acc, 1, 0) + x * cw_ref[j:j + 1, c0:c0 + cb]
        acc = acc[lo - a:, :]
        if bias_row is not None:
            acc = acc + cw_ref[bias_row:bias_row + 1, c0:c0 + cb]
        return _silu(acc)

    for rb in range(nch):
        for c0 in range(0, c, cb):
            act[rb * CHUNK:(rb + 1) * CHUNK, c0:c0 + cb] = taps_rolled(rb * CHUNK, (rb + 1) * CHUNK, c0)
    head_rows(tail[...], 0)

    for cl in range(nch):
        @pl.when(kind_ref[i * nch + cl] == 1)
        def _():
            head_rows(cache_ref[cl], cl * CHUNK)

    tail[...] = in_ref[l - SUBLANES:l, :]


def _gdn_kernel(kind_ref, emit_ref, inb_ref, outb_ref,
                qkv_ref, z_ref, sm_ref, cache_ref, s0_ref, cw_ref, parr_ref, parc_ref, nw_ref,
                o_ref, sout_ref,
                tail, act, tmp, state, gcol, beta_s, grow, u_s, w_s, qk_s, qd_s, kdt_s):
    del inb_ref, outb_ref
    i = pl.program_id(0)
    l = qkv_ref.shape[0]
    nch = l // CHUNK
    h_n, dk = GDN_HEADS, GDN_DK
    gh = GDN_GROUP
    ng = h_n // gh
    gr = gh * CHUNK

    _conv_silu_tile(i, kind_ref, qkv_ref, cache_ref, cw_ref, tail, act, tmp, None)

    sm = sm_ref[...]
    beta_s[...] = jax.nn.sigmoid(sm)
    g = parr_ref[0:1, :] * _softplus(sm + parr_ref[1:2, :])
    gcol[...] = _cumsum_rows(g)
    a_t = sm.T[h_n:2 * h_n, :]
    g_t = parc_ref[0:h_n, 0:1] * _softplus(a_t + parc_ref[h_n:2 * h_n, 0:1])
    g_t = _cumsum_lanes(g_t)
    for cl in range(nch):
        for h in range(h_n):
            grow[cl, h // gh:h // gh + 1, (h % gh) * CHUNK:(h % gh + 1) * CHUNK] = (
                g_t[h:h + 1, cl * CHUNK:(cl + 1) * CHUNK])

    sout_ref[1:, :, :, :] = jnp.zeros((nch - 1,) + tuple(sout_ref.shape[1:]), F32)

    ri = lax.broadcasted_iota(I32, (gr, gr), 0)
    ci = lax.broadcasted_iota(I32, (gr, gr), 1)
    same = (ri >> CHUNK_SHIFT) == (ci >> CHUNK_SHIFT)
    causal = same & (ri >= ci)
    strict = same & (ri > ci)

    def row0(c):
        return c * CHUNK if isinstance(c, int) else pl.multiple_of(c * CHUNK, CHUNK)

    def solve_chunk(c):
        base = row0(c)
        gc_blk = gcol[pl.ds(base, CHUNK), :]
        bt_blk = beta_s[pl.ds(base, CHUNK), :]
        for g in range(ng):
            qs, ks, vs, gcs, bts, gls = [], [], [], [], [], []
            for j in range(gh):
                h = g * gh + j
                q = act[pl.ds(base, CHUNK), h * dk:(h + 1) * dk]
                k = act[pl.ds(base, CHUNK), GDN_KD + h * dk:GDN_KD + (h + 1) * dk]
                qs.append(q * lax.rsqrt(jnp.sum(q * q, axis=-1, keepdims=True) + 1e-6) * (dk ** -0.5))
                ks.append(k * lax.rsqrt(jnp.sum(k * k, axis=-1, keepdims=True) + 1e-6))
                vs.append(act[pl.ds(base, CHUNK), 2 * GDN_KD + h * dk:2 * GDN_KD + (h + 1) * dk])
                gcs.append(gc_blk[:, h_n + h:h_n + h + 1])
                bts.append(bt_blk[:, h:h + 1])
                gls.append(jnp.broadcast_to(gc_blk[CHUNK - 1:CHUNK, h_n + h:h_n + h + 1], (CHUNK, 1)))
            q4 = jnp.concatenate(qs, axis=0)
            k4 = jnp.concatenate(ks, axis=0)
            v4 = jnp.concatenate(vs, axis=0)
            gc4 = jnp.concatenate(gcs, axis=0)
            bt4 = jnp.concatenate(bts, axis=0)
            gl4 = jnp.concatenate(gls, axis=0)
            eg4 = jnp.exp(gc4)
            decay = jnp.where(causal, jnp.exp(jnp.minimum(gc4 - grow[c, g:g + 1, :], 0.0)), 0.0)
            kb = k4.astype(BF16)
            xm = jnp.where(strict, -(bt4 * _dot_nt(kb, kb) * decay), 0.0)
            y = jnp.concatenate([v4 * bt4, k4 * (bt4 * eg4)], axis=1)
            p = xm
            for step in range(6):
                if step < GDN_SPLIT_STEPS:
                    y = y + _dot3(p, y)
                else:
                    y = y + _dot(p.astype(BF16), y.astype(BF16))
                if step < GDN_SPLIT_STEPS - 1:
                    p = _dot3(p, p)
                elif step < 5:
                    pb = p.astype(BF16)
                    p = _dot(pb, pb)
            idx = c * ng + g
            u_s[idx] = y[:, :GDN_DV]
            w_s[idx] = y[:, GDN_DV:].astype(BF16)
            qk_s[idx] = (_dot_nt(q4.astype(BF16), kb) * decay).astype(BF16)
            qd_s[idx] = (q4 * eg4).astype(BF16)
            kdt_s[idx] = (k4 * jnp.exp(gl4 - gc4)).T.astype(BF16)

    rblk = lax.broadcasted_iota(I32, (gr, GDN_DV), 0) >> CHUNK_SHIFT

    def recur_chunk(c, solve_next):
        base = row0(c)
        gi = i * nch + c

        @pl.when(kind_ref[gi] == 1)
        def _():
            for h in range(h_n):
                state[h // gh, :, (h % gh) * GDN_DV:(h % gh + 1) * GDN_DV] = s0_ref[c, h]

        last8 = base + CHUNK - SUBLANES
        if not isinstance(c, int):
            last8 = pl.multiple_of(last8, SUBLANES)
        gc_last = gcol[pl.ds(last8, SUBLANES), :][SUBLANES - 1:SUBLANES]
        for g in range(ng):
            idx = c * ng + g
            s4 = state[g]
            w4 = w_s[idx]
            qd4 = qd_s[idx]
            ws, os_ = [], []
            for j in range(gh):
                sb = s4[:, j * GDN_DV:(j + 1) * GDN_DV].astype(BF16)
                lhs = jnp.concatenate([w4[j * CHUNK:(j + 1) * CHUNK], qd4[j * CHUNK:(j + 1) * CHUNK]], axis=0)
                r = _dot(lhs, sb)
                ws.append(r[:CHUNK])
                os_.append(r[CHUNK:])
            v_new = u_s[idx] - jnp.concatenate(ws, axis=0)
            vb = v_new.astype(BF16)
            o4 = jnp.concatenate(os_, axis=0) + _dot(qk_s[idx], vb)
            vbd = jnp.concatenate([jnp.where(rblk == j, v_new, 0.0) for j in range(gh)], axis=1).astype(BF16)
            egl = jnp.concatenate(
                [jnp.broadcast_to(jnp.exp(gc_last[:, h_n + g * gh + j:h_n + g * gh + j + 1]), (1, GDN_DV))
                 for j in range(gh)], axis=1)
            state[g] = s4 * egl + _dot(kdt_s[idx], vbd)
            for j in range(gh):
                h = g * gh + j
                o = o4[j * CHUNK:(j + 1) * CHUNK]
                zz = z_ref[pl.ds(base, CHUNK), h * GDN_DV:(h + 1) * GDN_DV]
                on = o * lax.rsqrt(jnp.mean(o * o, axis=-1, keepdims=True) + EPS) * nw_ref[...] * _silu(zz)
                o_ref[pl.ds(base, CHUNK), h * GDN_DV:(h + 1) * GDN_DV] = on.astype(BF16)

        if solve_next:
            solve_chunk(c + 1)

        @pl.when(emit_ref[gi] >= 0)
        def _():
            for h in range(h_n):
                sout_ref[emit_ref[gi], h] = state[h // gh, :, (h % gh) * GDN_DV:(h % gh + 1) * GDN_DV]

    solve_chunk(0)
    lax.fori_loop(0, nch - 1, lambda c, carry: (recur_chunk(c, True), carry)[1], 0)
    recur_chunk(nch - 1, False)


def _gdn(proj, cache_slots, s0_slots, conv_w, a_log, dt_bias, norm_w, tables, n_out_blk):
    t = proj.shape[0]
    l = MIX_TILE
    nch = TILE_CHUNKS
    kind, emit, in_blk, out_blk = tables
    ng = GDN_HEADS // GDN_GROUP
    gr = GDN_GROUP * CHUNK
    cw = jnp.concatenate([conv_w, jnp.zeros((SUBLANES - CONV_K, GDN_CONV), F32)], axis=0)
    a_neg = -jnp.exp(a_log.astype(F32))
    lane_pad = jnp.zeros((LANES - 2 * GDN_HEADS,), F32)
    parr = jnp.zeros((SUBLANES, LANES), F32)
    parr = parr.at[0].set(jnp.concatenate([jnp.zeros((GDN_HEADS,), F32), a_neg, lane_pad]))
    parr = parr.at[1].set(jnp.concatenate([jnp.zeros((GDN_HEADS,), F32), dt_bias.astype(F32), lane_pad]))
    parc = jnp.broadcast_to(jnp.concatenate([a_neg, dt_bias.astype(F32)])[:, None], (2 * GDN_HEADS, LANES))
    grid_spec = pltpu.PrefetchScalarGridSpec(
        num_scalar_prefetch=4,
        grid=(t // l,),
        in_specs=[
            pl.BlockSpec((l, GDN_CONV), lambda i, *_: (i, 0)),
            pl.BlockSpec((l, GDN_KD), lambda i, *_: (i, P_ZA // GDN_KD)),
            pl.BlockSpec((l, LANES), lambda i, *_: (i, P_SMALL // LANES)),
            pl.BlockSpec((nch, SUBLANES, GDN_CONV), lambda i, k, e, ib, ob: (ib[i], 0, 0)),
            pl.BlockSpec((nch, GDN_HEADS, GDN_DK, GDN_DV), lambda i, k, e, ib, ob: (ib[i], 0, 0, 0)),
            pl.BlockSpec((SUBLANES, GDN_CONV), lambda i, *_: (0, 0)),
            pl.BlockSpec((SUBLANES, LANES), lambda i, *_: (0, 0)),
            pl.BlockSpec((2 * GDN_HEADS, LANES), lambda i, *_: (0, 0)),
            pl.BlockSpec((1, GDN_DV), lambda i, *_: (0, 0)),
        ],
        out_specs=[
            pl.BlockSpec((l, GDN_KD), lambda i, *_: (i, 0)),
            pl.BlockSpec((nch, GDN_HEADS, GDN_DK, GDN_DV), lambda i, k, e, ib, ob: (ob[i], 0, 0, 0)),
        ],
        scratch_shapes=[
            pltpu.VMEM((SUBLANES, GDN_CONV), F32),
            pltpu.VMEM((l, GDN_CONV), F32),
            pltpu.VMEM((2 * SUBLANES, GDN_CONV), F32),
            pltpu.VMEM((ng, GDN_DK, GDN_GROUP * GDN_DV), F32),
            pltpu.VMEM((l, LANES), F32),
            pltpu.VMEM((l, LANES), F32),
            pltpu.VMEM((nch, ng, gr), F32),
            pltpu.VMEM((nch * ng, gr, GDN_DV), F32),
            pltpu.VMEM((nch * ng, gr, GDN_DK), BF16),
            pltpu.VMEM((nch * ng, gr, gr), BF16),
            pltpu.VMEM((nch * ng, gr, GDN_DK), BF16),
            pltpu.VMEM((nch * ng, GDN_DK, gr), BF16),
        ],
    )
    return pl.pallas_call(
        _gdn_kernel,
        grid_spec=grid_spec,
        out_shape=[jax.ShapeDtypeStruct((t, GDN_KD), BF16),
                   jax.ShapeDtypeStruct((n_out_blk * nch, GDN_HEADS, GDN_DK, GDN_DV), F32)],
        compiler_params=_cparams(("arbitrary",)),
        name="gdn_mixer",
    )(jnp.asarray(kind), jnp.asarray(emit), jnp.asarray(in_blk), jnp.asarray(out_blk),
      proj, proj, proj, cache_slots, s0_slots, cw, parr, parc, norm_w.reshape(1, GDN_DV).astype(F32))


SSD_PAIRS = SSD_HEADS // 2
SM_DT = 2 * GDN_HEADS


def _ssd_kernel(kind_ref, emit_ref, inb_ref, outb_ref,
                xbc_ref, z_ref, sm_ref, cache_ref, s0_ref, cw_ref, parr_ref, parc_ref, nw_ref, dsk_ref,
                o_ref, sout_ref,
                tail, act, tmp, state, cscol, dtcol, csrow):
    del inb_ref, outb_ref
    i = pl.program_id(0)
    l = xbc_ref.shape[0]
    nch = l // CHUNK
    hp = 2 * SSD_HEADDIM

    _conv_silu_tile(i, kind_ref, xbc_ref, cache_ref, cw_ref, tail, act, tmp, CONV_K)

    sm = sm_ref[...]
    dtc = _softplus(sm + parr_ref[1:2, :])
    dtcol[...] = dtc
    cscol[...] = _cumsum_rows(parr_ref[0:1, :] * dtc)
    dt_t = _softplus(sm.T[SM_DT:SM_DT + SSD_HEADS, :] + parc_ref[SSD_HEADS:2 * SSD_HEADS, 0:1])
    cs_t = _cumsum_lanes(parc_ref[0:SSD_HEADS, 0:1] * dt_t)
    left_row = (lax.broadcasted_iota(I32, (1, l), 1) & (hp - 1)) < SSD_HEADDIM
    for p in range(SSD_PAIRS):
        ra = cs_t[2 * p:2 * p + 1, :]
        rb = cs_t[2 * p + 1:2 * p + 2, :]
        even = jnp.where(left_row, ra, pltpu.roll(rb, SSD_HEADDIM, 1))
        odd = jnp.where(left_row, pltpu.roll(ra, l - SSD_HEADDIM, 1), rb)
        for cl in range(nch):
            src = even if cl % 2 == 0 else odd
            v0 = (cl // 2) * hp
            csrow[cl, p:p + 1, :] = src[:, v0:v0 + hp]

    sout_ref[1:, :, :, :] = jnp.zeros((nch - 1,) + tuple(sout_ref.shape[1:]), F32)

    ri = lax.broadcasted_iota(I32, (CHUNK, hp), 0)
    li = lax.broadcasted_iota(I32, (CHUNK, hp), 1)
    left = li < SSD_HEADDIM
    causal = ri >= (li & (SSD_HEADDIM - 1))
    top = lax.broadcasted_iota(I32, (hp, 1), 0) < SSD_HEADDIM
    zpad_b = jnp.zeros((CHUNK, SSD_STATE), BF16)
    zpad_f = jnp.zeros((CHUNK, hp), F32)

    def chunk(c, carry):
        base = pl.multiple_of(c * CHUNK, CHUNK)
        gi = i * nch + c

        @pl.when(kind_ref[gi] == 1)
        def _():
            state[...] = s0_ref[c]

        dt_blk = dtcol[pl.ds(base, CHUNK), :]
        cs_blk = cscol[pl.ds(base, CHUNK), :]
        csr = csrow[c]
        ys = []
        for g in range(SSD_NGROUPS):
            b0 = SSD_WIDTH + g * SSD_STATE
            c0 = SSD_WIDTH + SSD_NGROUPS * SSD_STATE + g * SSD_STATE
            bf = act[pl.ds(base, CHUNK), b0:b0 + SSD_STATE]
            cf = act[pl.ds(base, CHUNK), c0:c0 + SSD_STATE]
            bg = bf.astype(BF16)
            cg = cf.astype(BF16)
            cbw = _dot_nt(cg, jnp.concatenate([bg, bg], axis=0))
            for q in range(SSD_PAIRS // SSD_NGROUPS):
                p = g * (SSD_PAIRS // SSD_NGROUPS) + q
                h0 = SM_DT + 2 * p
                xp = act[pl.ds(base, CHUNK), p * hp:(p + 1) * hp]
                dtp = jnp.where(left, dt_blk[:, h0:h0 + 1], dt_blk[:, h0 + 1:h0 + 2])
                csp = jnp.where(left, cs_blk[:, h0:h0 + 1], cs_blk[:, h0 + 1:h0 + 2])
                cl0 = cs_blk[CHUNK - 1:CHUNK, h0:h0 + 1]
                cl1 = cs_blk[CHUNK - 1:CHUNK, h0 + 1:h0 + 2]
                xdt = xp * dtp
                seg = jnp.where(causal, jnp.exp(jnp.minimum(csp - csr[p:p + 1, :], 0.0)), 0.0)
                scores = (cbw * seg).astype(BF16)
                bd = jnp.concatenate([jnp.where(left, xdt, 0.0), jnp.where(left, 0.0, xdt)], axis=0)
                y = _dot(scores, bd.astype(BF16))
                sp = state[p]
                sb = sp.astype(BF16)
                xdt_t = jnp.concatenate([xdt, zpad_f], axis=0).T.astype(BF16)
                yo, st = [], []
                for hh in range(2):
                    ecs = jnp.exp(cs_blk[:, h0 + hh:h0 + hh + 1])
                    cl = cl0 if hh == 0 else cl1
                    dec = jnp.exp(cl - cs_blk[:, h0 + hh:h0 + hh + 1])
                    yo.append(_dot_nt((cf * ecs).astype(BF16), sb))
                    st.append(_dot(xdt_t, jnp.concatenate([(bf * dec).astype(BF16), zpad_b], axis=0)))
                y = y + jnp.where(left, yo[0], yo[1])
                y = y + xp * dsk_ref[:, p * hp:(p + 1) * hp]
                state[p] = sp * jnp.where(top, jnp.exp(cl0), jnp.exp(cl1)) + jnp.where(top, st[0], st[1])
                ys.append(y)
        yf = jnp.concatenate(ys, axis=1)
        yg = yf * _silu(z_ref[pl.ds(base, CHUNK), :].astype(F32))
        out = yg * lax.rsqrt(jnp.mean(yg * yg, axis=-1, keepdims=True) + EPS) * nw_ref[...]
        o_ref[pl.ds(base, CHUNK), :] = out.astype(BF16)

        @pl.when(emit_ref[gi] >= 0)
        def _():
            sout_ref[emit_ref[gi]] = state[...]

        return carry

    lax.fori_loop(0, nch, chunk, 0)


def _ssd(proj, cache_slots, s0_slots, conv_w, conv_b, a_log, dt_bias, d_skip, norm_w, tables, n_out_blk):
    t = proj.shape[0]
    l = MIX_TILE
    nch = TILE_CHUNKS
    hp = 2 * SSD_HEADDIM
    kind, emit, in_blk, out_blk = tables
    cw = jnp.concatenate([conv_w, conv_b[None, :], jnp.zeros((SUBLANES - CONV_K - 1, SSD_CONV), F32)], axis=0)
    a_neg = -jnp.exp(a_log.astype(F32))
    pre = jnp.zeros((SM_DT,), F32)
    post = jnp.zeros((LANES - SM_DT - SSD_HEADS,), F32)
    parr = jnp.zeros((SUBLANES, LANES), F32)
    parr = parr.at[0].set(jnp.concatenate([pre, a_neg, post]))
    parr = parr.at[1].set(jnp.concatenate([pre, dt_bias.astype(F32), post]))
    parc = jnp.broadcast_to(jnp.concatenate([a_neg, dt_bias.astype(F32)])[:, None], (2 * SSD_HEADS, LANES))
    dsk = jnp.repeat(d_skip.astype(F32), SSD_HEADDIM).reshape(1, SSD_WIDTH)
    grid_spec = pltpu.PrefetchScalarGridSpec(
        num_scalar_prefetch=4,
        grid=(t // l,),
        in_specs=[
            pl.BlockSpec((l, SSD_CONV), lambda i, *_: (i, P_XBC // SSD_CONV)),
            pl.BlockSpec((l, SSD_WIDTH), lambda i, *_: (i, P_ZC // SSD_WIDTH)),
            pl.BlockSpec((l, LANES), lambda i, *_: (i, P_SMALL // LANES)),
            pl.BlockSpec((nch, SUBLANES, SSD_CONV), lambda i, k, e, ib, ob: (ib[i], 0, 0)),
            pl.BlockSpec((nch, SSD_PAIRS, hp, SSD_STATE), lambda i, k, e, ib, ob: (ib[i], 0, 0, 0)),
            pl.BlockSpec((SUBLANES, SSD_CONV), lambda i, *_: (0, 0)),
            pl.BlockSpec((SUBLANES, LANES), lambda i, *_: (0, 0)),
            pl.BlockSpec((2 * SSD_HEADS, LANES), lambda i, *_: (0, 0)),
            pl.BlockSpec((1, SSD_WIDTH), lambda i, *_: (0, 0)),
            pl.BlockSpec((1, SSD_WIDTH), lambda i, *_: (0, 0)),
        ],
        out_specs=[
            pl.BlockSpec((l, SSD_WIDTH), lambda i, *_: (i, 0)),
            pl.BlockSpec((nch, SSD_PAIRS, hp, SSD_STATE), lambda i, k, e, ib, ob: (ob[i], 0, 0, 0)),
        ],
        scratch_shapes=[
            pltpu.VMEM((SUBLANES, SSD_CONV), F32),
            pltpu.VMEM((l, SSD_CONV), F32),
            pltpu.VMEM((2 * SUBLANES, SSD_CONV), F32),
            pltpu.VMEM((SSD_PAIRS, hp, SSD_STATE), F32),
            pltpu.VMEM((l, LANES), F32),
            pltpu.VMEM((l, LANES), F32),
            pltpu.VMEM((nch, SSD_PAIRS, hp), F32),
        ],
    )
    return pl.pallas_call(
        _ssd_kernel,
        grid_spec=grid_spec,
        out_shape=[jax.ShapeDtypeStruct((t, SSD_WIDTH), BF16),
                   jax.ShapeDtypeStruct((n_out_blk * nch, SSD_PAIRS, hp, SSD_STATE), F32)],
        compiler_params=_cparams(("arbitrary",)),
        name="ssd_mixer",
    )(jnp.asarray(kind), jnp.asarray(emit), jnp.asarray(in_blk), jnp.asarray(out_blk),
      proj, proj, proj, cache_slots, s0_slots, cw, parr, parc,
      norm_w.reshape(1, SSD_WIDTH).astype(F32), dsk)


S5_SB = 2
S5_LB = 512


def _s5_kernel(kind_ref, u_ref, h0r_ref, h0i_ref, perm_ref, permt_ref, bre_ref, bim_ref, cre_ref, cim_ref,
               ar_ref, ai_ref, dsk_ref, wglu_ref, bglu_ref,
               o_ref, hfr_ref, hfi_ref,
               bur, bui, pre, pim, cr, ci, inr, ini):
    i = pl.program_id(0)
    l = u_ref.shape[0]
    nch = l // CHUNK
    n = S5_N
    usb = S5_WIDTH // S5_SB
    nsb = n // S5_SB

    @pl.when(i == 0)
    def _():
        pre[0:1, :] = ar_ref[...]
        pim[0:1, :] = ai_ref[...]

        def pw(t, carry):
            pr = pre[pl.ds(t - 1, 1), :]
            pi = pim[pl.ds(t - 1, 1), :]
            pre[pl.ds(t, 1), :] = pr * ar_ref[...] - pi * ai_ref[...]
            pim[pl.ds(t, 1), :] = pr * ai_ref[...] + pi * ar_ref[...]
            return carry

        lax.fori_loop(1, CHUNK, pw, 0)
        cr[...] = jnp.zeros((1, n), F32)
        ci[...] = jnp.zeros((1, n), F32)

    up = _dot(perm_ref[...], u_ref[...].astype(BF16)).astype(BF16)
    for sb in range(S5_SB):
        us = up[:, sb * usb:(sb + 1) * usb]
        bur[:, sb * nsb:(sb + 1) * nsb] = _dot(us, bre_ref[sb])
        bui[:, sb * nsb:(sb + 1) * nsb] = _dot(us, bim_ref[sb])

    for c0 in range(0, n, S5_LB):
        a_r = jnp.broadcast_to(ar_ref[:, c0:c0 + S5_LB], (nch, S5_LB))
        a_i = jnp.broadcast_to(ai_ref[:, c0:c0 + S5_LB], (nch, S5_LB))

        def step(t, carry):
            hr, hi = carry
            r0 = pl.multiple_of(t * nch, nch)
            nr = a_r * hr - a_i * hi + bur[pl.ds(r0, nch), c0:c0 + S5_LB]
            ni = a_r * hi + a_i * hr + bui[pl.ds(r0, nch), c0:c0 + S5_LB]
            bur[pl.ds(r0, nch), c0:c0 + S5_LB] = nr
            bui[pl.ds(r0, nch), c0:c0 + S5_LB] = ni
            return nr, ni

        z = jnp.zeros((nch, S5_LB), F32)
        lax.fori_loop(0, CHUNK, step, (z, z))

    a64r = pre[CHUNK - 1:CHUNK, :]
    a64i = pim[CHUNK - 1:CHUNK, :]
    c_r = cr[...]
    c_i = ci[...]
    for s in range(nch):
        start = kind_ref[i * nch + s] == 1
        i_r = jnp.where(start, h0r_ref[s:s + 1, :], c_r)
        i_i = jnp.where(start, h0i_ref[s:s + 1, :], c_i)
        inr[s:s + 1, :] = i_r
        ini[s:s + 1, :] = i_i
        e_r = bur[l - nch + s:l - nch + s + 1, :]
        e_i = bui[l - nch + s:l - nch + s + 1, :]
        c_r = a64r * i_r - a64i * i_i + e_r
        c_i = a64r * i_i + a64i * i_r + e_i
        hfr_ref[s:s + 1, :] = c_r
        hfi_ref[s:s + 1, :] = c_i
    cr[...] = c_r
    ci[...] = c_i

    for c0 in range(0, n, S5_LB):
        n_r = inr[:, c0:c0 + S5_LB]
        n_i = ini[:, c0:c0 + S5_LB]

        def fix(t, carry):
            r0 = pl.multiple_of(t * nch, nch)
            p_r = pre[pl.ds(t, 1), c0:c0 + S5_LB]
            p_i = pim[pl.ds(t, 1), c0:c0 + S5_LB]
            bur[pl.ds(r0, nch), c0:c0 + S5_LB] += p_r * n_r - p_i * n_i
            bui[pl.ds(r0, nch), c0:c0 + S5_LB] += p_r * n_i + p_i * n_r
            return carry

        lax.fori_loop(0, CHUNK, fix, 0)

    ys = []
    for sb in range(S5_SB):
        hr = bur[:, sb * nsb:(sb + 1) * nsb].astype(BF16)
        hi = bui[:, sb * nsb:(sb + 1) * nsb].astype(BF16)
        ys.append(_dot(hr, cre_ref[sb]) - _dot(hi, cim_ref[sb]))
    yp = jnp.concatenate(ys, axis=1)
    y_hi = yp.astype(BF16)
    r1 = yp - y_hi.astype(F32)
    y_mid = r1.astype(BF16)
    y_lo = (r1 - y_mid.astype(F32)).astype(BF16)
    pt = permt_ref[...]
    y = (_dot(pt, y_hi) + _dot(pt, y_mid)) + _dot(pt, y_lo)
    y = y + u_ref[...] * dsk_ref[...]
    y = y * (0.5 * (1.0 + jnp.tanh(math.sqrt(2.0 / math.pi) * (y + 0.044715 * (y * y * y)))))
    out = y * jax.nn.sigmoid(_dot(y.astype(BF16), wglu_ref[...]) + bglu_ref[...])
    o_ref[...] = out.astype(BF16)


def _s5_tables(a_re, a_im, b_re, b_im, c_re, c_im, log_dt):
    a_re, a_im = a_re.astype(F32), a_im.astype(F32)
    dt = jnp.exp(log_dt.astype(F32))[:, None]
    mag = jnp.exp(dt * a_re)
    abar_re, abar_im = mag * jnp.cos(dt * a_im), mag * jnp.sin(dt * a_im)
    den = a_re * a_re + a_im * a_im
    num_re, num_im = abar_re - 1.0, abar_im
    zoh_re = (num_re * a_re + num_im * a_im) / den
    zoh_im = (num_im * a_re - num_re * a_im) / den
    b_re, b_im = b_re.astype(F32), b_im.astype(F32)
    bbar_re = zoh_re[..., None] * b_re - zoh_im[..., None] * b_im
    bbar_im = zoh_re[..., None] * b_im + zoh_im[..., None] * b_re
    gsb = S5_GROUPS // S5_SB
    eye = jnp.eye(gsb, dtype=F32)

    def bblk(b):
        b = b.reshape(S5_SB, gsb, S5_STATE, S5_GROUP)
        return jnp.einsum('sgpc,gh->sgchp', b, eye).reshape(S5_SB, gsb * S5_GROUP, gsb * S5_STATE).astype(BF16)

    def cblk(c):
        c = c.astype(F32).reshape(S5_SB, gsb, S5_GROUP, S5_STATE)
        return jnp.einsum('sgcp,gh->sgphc', c, eye).reshape(S5_SB, gsb * S5_STATE, gsb * S5_GROUP).astype(BF16)

    return (abar_re.reshape(1, S5_N), abar_im.reshape(1, S5_N),
            bblk(bbar_re), bblk(bbar_im), cblk(c_re), cblk(c_im))


def _s5_perm():
    r_new = np.arange(MIX_TILE)
    r_old = (r_new % TILE_CHUNKS) * CHUNK + r_new // TILE_CHUNKS
    p = np.zeros((MIX_TILE, MIX_TILE), np.float32)
    p[r_new, r_old] = 1.0
    return jnp.asarray(p, BF16), jnp.asarray(p.T, BF16)


def _s5(proj, h0r, h0i, tabs, d_skip, w_glu, b_glu, kind):
    t = proj.shape[0]
    l = MIX_TILE
    nch = TILE_CHUNKS
    abr, abi, bre, bim, cre, cim = tabs
    perm, permt = _s5_perm()
    full2 = lambda a: pl.BlockSpec(a.shape, lambda i, *_: (0, 0))
    full3 = lambda a: pl.BlockSpec(a.shape, lambda i, *_: (0, 0, 0))
    dsk = d_skip.astype(F32).reshape(1, S5_WIDTH)
    wg = w_glu.astype(BF16)
    bg = b_glu.astype(F32).reshape(1, S5_WIDTH)
    grid_spec = pltpu.PrefetchScalarGridSpec(
        num_scalar_prefetch=1,
        grid=(t // l,),
        in_specs=[
            pl.BlockSpec((l, S5_WIDTH), lambda i, *_: (i, P_UB // S5_WIDTH)),
            pl.BlockSpec((nch, S5_N), lambda i, *_: (i, 0)),
            pl.BlockSpec((nch, S5_N), lambda i, *_: (i, 0)),
            full2(perm), full2(permt), full3(bre), full3(bim), full3(cre), full3(cim),
            full2(abr), full2(abi), full2(dsk), full2(wg), full2(bg),
        ],
        out_specs=[
            pl.BlockSpec((l, S5_WIDTH), lambda i, *_: (i, 0)),
            pl.BlockSpec((nch, S5_N), lambda i, *_: (i, 0)),
            pl.BlockSpec((nch, S5_N), lambda i, *_: (i, 0)),
        ],
        scratch_shapes=[
            pltpu.VMEM((l, S5_N), F32), pltpu.VMEM((l, S5_N), F32),
            pltpu.VMEM((CHUNK, S5_N), F32), pltpu.VMEM((CHUNK, S5_N), F32),
            pltpu.VMEM((1, S5_N), F32), pltpu.VMEM((1, S5_N), F32),
            pltpu.VMEM((nch, S5_N), F32), pltpu.VMEM((nch, S5_N), F32),
        ],
    )
    nseg = t // CHUNK
    return pl.pallas_call(
        _s5_kernel,
        grid_spec=grid_spec,
        out_shape=[jax.ShapeDtypeStruct((t, S5_WIDTH), BF16),
                   jax.ShapeDtypeStruct((nseg, S5_N), F32),
                   jax.ShapeDtypeStruct((nseg, S5_N), F32)],
        compiler_params=_cparams(("arbitrary",)),
        name="s5_mixer",
    )(jnp.asarray(kind), proj, h0r, h0i, perm, permt, bre, bim, cre, cim, abr, abi, dsk, wg, bg)


MIXOUT_TM = 512
R_GRP = 0
R_EXP = SUBLANES
NEG_BIG = -1e30


def _mixout_kernel(oa_ref, ob_ref, oc_ref, w_ref, x_ref, nw_ref, wr_ref, rb_ref, x1_ref, h2_ref, lg_ref):
    acc = _dot(oa_ref[...], w_ref[0:GDN_KD, :])
    acc = acc + _dot(ob_ref[...], w_ref[GDN_KD:GDN_KD + S5_WIDTH, :])
    acc = acc + _dot(oc_ref[...], w_ref[GDN_KD + S5_WIDTH:, :])
    x1 = x_ref[...] + acc
    x1_ref[...] = x1
    h = x1 * lax.rsqrt(jnp.mean(x1 * x1, axis=-1, keepdims=True) + EPS) * nw_ref[...]
    hb = h.astype(BF16)
    h2_ref[...] = _pack_bf16_pairs(hb)
    lg_ref[...] = _dot(hb, wr_ref[...]) + rb_ref[...]


def _mixout(oa, ob, oc, w_out, x, nw, wr, rb):
    t, d = x.shape
    tm = MIXOUT_TM
    row = lambda w: pl.BlockSpec((tm, w), lambda i: (i, 0))
    full = lambda a: pl.BlockSpec(a.shape, lambda i: (0, 0))
    return pl.pallas_call(
        _mixout_kernel,
        grid=(t // tm,),
        in_specs=[row(GDN_KD), row(S5_WIDTH), row(SSD_WIDTH), full(w_out), row(d), full(nw), full(wr), full(rb)],
        out_specs=[row(d), row(d // 2), row(LANES)],
        out_shape=[jax.ShapeDtypeStruct((t, d), F32), jax.ShapeDtypeStruct((t, d // 2), U32),
                   jax.ShapeDtypeStruct((t, LANES), F32)],
        compiler_params=_cparams(("parallel",)),
        name="mix_out",
    )(oa, ob, oc, w_out, x, nw, wr, rb)


def _router_weights(rg_w, rg_b, re_w, re_b):
    d = rg_w.shape[0]
    wr = jnp.concatenate([rg_w, jnp.zeros((d, R_EXP - N_GROUPS), F32), re_w,
                          jnp.zeros((d, LANES - R_EXP - N_EXPERTS), F32)], axis=1).astype(BF16)
    rb = jnp.concatenate([rg_b.astype(F32), jnp.full((R_EXP - N_GROUPS,), NEG_BIG, F32), re_b.astype(F32),
                          jnp.zeros((LANES - R_EXP - N_EXPERTS,), F32)]).reshape(1, LANES)
    return wr, rb


ROUTE_TM = 512


def _router_kernel(lg_ref, tri_ref, idx_ref, gate_ref, cnt_ref, run):
    i = pl.program_id(0)
    tm = lg_ref.shape[0]

    @pl.when(i == 0)
    def _():
        run[...] = jnp.zeros(run.shape, F32)

    lt = lg_ref[...].T
    row8 = lax.broadcasted_iota(I32, (SUBLANES, tm), 0)
    grp = lt[R_GRP:R_GRP + SUBLANES, :]
    gm = jnp.max(grp, axis=0, keepdims=True)
    gp_top = 1.0 / jnp.sum(jnp.exp(grp - gm), axis=0, keepdims=True)
    g_top = jnp.min(jnp.where(grp == gm, row8, SUBLANES), axis=0, keepdims=True)
    ing = jnp.zeros((EPG, tm), F32)
    for g in range(N_GROUPS):
        ing = jnp.where(g_top == g, lt[R_EXP + g * EPG:R_EXP + (g + 1) * EPG, :], ing)
    em = jnp.max(ing, axis=0, keepdims=True)
    ee = jnp.exp(ing - em)
    p = ee / jnp.sum(ee, axis=0, keepdims=True)
    v1 = jnp.max(p, axis=0, keepdims=True)
    i1 = jnp.min(jnp.where(p == v1, row8, EPG), axis=0, keepdims=True)
    p2 = jnp.where(row8 == i1, -1.0, p)
    v2 = jnp.max(p2, axis=0, keepdims=True)
    i2 = jnp.min(jnp.where(p2 == v2, row8, EPG), axis=0, keepdims=True)
    den = v1 + v2
    gate1 = gp_top * v1 / den
    gate2 = gp_top * v2 / den
    e1 = g_top * EPG + i1
    e2 = g_top * EPG + i2

    erow = lax.broadcasted_iota(I32, (N_EXPERTS, tm), 0)
    hit1 = erow == e1
    hit2 = erow == e2
    oh = jnp.where(hit1 | hit2, 1.0, 0.0)
    before = _dot(oh.astype(BF16), tri_ref[...]) + run[:, 0:1]
    rank1 = jnp.sum(jnp.where(hit1, before, 0.0), axis=0, keepdims=True).astype(I32)
    rank2 = jnp.sum(jnp.where(hit2, before, 0.0), axis=0, keepdims=True).astype(I32)
    run[...] = run[...] + jnp.sum(oh, axis=1, keepdims=True)
    cnt_ref[...] = run[...].astype(I32)

    zi = jnp.zeros((SUBLANES - 4, tm), I32)
    idx_ref[...] = jnp.concatenate([e1, e2, rank1, rank2, zi], axis=0)
    r128 = lax.broadcasted_iota(I32, (LANES, tm), 0)
    gt = jnp.where(r128 == 0, gate1, jnp.where(r128 == 1, gate2, 0.0))
    gate_ref[...] = gt.T


def _router(logits):
    t = logits.shape[0]
    tm = ROUTE_TM
    tri = jnp.asarray(np.triu(np.ones((tm, tm), np.float32), 1), BF16)
    return pl.pallas_call(
        _router_kernel,
        grid=(t // tm,),
        in_specs=[pl.BlockSpec((tm, LANES), lambda i: (i, 0)),
                  pl.BlockSpec((tm, tm), lambda i: (0, 0))],
        out_specs=[pl.BlockSpec((SUBLANES, tm), lambda i: (0, i)),
                   pl.BlockSpec((tm, LANES), lambda i: (i, 0)),
                   pl.BlockSpec((N_EXPERTS, LANES), lambda i: (0, 0))],
        out_shape=[jax.ShapeDtypeStruct((SUBLANES, t), I32),
                   jax.ShapeDtypeStruct((t, LANES), F32),
                   jax.ShapeDtypeStruct((N_EXPERTS, LANES), I32)],
        scratch_shapes=[pltpu.VMEM((N_EXPERTS, LANES), F32)],
        compiler_params=_cparams(("arbitrary",)),
        name="router",
    )(logits, tri)


MOE_BLOCK = 512
MOE_BLOCK_SHIFT = 9
DMA_UNROLL = 16
TOP_K = 2
DISPATCH_TM = 512
COMBINE_TM = 256


def _dispatch_kernel(pstart_ref, cnt_ref, nv_ref, idx_ref, h2_ref, xs_ref, zbuf, sem):
    i = pl.program_id(0)
    tm = idx_ref.shape[1]
    nb = xs_ref.shape[0] // MOE_BLOCK

    @pl.when(i == 0)
    def _():
        zbuf[...] = jnp.zeros(zbuf.shape, zbuf.dtype)

        def pad_copy(e, r):
            return pltpu.make_async_copy(zbuf.at[pl.ds(0, 1)], xs_ref.at[pl.ds(pstart_ref[e] + r, 1)], sem)

        def per_expert(e, carry):
            n = cnt_ref[e]
            padded = ((n + MOE_BLOCK - 1) >> MOE_BLOCK_SHIFT) << MOE_BLOCK_SHIFT
            lax.fori_loop(n, padded, lambda r, c: (pad_copy(e, r).start(), c)[1], 0)
            lax.fori_loop(n, padded, lambda r, c: (pad_copy(e, r).wait(), c)[1], 0)
            return carry

        lax.fori_loop(0, N_EXPERTS, per_expert, 0)

        def blk_copy(b):
            return pltpu.make_async_copy(zbuf, xs_ref.at[pl.ds(b * MOE_BLOCK, MOE_BLOCK)], sem)

        lax.fori_loop(nv_ref[0], nb, lambda b, c: (blk_copy(b).start(), c)[1], 0)
        lax.fori_loop(nv_ref[0], nb, lambda b, c: (blk_copy(b).wait(), c)[1], 0)

    def copy(t, k):
        slot = idx_ref[k, t]
        return pltpu.make_async_copy(h2_ref.at[pl.ds(t, 1)], xs_ref.at[pl.ds(slot, 1)], sem)

    def issue(t, carry):
        copy(t, 0).start(priority=0)
        copy(t, 1).start(priority=1)
        return carry

    def drain(t, carry):
        copy(t, 0).wait()
        copy(t, 1).wait()
        return carry

    lax.fori_loop(0, tm, issue, 0, unroll=DMA_UNROLL)
    lax.fori_loop(0, tm, drain, 0, unroll=DMA_UNROLL)


def _dispatch(pad_start, counts, n_valid, idx, h2, n_slots):
    t, d = h2.shape
    tm = DISPATCH_TM
    grid_spec = pltpu.PrefetchScalarGridSpec(
        num_scalar_prefetch=3,
        grid=(t // tm,),
        in_specs=[pl.BlockSpec((SUBLANES, tm), lambda i, *_: (0, i), memory_space=pltpu.SMEM),
                  pl.BlockSpec((tm, d), lambda i, *_: (i, 0))],
        out_specs=pl.BlockSpec(memory_space=pl.ANY),
        scratch_shapes=[pltpu.VMEM((MOE_BLOCK, d), h2.dtype), pltpu.SemaphoreType.DMA(())],
    )
    return pl.pallas_call(
        _dispatch_kernel,
        grid_spec=grid_spec,
        out_shape=jax.ShapeDtypeStruct((n_slots, d), h2.dtype),
        compiler_params=_cparams(("arbitrary",)),
        name="moe_dispatch",
    )(pad_start, counts, n_valid, idx, h2)


def _expert_kernel(be_ref, nv_ref, xs_ref, wg_ref, wu_ref, wd_ref, ys_ref, wg_s, wu_s, wd_s):
    b = pl.program_id(0)
    valid = b < nv_ref[0]

    @pl.when(valid & ((b == 0) | (be_ref[b] != be_ref[jnp.maximum(b - 1, 0)])))
    def _():
        wg_s[...] = wg_ref[0].astype(BF16)
        wu_s[...] = wu_ref[0].astype(BF16)
        wd_s[...] = wd_ref[0].astype(BF16)

    @pl.when(valid)
    def _():
        x_lo, x_hi = _unpack_bf16_pairs(xs_ref[...])
        kh = wg_s.shape[0] // 2
        g = _dot(x_lo, wg_s[0:kh, :]) + _dot(x_hi, wg_s[kh:, :])
        u = _dot(x_lo, wu_s[0:kh, :]) + _dot(x_hi, wu_s[kh:, :])
        h = (_silu(g) * u).astype(BF16)
        ys_ref[...] = _dot(h, wd_s[...])

    @pl.when(jnp.logical_not(valid))
    def _():
        ys_ref[...] = jnp.zeros(ys_ref.shape, F32)


def _experts(block_expert, n_valid, xs, wg, wu, wd, layer):
    n_slots, dp = xs.shape
    d, de = wg.shape[2], wg.shape[3]
    assert d == 2 * dp
    nb = n_slots // MOE_BLOCK
    blk = lambda b, be, nv: (jnp.minimum(b, nv[0] - 1), 0)
    wsel = lambda b, be, nv: (layer, be[jnp.minimum(b, nv[0] - 1)], 0, 0)
    grid_spec = pltpu.PrefetchScalarGridSpec(
        num_scalar_prefetch=2,
        grid=(nb,),
        in_specs=[pl.BlockSpec((MOE_BLOCK, dp), blk),
                  pl.BlockSpec((None, 1, d, de), wsel),
                  pl.BlockSpec((None, 1, d, de), wsel),
                  pl.BlockSpec((None, 1, de, d), wsel)],
        out_specs=pl.BlockSpec((MOE_BLOCK, d), lambda b, be, nv: (b, 0)),
        scratch_shapes=[pltpu.VMEM((d, de), BF16), pltpu.VMEM((d, de), BF16), pltpu.VMEM((de, d), BF16)],
    )
    return pl.pallas_call(
        _expert_kernel,
        grid_spec=grid_spec,
        out_shape=jax.ShapeDtypeStruct((n_slots, d), F32),
        compiler_params=_cparams(("arbitrary",)),
        name="moe_experts",
    )(block_expert, n_valid, xs, wg, wu, wd)


def _combine_kernel(idx_ref, idxn_ref, x1_ref, gate_ref, ys_ref, nw_ref, *rest, n_first):
    if n_first is None:
        out_ref, ybuf, sem = rest
    else:
        out_a_ref, out_b_ref, ybuf, sem = rest
    tm = x1_ref.shape[0]
    step = pl.program_id(0)
    cur = step % 2

    def copy(ref, buf, t, k):
        return pltpu.make_async_copy(ys_ref.at[pl.ds(ref[k, t], 1)], ybuf.at[buf, k, pl.ds(t, 1)], sem.at[buf])

    def issue(ref, buf):
        def body(t, carry):
            copy(ref, buf, t, 0).start(priority=0)
            copy(ref, buf, t, 1).start(priority=1)
            return carry
        lax.fori_loop(0, tm, body, 0, unroll=DMA_UNROLL)

    @pl.when(step == 0)
    def _():
        issue(idx_ref, 0)

    @pl.when(step + 1 < pl.num_programs(0))
    def _():
        issue(idxn_ref, 1 - cur)

    def drain(t, carry):
        copy(idx_ref, cur, t, 0).wait()
        copy(idx_ref, cur, t, 1).wait()
        return carry

    lax.fori_loop(0, tm, drain, 0, unroll=DMA_UNROLL)
    g = gate_ref[...]
    y = ybuf[cur, 0] * g[:, 0:1] + ybuf[cur, 1] * g[:, 1:2]
    x2 = x1_ref[...] + y
    if n_first is None:
        out_ref[...] = x2
    else:
        x2 = x2 * lax.rsqrt(jnp.mean(x2 * x2, axis=-1, keepdims=True) + EPS) * nw_ref[...]
        i = pl.program_id(0)

        @pl.when(i < n_first)
        def _():
            out_a_ref[...] = x2

        @pl.when(i >= n_first)
        def _():
            out_b_ref[...] = x2


def _combine(idx, x1, gates, ys, nw, t_first):
    t, d = x1.shape
    tm = COMBINE_TM
    row = pl.BlockSpec((tm, d), lambda i, *_: (i, 0))
    if t_first is None:
        n_first = None
        out_specs = row
        out_shape = jax.ShapeDtypeStruct((t, d), F32)
    else:
        assert t_first % tm == 0
        n_first = t_first // tm
        out_specs = [pl.BlockSpec((tm, d), lambda i, *_: (jnp.minimum(i, n_first - 1), 0)),
                     pl.BlockSpec((tm, d), lambda i, *_: (jnp.maximum(i - n_first, 0), 0))]
        out_shape = [jax.ShapeDtypeStruct((t_first, d), F32), jax.ShapeDtypeStruct((t - t_first, d), F32)]
    grid_spec = pltpu.PrefetchScalarGridSpec(
        num_scalar_prefetch=0,
        grid=(t // tm,),
        in_specs=[pl.BlockSpec((SUBLANES, tm), lambda i, *_: (0, i), memory_space=pltpu.SMEM),
                  pl.BlockSpec((SUBLANES, tm), lambda i, *_: (0, jnp.minimum(i + 1, t // tm - 1)),
                               memory_space=pltpu.SMEM),
                  row,
                  pl.BlockSpec((tm, LANES), lambda i, *_: (i, 0)),
                  pl.BlockSpec(memory_space=pl.ANY),
                  pl.BlockSpec((1, d), lambda i, *_: (0, 0))],
        out_specs=out_specs,
        scratch_shapes=[pltpu.VMEM((2, TOP_K, tm, d), F32), pltpu.SemaphoreType.DMA((2,))],
    )
    return pl.pallas_call(
        functools.partial(_combine_kernel, n_first=n_first),
        grid_spec=grid_spec,
        out_shape=out_shape,
        compiler_params=_cparams(("arbitrary",)),
        name="moe_combine",
    )(idx, idx, x1, gates, ys, nw)


def _moe(x1, h2, logits, wg, wu, wd, layer, norm_final, t_first):
    t, d = x1.shape
    idx, gates, cnt = _router(logits)
    counts = cnt[:, 0]
    padded = ((counts + MOE_BLOCK - 1) >> MOE_BLOCK_SHIFT) << MOE_BLOCK_SHIFT
    pad_end = jnp.cumsum(padded)
    pad_start = (pad_end - padded).astype(I32)
    nb = (2 * t + N_EXPERTS * (MOE_BLOCK - 1) + MOE_BLOCK - 1) // MOE_BLOCK
    n_valid = (pad_end[-1] >> MOE_BLOCK_SHIFT).astype(I32).reshape(1)
    starts = jnp.arange(nb, dtype=I32) * MOE_BLOCK
    block_expert = jnp.minimum(jnp.sum((pad_end[None, :] <= starts[:, None]).astype(I32), axis=1),
                               N_EXPERTS - 1).astype(I32)
    hit = idx[None, 0:TOP_K, :] == jnp.arange(N_EXPERTS, dtype=I32)[:, None, None]
    slots = jnp.sum(jnp.where(hit, pad_start[:, None, None], 0), axis=0) + idx[TOP_K:2 * TOP_K]
    slots = jnp.concatenate([slots, jnp.zeros((SUBLANES - TOP_K, t), I32)], axis=0)
    xs = _dispatch(pad_start, counts, n_valid, slots, h2, nb * MOE_BLOCK)
    ys = _experts(block_expert, n_valid, xs, wg, wu, wd, layer)
    return _combine(slots, x1, gates, ys, norm_final, t_first)


def _stream_ends(nbp, seq, nbs, dseq):
    ends = [(b + 1) * seq for b in range(nbp)] + [nbp * seq + (s + 1) * dseq for s in range(nbs)]
    return np.asarray(ends)


def kernel(x_prompt, x_sample, cache_conv_gdn, state_gdn, state_s5, cache_conv_ssd, state_ssd, norm_mix, w_in, gdn_conv_w, gdn_a_log, gdn_dt_bias, gdn_norm, s5_a_re, s5_a_im, s5_b_re, s5_b_im, s5_c_re, s5_c_im, s5_log_dt, s5_d, s5_w_glu, s5_b_glu, ssd_conv_w, ssd_conv_b, ssd_a_log, ssd_dt_bias, ssd_d, ssd_norm, w_out, norm_ffn, router_group_w, router_group_b, router_expert_w, router_expert_b, expert_w_gate, expert_w_up, expert_w_down, norm_final):
    nbp, seq, d = x_prompt.shape
    nbs, dseq, _ = x_sample.shape
    depth = w_in.shape[0]
    tp = nbp * seq
    t = tp + nbs * dseq
    x = jnp.concatenate([x_prompt.reshape(tp, d), x_sample.reshape(nbs * dseq, d)], axis=0)

    tables = _chunk_tables(nbp, seq, nbs, dseq)
    kind = tables[0]
    n_out_blk = nbp + nbs // TILE_CHUNKS
    ends = _stream_ends(nbp, seq, nbs, dseq)
    tail_rows = (ends[:, None] + np.arange(-(CONV_K - 1), 0)[None, :]).reshape(-1)
    end_seg = ends // CHUNK - 1
    ncp = tp // CHUNK
    state_rows = np.concatenate([np.arange(nbp) * TILE_CHUNKS, nbp * TILE_CHUNKS + np.arange(nbs)])

    new_conv_gdn, new_gdn, new_s5, new_conv_ssd, new_ssd = [], [], [], [], []
    for l in range(depth):
        proj = _proj(x, norm_mix[l].reshape(1, d).astype(F32), _rearrange_w_in(w_in[l]))

        oa, sg = _gdn(proj, _conv_cache_slots(cache_conv_gdn[l].astype(F32)),
                      _init_slots(state_gdn[l].astype(F32)), gdn_conv_w[l].astype(F32), gdn_a_log[l],
                      gdn_dt_bias[l], gdn_norm[l], tables, n_out_blk)

        h0 = state_s5[l].astype(F32).reshape(nbs, S5_N, 2)
        zeros_p = jnp.zeros((ncp, S5_N), F32)
        ob, hfr, hfi = _s5(proj, jnp.concatenate([zeros_p, h0[..., 0]], axis=0),
                           jnp.concatenate([zeros_p, h0[..., 1]], axis=0),
                           _s5_tables(s5_a_re[l], s5_a_im[l], s5_b_re[l], s5_b_im[l], s5_c_re[l], s5_c_im[l],
                                      s5_log_dt[l]),
                           s5_d[l], s5_w_glu[l], s5_b_glu[l], kind)

        oc, ss = _ssd(proj, _conv_cache_slots(cache_conv_ssd[l].astype(F32)),
                      _init_slots(state_ssd[l].astype(F32).reshape(nbs, SSD_PAIRS, 2 * SSD_HEADDIM, SSD_STATE)),
                      ssd_conv_w[l].astype(F32), ssd_conv_b[l].astype(F32), ssd_a_log[l], ssd_dt_bias[l],
                      ssd_d[l], ssd_norm[l], tables, n_out_blk)

        wr, rb = _router_weights(router_group_w[l].astype(F32), router_group_b[l],
                                 router_expert_w[l].astype(F32), router_expert_b[l])
        x1, h2, logits = _mixout(oa, ob, oc, w_out[l].astype(BF16), x,
                                 norm_ffn[l].reshape(1, d).astype(F32), wr, rb)
        x = _moe(x1, h2, logits, expert_w_gate, expert_w_up, expert_w_down, l,
                 norm_final.reshape(1, d).astype(F32), tp if l == depth - 1 else None)

        tails = proj[tail_rows]
        new_conv_gdn.append(tails[:, P_QKV:P_QKV + GDN_CONV].reshape(nbp + nbs, CONV_K - 1, GDN_CONV))
        new_conv_ssd.append(tails[:, P_XBC:P_XBC + SSD_CONV].reshape(nbp + nbs, CONV_K - 1, SSD_CONV))
        new_gdn.append(sg[state_rows])
        new_ssd.append(ss[state_rows].reshape(nbp + nbs, SSD_HEADS, SSD_HEADDIM, SSD_STATE))
        new_s5.append(jnp.stack([hfr[end_seg], hfi[end_seg]], axis=-1)
                      .reshape(nbp + nbs, S5_GROUPS, S5_STATE, 2))

    def split(parts):
        a = jnp.stack(parts)
        return a[:, :nbp], a[:, nbp:]

    cg_p, cg_s = split(new_conv_gdn)
    sg_p, sg_s = split(new_gdn)
    s5_p, s5_s = split(new_s5)
    cs_p, cs_s = split(new_conv_ssd)
    ss_p, ss_s = split(new_ssd)
    y_prompt = x[0].reshape(nbp, seq, d)
    y_sample = x[1].reshape(nbs, dseq, d)
    return (y_prompt, y_sample, cg_p, sg_p, s5_p, cs_p, ss_p, cg_s, sg_s, s5_s, cs_s, ss_s)
```
